```python
import math
import jax, jax.numpy as jnp
from jax import lax
import numpy as np

D_MODEL = 1024
BATCH = 8
SEQ = 16384
DEPTH = 2

N_A_LAYERS = DEPTH // 2
N_B_LAYERS = DEPTH - N_A_LAYERS
CONV_WIDTH = 3
N_HEADS = 8
HEAD_DIM = 64
ATT_WIDTH = N_HEADS * HEAD_DIM
Q_BLOCK = 128
RMS_EPS = 1e-6

kernel_name = "yoco_shortconv_stickbreaking"


def _rmsnorm(x, g):
    xf = x.astype(jnp.float32)
    y = xf * lax.rsqrt(jnp.mean(xf * xf, axis=-1, keepdims=True) + RMS_EPS)
    return (y * g.astype(jnp.float32)).astype(x.dtype)


def _causal_depthwise_conv(u, w):
    return lax.conv_general_dilated(
        u, w[:, None, :].astype(u.dtype),
        window_strides=(1,),
        padding=[(CONV_WIDTH - 1, 0)],
        dimension_numbers=("NWC", "WIO", "NWC"),
        feature_group_count=u.shape[-1])


def _short_conv_mixer(h, w_in, w_conv, w_out):
    u = h @ w_in
    b_gate, c_gate, xin, g = jnp.split(u, 4, axis=-1)
    y = b_gate * _causal_depthwise_conv(c_gate * xin, w_conv)
    return (y * jax.nn.silu(g)) @ w_out


def _stick_breaking_attention(q, k, v):
    seq = q.shape[2]
    n_blocks = seq // Q_BLOCK
    scale = HEAD_DIM ** -0.5
    qf = q.astype(jnp.float32)
    kf = k.astype(jnp.float32)
    vf = v.astype(jnp.float32)
    outs = []
    for blk in range(n_blocks):
        q0 = blk * Q_BLOCK
        kl = q0 + Q_BLOCK
        qb = qf[:, :, q0:kl]
        z = jnp.einsum("bhqd,bhkd->bhqk", qb, kf[:, :, :kl]) * scale
        q_pos = q0 + jnp.arange(Q_BLOCK)
        mask = jnp.arange(kl)[None, :] < q_pos[:, None]
        log_one_minus = jnp.where(mask, jax.nn.log_sigmoid(-z), 0.0)
        rev = lax.cumsum(log_one_minus, axis=3, reverse=True)
        weights = jnp.exp(jnp.where(mask, z + rev, -jnp.inf))
        outs.append(jnp.einsum("bhqk,bhkd->bhqd", weights, vf[:, :, :kl]))
    o = jnp.concatenate(outs, axis=2)
    return o.astype(q.dtype)


def _fwd_setup_inputs(seed: int = 0) -> dict:
    key = jax.random.key(seed)
    ks = jax.random.split(key, 12)
    D = D_MODEL
    s = D ** -0.5
    f32 = jnp.float32
    return {
        "x": jax.random.normal(ks[0], (BATCH, SEQ, D), f32),
        "norm_a": 1.0 + 0.02 * jax.random.normal(ks[1], (N_A_LAYERS, D), f32),
        "w_in_a": s * jax.random.normal(ks[2], (N_A_LAYERS, D, 4 * D), f32),
        "conv_a": (CONV_WIDTH ** -0.5) * jax.random.normal(ks[3], (N_A_LAYERS, CONV_WIDTH, D), f32),
        "w_out_a": s * jax.random.normal(ks[4], (N_A_LAYERS, D, D), f32),
        "norm_kv": 1.0 + 0.02 * jax.random.normal(ks[5], (D,), f32),
        "w_kv": s * jax.random.normal(ks[6], (D, 2 * ATT_WIDTH), f32),
        "norm_b": 1.0 + 0.02 * jax.random.normal(ks[7], (N_B_LAYERS, D), f32),
        "w_in_b": s * jax.random.normal(ks[8], (N_B_LAYERS, D, 2 * ATT_WIDTH), f32),
        "w_out_b": (ATT_WIDTH ** -0.5) * jax.random.normal(ks[9], (N_B_LAYERS, ATT_WIDTH, D), f32),
        "norm_f": 1.0 + 0.02 * jax.random.normal(ks[10], (D,), f32),
    }


def _fwd_reference(x, norm_a, w_in_a, conv_a, w_out_a, norm_kv, w_kv, norm_b, w_in_b, w_out_b, norm_f):
    bsz, seq, _ = x.shape
    k = v = None
    for layer in range(DEPTH):
        if layer < N_A_LAYERS:
            i = layer
            h = _rmsnorm(x, norm_a[i])
            x = x + _short_conv_mixer(h, w_in_a[i], conv_a[i], w_out_a[i])
            if layer == N_A_LAYERS - 1:
                kv = _rmsnorm(x, norm_kv) @ w_kv
                k, v = jnp.split(kv, 2, axis=-1)
                k = k.reshape(bsz, seq, N_HEADS, HEAD_DIM).transpose(0, 2, 1, 3)
                v = v.reshape(bsz, seq, N_HEADS, HEAD_DIM).transpose(0, 2, 1, 3)
        else:
            j = layer - N_A_LAYERS
            h = _rmsnorm(x, norm_b[j])
            q, g = jnp.split(h @ w_in_b[j], 2, axis=-1)
            q = q.reshape(bsz, seq, N_HEADS, HEAD_DIM).transpose(0, 2, 1, 3)
            o = _stick_breaking_attention(q, k, v)
            o = o.transpose(0, 2, 1, 3).reshape(bsz, seq, ATT_WIDTH)
            x = x + (o * jax.nn.silu(g)) @ w_out_b[j]
    return _rmsnorm(x, norm_f)


import jax as _jax
import jax.numpy as _jnp

TWIN_FORMAT = 'train_step'
FWD_PARAMS = ['x', 'norm_a', 'w_in_a', 'conv_a', 'w_out_a', 'norm_kv', 'w_kv', 'norm_b', 'w_in_b', 'w_out_b', 'norm_f']
TWIN_WEIGHTS = ['norm_a', 'w_in_a', 'conv_a', 'w_out_a', 'norm_kv', 'w_kv', 'norm_b', 'w_in_b', 'w_out_b', 'norm_f']
TWIN_DIFF_INPUT = 'x'
TWIN_INPUTS = ['x', 'norm_a', 'w_in_a', 'conv_a', 'w_out_a', 'norm_kv', 'w_kv', 'norm_b', 'w_in_b', 'w_out_b', 'norm_f', 'loss_target', 'm_norm_a', 'm_w_in_a', 'm_conv_a', 'm_w_out_a', 'm_norm_kv', 'm_w_kv', 'm_norm_b', 'm_w_in_b', 'm_w_out_b', 'm_norm_f', 'v_norm_a', 'v_w_in_a', 'v_conv_a', 'v_w_out_a', 'v_norm_kv', 'v_w_kv', 'v_norm_b', 'v_w_in_b', 'v_w_out_b', 'v_norm_f']
TWIN_OUTPUTS = ['loss', 'grad_x', 'grad_norm_a', 'grad_w_in_a', 'grad_conv_a', 'grad_w_out_a', 'grad_norm_kv', 'grad_w_kv', 'grad_norm_b', 'grad_w_in_b', 'grad_w_out_b', 'grad_norm_f', 'delta_norm_a', 'delta_w_in_a', 'delta_conv_a', 'delta_w_out_a', 'delta_norm_kv', 'delta_w_kv', 'delta_norm_b', 'delta_w_in_b', 'delta_w_out_b', 'delta_norm_f', 'new_m_norm_a', 'new_m_w_in_a', 'new_m_conv_a', 'new_m_w_out_a', 'new_m_norm_kv', 'new_m_w_kv', 'new_m_norm_b', 'new_m_w_in_b', 'new_m_w_out_b', 'new_m_norm_f', 'new_v_norm_a', 'new_v_w_in_a', 'new_v_conv_a', 'new_v_w_out_a', 'new_v_norm_kv', 'new_v_w_kv', 'new_v_norm_b', 'new_v_w_in_b', 'new_v_w_out_b', 'new_v_norm_f']
TWIN_LEAF_KINDS = {'loss': 'loss', 'grad_x': 'grad_x', 'grad_norm_a': 'grad_w', 'grad_w_in_a': 'grad_w', 'grad_conv_a': 'grad_w', 'grad_w_out_a': 'grad_w', 'grad_norm_kv': 'grad_w', 'grad_w_kv': 'grad_w', 'grad_norm_b': 'grad_w', 'grad_w_in_b': 'grad_w', 'grad_w_out_b': 'grad_w', 'grad_norm_f': 'grad_w', 'delta_norm_a': 'delta_w', 'delta_w_in_a': 'delta_w', 'delta_conv_a': 'delta_w', 'delta_w_out_a': 'delta_w', 'delta_norm_kv': 'delta_w', 'delta_w_kv': 'delta_w', 'delta_norm_b': 'delta_w', 'delta_w_in_b': 'delta_w', 'delta_w_out_b': 'delta_w', 'delta_norm_f': 'delta_w', 'new_m_norm_a': 'new_m', 'new_m_w_in_a': 'new_m', 'new_m_conv_a': 'new_m', 'new_m_w_out_a': 'new_m', 'new_m_norm_kv': 'new_m', 'new_m_w_kv': 'new_m', 'new_m_norm_b': 'new_m', 'new_m_w_in_b': 'new_m', 'new_m_w_out_b': 'new_m', 'new_m_norm_f': 'new_m', 'new_v_norm_a': 'new_v', 'new_v_w_in_a': 'new_v', 'new_v_conv_a': 'new_v', 'new_v_w_out_a': 'new_v', 'new_v_norm_kv': 'new_v', 'new_v_w_kv': 'new_v', 'new_v_norm_b': 'new_v', 'new_v_w_in_b': 'new_v', 'new_v_w_out_b': 'new_v', 'new_v_norm_f': 'new_v'}


def _forward(args):
    return _fwd_reference(*[args[k] for k in FWD_PARAMS])


def _output_shape():
    def fwd():
        inp = _fwd_setup_inputs(0)
        return _fwd_reference(*[inp[k] for k in FWD_PARAMS])
    out = _jax.eval_shape(fwd)
    return out.shape, out.dtype

N_MICROBATCH = 1
ADAM_LR = 0.001
ADAM_B1 = 0.9
ADAM_B2 = 0.999
ADAM_EPS = 1e-08
ADAM_WD = 0.01
ADAM_STEP = 10
PER_EXAMPLE_BATCH_AXIS = {'x': 0, 'loss_target': 0}
SHARED_INPUTS = []
_WEIGHT_DTYPES = {'norm_a': _jnp.float32, 'w_in_a': _jnp.float32, 'conv_a': _jnp.float32, 'w_out_a': _jnp.float32, 'norm_kv': _jnp.float32, 'w_kv': _jnp.float32, 'norm_b': _jnp.float32, 'w_in_b': _jnp.float32, 'w_out_b': _jnp.float32, 'norm_f': _jnp.float32}
MOMENT_SCALE = {'norm_a': 3.900039e-01, 'w_in_a': 1.918657e-01, 'conv_a': 2.010861e-01, 'w_out_a': 1.915397e-01, 'norm_kv': 1.227189e-01, 'w_kv': 1.239155e-01, 'norm_b': 1.254405e-01, 'w_in_b': 1.269781e-01, 'w_out_b': 1.124882e-01, 'norm_f': 1.280427e+02}


def _to_microbatches(a, axis):
    t = _jnp.moveaxis(a, axis, 0)
    t = t.reshape((N_MICROBATCH, t.shape[0] // N_MICROBATCH) + t.shape[1:])
    return _jnp.moveaxis(t, 1, axis + 1)


def setup_inputs(seed: int = 0) -> dict:
    inp = _fwd_setup_inputs(seed)
    key = _jax.random.fold_in(_jax.random.key(seed), 7919)
    shape, _ = _output_shape()
    out = dict(inp)
    out["loss_target"] = _jax.random.normal(_jax.random.fold_in(key, 0), shape, _jnp.float32)
    for i, name in enumerate(TWIN_WEIGHTS):
        w = inp[name].astype(_jnp.float32)
        if MOMENT_SCALE is None:
            s = _jnp.sqrt(_jnp.mean(_jnp.square(w)) + 1e-30)
        else:
            s = MOMENT_SCALE[name]
        km, kv = _jax.random.split(_jax.random.fold_in(key, i + 1))
        out[name] = w
        out["m_" + name] = s * _jax.random.normal(km, w.shape, _jnp.float32)
        out["v_" + name] = (s * s) * _jax.random.uniform(kv, w.shape, _jnp.float32, 0.5, 1.5)
    if N_MICROBATCH > 1:
        for name, axis in PER_EXAMPLE_BATCH_AXIS.items():
            out[name] = _to_microbatches(out[name], axis)
    return {'x': out['x'], 'norm_a': out['norm_a'], 'w_in_a': out['w_in_a'], 'conv_a': out['conv_a'], 'w_out_a': out['w_out_a'], 'norm_kv': out['norm_kv'], 'w_kv': out['w_kv'], 'norm_b': out['norm_b'], 'w_in_b': out['w_in_b'], 'w_out_b': out['w_out_b'], 'norm_f': out['norm_f'], 'loss_target': out['loss_target'], 'm_norm_a': out['m_norm_a'], 'm_w_in_a': out['m_w_in_a'], 'm_conv_a': out['m_conv_a'], 'm_w_out_a': out['m_w_out_a'], 'm_norm_kv': out['m_norm_kv'], 'm_w_kv': out['m_w_kv'], 'm_norm_b': out['m_norm_b'], 'm_w_in_b': out['m_w_in_b'], 'm_w_out_b': out['m_w_out_b'], 'm_norm_f': out['m_norm_f'], 'v_norm_a': out['v_norm_a'], 'v_w_in_a': out['v_w_in_a'], 'v_conv_a': out['v_conv_a'], 'v_w_out_a': out['v_w_out_a'], 'v_norm_kv': out['v_norm_kv'], 'v_w_kv': out['v_w_kv'], 'v_norm_b': out['v_norm_b'], 'v_w_in_b': out['v_w_in_b'], 'v_w_out_b': out['v_w_out_b'], 'v_norm_f': out['v_norm_f']}


def _loss(weights, diff, rest, loss_target):
    with _jax.named_scope("forward"):
        args = {**rest, TWIN_DIFF_INPUT: diff, **{k: w.astype(_WEIGHT_DTYPES[k]) for k, w in weights.items()}}
        y = _forward(args)
    with _jax.named_scope("loss_head"):
        err = _jnp.square(y.astype(_jnp.float32) - loss_target)
        return 0.5 * _jnp.sum(_jnp.mean(err, axis=-1)) if err.ndim else 0.5 * err


def _adamw(w, g, m, v):
    m = ADAM_B1 * m + (1.0 - ADAM_B1) * g
    v = ADAM_B2 * v + (1.0 - ADAM_B2) * _jnp.square(g)
    m_hat = m / (1.0 - ADAM_B1 ** ADAM_STEP)
    v_hat = v / (1.0 - ADAM_B2 ** ADAM_STEP)
    delta = -ADAM_LR * (m_hat / (_jnp.sqrt(v_hat) + ADAM_EPS) + ADAM_WD * w)
    return delta, m, v


def reference(x, norm_a, w_in_a, conv_a, w_out_a, norm_kv, w_kv, norm_b, w_in_b, w_out_b, norm_f, loss_target, m_norm_a, m_w_in_a, m_conv_a, m_w_out_a, m_norm_kv, m_w_kv, m_norm_b, m_w_in_b, m_w_out_b, m_norm_f, v_norm_a, v_w_in_a, v_conv_a, v_w_out_a, v_norm_kv, v_w_kv, v_norm_b, v_w_in_b, v_w_out_b, v_norm_f):
    given = dict(x=x, norm_a=norm_a, w_in_a=w_in_a, conv_a=conv_a, w_out_a=w_out_a, norm_kv=norm_kv, w_kv=w_kv, norm_b=norm_b, w_in_b=w_in_b, w_out_b=w_out_b, norm_f=norm_f, loss_target=loss_target, m_norm_a=m_norm_a, m_w_in_a=m_w_in_a, m_conv_a=m_conv_a, m_w_out_a=m_w_out_a, m_norm_kv=m_norm_kv, m_w_kv=m_w_kv, m_norm_b=m_norm_b, m_w_in_b=m_w_in_b, m_w_out_b=m_w_out_b, m_norm_f=m_norm_f, v_norm_a=v_norm_a, v_w_in_a=v_w_in_a, v_conv_a=v_conv_a, v_w_out_a=v_w_out_a, v_norm_kv=v_norm_kv, v_w_kv=v_w_kv, v_norm_b=v_norm_b, v_w_in_b=v_w_in_b, v_w_out_b=v_w_out_b, v_norm_f=v_norm_f)
    weights = {n: given[n] for n in TWIN_WEIGHTS}
    shared = {n: given[n] for n in SHARED_INPUTS}
    per_example = {n: given[n] for n in ['x']}
    grad_fn = _jax.value_and_grad(_loss, argnums=(0, 1))

    def one_microbatch(ex, loss_target):
        ex = dict(ex)
        diff = ex.pop(TWIN_DIFF_INPUT)
        return grad_fn(weights, diff, {**shared, **ex}, loss_target)

    if N_MICROBATCH == 1:
        loss, (grad_w, grad_x) = one_microbatch(per_example, given["loss_target"])
    else:
        def body(carry, xs):
            loss_sum, grad_sum = carry
            l_k, (gw_k, gx_k) = one_microbatch(xs[0], xs[1])
            with _jax.named_scope("update"):
                return (loss_sum + l_k, _jax.tree.map(_jnp.add, grad_sum, gw_k)), gx_k

        init = (_jnp.zeros((), _jnp.float32), _jax.tree.map(_jnp.zeros_like, weights))
        (loss, grad_w), grad_x = _jax.lax.scan(body, init, (per_example, given["loss_target"]))
    with _jax.named_scope("update"):
        delta_w, new_m, new_v = {}, {}, {}
        for n in TWIN_WEIGHTS:
            delta_w[n], new_m[n], new_v[n] = _adamw(weights[n], grad_w[n], given["m_" + n], given["v_" + n])
    return (loss, grad_x, *[grad_w[n] for n in TWIN_WEIGHTS], *[delta_w[n] for n in TWIN_WEIGHTS],
            *[new_m[n] for n in TWIN_WEIGHTS], *[new_v[n] for n in TWIN_WEIGHTS])
```

```python
import functools

import jax
import jax.numpy as jnp
from jax import lax
from jax.experimental import pallas as pl
from jax.experimental.pallas import tpu as pltpu

_MXU_DTYPE = jnp.bfloat16
_F32 = jnp.float32

RMS_EPS = 1e-6
HEAD_DIM = 64
LANES = 128
N_DEV = 8
ATT_TILE = 256
ROW_TILE = 256
VMEM_LIMIT = 56 * 1024 * 1024
SKIP_LOG = -110.0

ADAM_LR = 0.001
ADAM_B1 = 0.9
ADAM_B2 = 0.999
ADAM_EPS = 1e-08
ADAM_WD = 0.01
ADAM_STEP = 10

_NT = (((1,), (1,)), ((), ()))
_TN = (((0,), (0,)), ((), ()))


def _params(n_grid):
    return pltpu.CompilerParams(dimension_semantics=("arbitrary",) * n_grid, vmem_limit_bytes=VMEM_LIMIT)


def _dot(a, b):
    return jnp.dot(a, b, preferred_element_type=_F32)


def _dot_nt(a, b):
    return lax.dot_general(a, b, _NT, preferred_element_type=_F32)


def _dot_tn(a, b):
    return lax.dot_general(a, b, _TN, preferred_element_type=_F32)


def _mx(a):
    return a.astype(_MXU_DTYPE)


def _sigmoid(a):
    return 1.0 / (1.0 + jnp.exp(-a))


def _rms_stats(xv):
    r = lax.rsqrt(jnp.mean(xv * xv, axis=-1, keepdims=True) + RMS_EPS)
    return r, xv * r


def _rms_bwd(r, xh, dyg):
    return r * (dyg - xh * jnp.mean(dyg * xh, axis=-1, keepdims=True))


def _row_spec(tm, width):
    return pl.BlockSpec((tm, width), lambda i: (i, 0))


def _full_spec(shape):
    zeros = (0,) * len(shape)
    return pl.BlockSpec(shape, lambda *_: zeros)


def _exchange(name, arrays, scatter):
    n = len(arrays)
    pieces = [a.shape[1:] if scatter else a.shape for a in arrays]

    def body(*refs):
        ins, outs = refs[:n], refs[n:2 * n]
        send_sems, recv_sems, local_sems = refs[2 * n:]
        x, y, c = lax.axis_index("x"), lax.axis_index("y"), lax.axis_index("c")
        me = 4 * x + 2 * y + c

        def peer_of(m):
            px = 1 - x if m & 4 else x
            py = 1 - y if m & 2 else y
            pc = 1 - c if m & 1 else c
            return (px, py, pc), 4 * px + 2 * py + pc

        def remote(t, m):
            dev, idx = peer_of(m)
            k = t * (N_DEV - 1) + m - 1
            src = ins[t].at[idx] if scatter else ins[t]
            return pltpu.make_async_remote_copy(
                src_ref=src, dst_ref=outs[t].at[me], send_sem=send_sems.at[k], recv_sem=recv_sems.at[k],
                device_id=dev, device_id_type=pl.DeviceIdType.MESH)

        def arrival(t, m):
            dev, idx = peer_of(m)
            k = t * (N_DEV - 1) + m - 1
            src = ins[t].at[idx] if scatter else ins[t]
            return pltpu.make_async_remote_copy(
                src_ref=src, dst_ref=outs[t].at[idx], send_sem=send_sems.at[k], recv_sem=recv_sems.at[k],
                device_id=dev, device_id_type=pl.DeviceIdType.MESH)

        local = []
        for t in range(n):
            mine = ins[t].at[me] if scatter else ins[t]
            cp = pltpu.make_async_copy(mine, outs[t].at[me], local_sems.at[t])
            cp.start()
            local.append(cp)
        sent = []
        for m in range(1, N_DEV):
            for t in range(n):
                cp = remote(t, m)
                cp.start()
                sent.append(cp)
        for m in range(1, N_DEV):
            for t in range(n):
                arrival(t, m).wait_recv()
        for cp in sent:
            cp.wait_send()
        for cp in local:
            cp.wait()

    any_spec = pl.BlockSpec(memory_space=pl.ANY)
    return pl.pallas_call(
        body, name=name,
        in_specs=[any_spec] * n, out_specs=[any_spec] * n,
        out_shape=[jax.ShapeDtypeStruct((N_DEV,) + p, a.dtype) for p, a in zip(pieces, arrays)],
        scratch_shapes=[pltpu.SemaphoreType.DMA((n * (N_DEV - 1),)), pltpu.SemaphoreType.DMA((n * (N_DEV - 1),)),
                        pltpu.SemaphoreType.DMA((n,))],
    )(*arrays)


def _ln_matmul_in_a(x, gain, w_blk):
    s, d = x.shape
    nb, _, bn = w_blk.shape
    tm = min(ROW_TILE, s)

    def body(x_ref, g_ref, w_ref, u_ref):
        _, xh = _rms_stats(x_ref[...])
        h = _mx(xh * g_ref[...])
        for j in range(nb):
            u_ref[:, j * bn:(j + 1) * bn] = _dot(h, w_ref[j])

    return pl.pallas_call(
        body, name="ln_matmul_in_a", grid=(s // tm,),
        in_specs=[_row_spec(tm, d), _full_spec((1, d)), _full_spec((nb, d, bn))],
        out_specs=_row_spec(tm, nb * bn),
        out_shape=jax.ShapeDtypeStruct((s, nb * bn), _F32),
        compiler_params=_params(1),
    )(x, gain, w_blk)


def _shift_rows_down(p, before, tm):
    row = lax.broadcasted_iota(jnp.int32, p.shape, 0)
    p1 = jnp.where(row == 0, before[7:8, :], pltpu.roll(p, 1, 0))
    p2 = jnp.where(row == 0, before[6:7, :], jnp.where(row == 1, before[7:8, :], pltpu.roll(p, 2, 0)))
    return p1, p2


def _layer_a_out(u, x, conv, w_out_a, gain_kv, gain_b, w_kv, w_in_b):
    s, d = x.shape
    a = w_kv.shape[1] // 2
    tm = min(ROW_TILE, s)
    scale = HEAD_DIM ** -0.5

    def body(u_ref, x_ref, conv_ref, woa_ref, gkv_ref, gb_ref, wkv_ref, wib_ref,
             x1_ref, kv_ref, q_ref, gate_ref, halo_ref):
        @pl.when(pl.program_id(0) == 0)
        def _():
            halo_ref[...] = jnp.zeros_like(halo_ref)

        b, c, xin, g = (u_ref[:, k * d:(k + 1) * d] for k in range(4))
        p = c * xin
        p1, p2 = _shift_rows_down(p, halo_ref[...], tm)
        halo_ref[...] = p[tm - 8:tm, :]
        w = conv_ref[...]
        cv = w[0:1, :] * p2 + w[1:2, :] * p1 + w[2:3, :] * p
        ya = (b * cv) * (g * _sigmoid(g))
        x1 = x_ref[...] + _dot(_mx(ya), woa_ref[...])
        x1_ref[...] = x1
        _, xh = _rms_stats(x1)
        kv_ref[...] = _dot(_mx(xh * gkv_ref[...]), wkv_ref[...]).astype(kv_ref.dtype)
        qg = _dot(_mx(xh * gb_ref[...]), wib_ref[...])
        q_ref[...] = (qg[:, :a] * scale).astype(q_ref.dtype)
        gate_ref[...] = qg[:, a:]

    return pl.pallas_call(
        body, name="layer_a_out", grid=(s // tm,),
        in_specs=[_row_spec(tm, 4 * d), _row_spec(tm, d), _full_spec((8, d)), _full_spec((d, d)),
                  _full_spec((1, d)), _full_spec((1, d)), _full_spec((d, 2 * a)), _full_spec((d, 2 * a))],
        out_specs=[_row_spec(tm, d), _row_spec(tm, 2 * a), _row_spec(tm, a), _row_spec(tm, a)],
        out_shape=[jax.ShapeDtypeStruct((s, d), _F32), jax.ShapeDtypeStruct((s, 2 * a), _MXU_DTYPE),
                   jax.ShapeDtypeStruct((s, a), _MXU_DTYPE), jax.ShapeDtypeStruct((s, a), _F32)],
        scratch_shapes=[pltpu.VMEM((8, d), _F32)],
        compiler_params=_params(1),
    )(u, x, conv, w_out_a, gain_kv, gain_b, w_kv, w_in_b)


def _neg_softplus(z):
    e = jnp.exp(-jnp.abs(z))
    return -(jnp.maximum(z, 0.0) + jnp.log(1.0 + e)), e


def _split_dot(val, tri):
    hi = _mx(val)
    lo = _mx(val - hi.astype(_F32))
    return _dot(hi, tri) + _dot(lo, tri)


def _head_masks():
    lane = lax.broadcasted_iota(jnp.int32, (1, LANES), 1)
    return [lane < HEAD_DIM, lane >= HEAD_DIM]


def _attn_fwd(q, kv, tri_suffix):
    s, a = q.shape
    n_hp = a // LANES
    tq = min(ATT_TILE, s)
    nq = s // tq

    def body(q_ref, k_ref, v_ref, tri_ref, o_ref, c_ref, stop_ref):
        hp, i = pl.program_id(0), pl.program_id(1)
        qv = q_ref[...]
        tri = tri_ref[...]
        row = lax.broadcasted_iota(jnp.int32, (tq, tq), 0)
        col = lax.broadcasted_iota(jnp.int32, (tq, tq), 1)
        causal = col < row

        def block(j, qh, c, acc, diag):
            off = pl.multiple_of(j * tq, tq)
            kb = k_ref[pl.ds(off, tq), :]
            vb = v_ref[pl.ds(off, tq), :]
            z = _dot_nt(qh, kb)
            lg, _ = _neg_softplus(z)
            if diag:
                lg = jnp.where(causal, lg, 0.0)
            w = jnp.exp(z + _split_dot(lg, tri) + c)
            if diag:
                w = jnp.where(causal, w, 0.0)
            return c + jnp.sum(lg, axis=1, keepdims=True), acc + _dot(_mx(w), vb)

        res = []
        for h2, hmask in enumerate(_head_masks()):
            qh = jnp.where(hmask, qv, jnp.zeros_like(qv))
            c, acc = block(i, qh, jnp.zeros((tq, 1), _F32), jnp.zeros((tq, LANES), _F32), True)

            def cond(st):
                return jnp.logical_and(st[0] >= 0, jnp.max(st[1]) > SKIP_LOG)

            def step(st, qh=qh):
                c, acc = block(st[0], qh, st[1], st[2], False)
                return st[0] - 1, c, acc

            j, c, acc = lax.while_loop(cond, step, (i - 1, c, acc))
            stop_ref[hp * 2 + h2, i] = j + 1
            res.append((c, acc))
        first = _head_masks()[0]
        o_ref[...] = jnp.where(first, res[0][1], res[1][1])
        c_ref[...] = jnp.where(first, res[0][0], res[1][0])

    q_spec = pl.BlockSpec((tq, LANES), lambda h, i: (i, h))
    return pl.pallas_call(
        body, name="attn_fwd", grid=(n_hp, nq),
        in_specs=[q_spec, pl.BlockSpec((s, LANES), lambda h, i: (0, h)),
                  pl.BlockSpec((s, LANES), lambda h, i: (0, n_hp + h)), _full_spec((tq, tq))],
        out_specs=[q_spec, q_spec, pl.BlockSpec(memory_space=pltpu.SMEM)],
        out_shape=[jax.ShapeDtypeStruct((s, a), _F32), jax.ShapeDtypeStruct((s, a), _F32),
                   jax.ShapeDtypeStruct((2 * n_hp, nq), jnp.int32)],
        compiler_params=_params(2),
    )(q, kv, kv, tri_suffix)


def _layer_b_out_loss(o, gate, x1, w_out_b, gain_f, target):
    s, d = x1.shape
    a = o.shape[1]
    tm = min(ROW_TILE, s)

    def body(o_ref, gate_ref, x1_ref, wob_ref, gf_ref, tgt_ref, dx2_ref, gnf_ref, loss_ref):
        @pl.when(pl.program_id(0) == 0)
        def _():
            gnf_ref[...] = jnp.zeros_like(gnf_ref)
            loss_ref[...] = jnp.zeros_like(loss_ref)

        gt = gate_ref[...]
        ob = o_ref[...] * (gt * _sigmoid(gt))
        x2 = x1_ref[...] + _dot(_mx(ob), wob_ref[...])
        r, xh = _rms_stats(x2)
        gf = gf_ref[...]
        err = xh * gf - tgt_ref[...]
        part = jnp.sum(jnp.sum(err * err, axis=1, keepdims=True), axis=0, keepdims=True)
        loss_ref[...] += part * (0.5 / d)
        dy = err * (1.0 / d)
        gnf_ref[...] += jnp.sum(dy * xh, axis=0, keepdims=True)
        dx2_ref[...] = _rms_bwd(r, xh, dy * gf)

    return pl.pallas_call(
        body, name="layer_b_out_loss", grid=(s // tm,),
        in_specs=[_row_spec(tm, a), _row_spec(tm, a), _row_spec(tm, d), _full_spec((a, d)), _full_spec((1, d)),
                  _row_spec(tm, d)],
        out_specs=[_row_spec(tm, d), _full_spec((1, d)), _full_spec((1, LANES))],
        out_shape=[jax.ShapeDtypeStruct((s, d), _F32), jax.ShapeDtypeStruct((1, d), _F32),
                   jax.ShapeDtypeStruct((1, LANES), _F32)],
        compiler_params=_params(1),
    )(o, gate, x1, w_out_b, gain_f, target)


def _silu_and_grad(g):
    sg = _sigmoid(g)
    return g * sg, sg * (1.0 + g * (1.0 - sg))


def _layer_b_bwd(dx2, o, gate, w_out_b):
    s, d = dx2.shape
    a = o.shape[1]
    tm = min(ROW_TILE, s)

    def body(dx2_ref, o_ref, gate_ref, wob_ref, do_ref, dgate_ref, gw_ref):
        @pl.when(pl.program_id(0) == 0)
        def _():
            gw_ref[...] = jnp.zeros_like(gw_ref)

        dxb = _mx(dx2_ref[...])
        sl, dsl = _silu_and_grad(gate_ref[...])
        ov = o_ref[...]
        d_ob = _dot_nt(dxb, wob_ref[...])
        gw_ref[...] += _dot_tn(_mx(ov * sl), dxb)
        do_ref[...] = (d_ob * sl).astype(do_ref.dtype)
        dgate_ref[...] = (d_ob * ov * dsl).astype(dgate_ref.dtype)

    return pl.pallas_call(
        body, name="layer_b_bwd", grid=(s // tm,),
        in_specs=[_row_spec(tm, d), _row_spec(tm, a), _row_spec(tm, a), _full_spec((a, d))],
        out_specs=[_row_spec(tm, a), _row_spec(tm, a), _full_spec((a, d))],
        out_shape=[jax.ShapeDtypeStruct((s, a), _MXU_DTYPE), jax.ShapeDtypeStruct((s, a), _MXU_DTYPE),
                   jax.ShapeDtypeStruct((a, d), _F32)],
        compiler_params=_params(1),
    )(dx2, o, gate, w_out_b)


def _attn_bwd(q, kv, do, c_tot, stop, tri_suffix, tri_prefix):
    s, a = q.shape
    n_hp = a // LANES
    tq = min(ATT_TILE, s)
    nq = s // tq
    scale = HEAD_DIM ** -0.5
    chunk = min(1024, s)

    def body(stop_ref, q_ref, do_ref, c_ref, k_ref, v_ref, ts_ref, tp_ref, dq_ref, dk_hbm, dv_hbm,
             dk_acc, dv_acc, stage, sem):
        hp, i = pl.program_id(0), pl.program_id(1)

        @pl.when(i == 0)
        def _():
            dk_acc[...] = jnp.zeros_like(dk_acc)
            dv_acc[...] = jnp.zeros_like(dv_acc)

        qv, dov, cv = q_ref[...], do_ref[...], c_ref[...]
        ts, tp = ts_ref[...], tp_ref[...]
        row = lax.broadcasted_iota(jnp.int32, (tq, tq), 0)
        col = lax.broadcasted_iota(jnp.int32, (tq, tq), 1)
        causal = col < row

        def block(j, qh, doh, ch, st, diag):
            asc, pre, dq = st
            off = pl.multiple_of(j * tq, tq)
            kb = k_ref[pl.ds(off, tq), :]
            vb = v_ref[pl.ds(off, tq), :]
            z = _dot_nt(qh, kb)
            lg, e = _neg_softplus(z)
            inv = 1.0 / (1.0 + e)
            sig = jnp.where(z >= 0.0, inv, e * inv)
            if diag:
                lg = jnp.where(causal, lg, 0.0)
            tot = jnp.sum(lg, axis=1, keepdims=True)
            newer = jnp.zeros_like(tot) if diag else ch - asc - tot
            wgt = jnp.exp(z + _split_dot(lg, ts) + newer)
            if diag:
                wgt = jnp.where(causal, wgt, 0.0)
            g = wgt * _dot_nt(doh, vb)
            dz = g - sig * (_split_dot(g, tp) + pre)
            if diag:
                dz = jnp.where(causal, dz, 0.0)
            dzb = _mx(dz)
            dk_acc[pl.ds(off, tq), :] += _dot_tn(dzb, qh)
            dv_acc[pl.ds(off, tq), :] += _dot_tn(_mx(wgt), doh)
            return asc + tot, pre + jnp.sum(g, axis=1, keepdims=True), dq + _dot(dzb, kb)

        dqs = []
        for h2, hmask in enumerate(_head_masks()):
            qh = jnp.where(hmask, qv, jnp.zeros_like(qv))
            doh = jnp.where(hmask, dov, jnp.zeros_like(dov))
            ch = cv[:, h2 * HEAD_DIM:h2 * HEAD_DIM + 1]
            first = jnp.clip(stop_ref[hp * 2 + h2, i], 0, i)
            st = (jnp.zeros((tq, 1), _F32), jnp.zeros((tq, 1), _F32), jnp.zeros((tq, LANES), _F32))
            st = lax.fori_loop(first, i, lambda j, st, qh=qh, doh=doh, ch=ch: block(j, qh, doh, ch, st, False), st)
            dqs.append(block(i, qh, doh, ch, st, True)[2])
        dq_ref[...] = (jnp.where(_head_masks()[0], dqs[0], dqs[1]) * scale).astype(dq_ref.dtype)

        @pl.when(i == nq - 1)
        def _():
            cols = pl.ds(pl.multiple_of(hp * LANES, LANES), LANES)
            for acc, out in ((dk_acc, dk_hbm), (dv_acc, dv_hbm)):
                def cast(n, carry, acc=acc):
                    rows = pl.ds(pl.multiple_of(n * chunk, chunk), chunk)
                    stage[rows, :] = acc[rows, :].astype(stage.dtype)
                    return carry
                lax.fori_loop(0, s // chunk, cast, 0)
                cp = pltpu.make_async_copy(stage, out.at[:, cols], sem)
                cp.start()
                cp.wait()

    q_spec = pl.BlockSpec((tq, LANES), lambda h, i, *_: (i, h))
    grid_spec = pltpu.PrefetchScalarGridSpec(
        num_scalar_prefetch=1, grid=(n_hp, nq),
        in_specs=[q_spec, q_spec, q_spec, pl.BlockSpec((s, LANES), lambda h, i, *_: (0, h)),
                  pl.BlockSpec((s, LANES), lambda h, i, *_: (0, n_hp + h)),
                  pl.BlockSpec((tq, tq), lambda h, i, *_: (0, 0)), pl.BlockSpec((tq, tq), lambda h, i, *_: (0, 0))],
        out_specs=[q_spec, pl.BlockSpec(memory_space=pl.ANY), pl.BlockSpec(memory_space=pl.ANY)],
        scratch_shapes=[pltpu.VMEM((s, LANES), _F32), pltpu.VMEM((s, LANES), _F32), pltpu.VMEM((s, LANES), _MXU_DTYPE),
                        pltpu.SemaphoreType.DMA],
    )
    return pl.pallas_call(
        body, name="attn_bwd", grid_spec=grid_spec,
        out_shape=[jax.ShapeDtypeStruct((s, a), _MXU_DTYPE)] * 3,
        compiler_params=_params(2),
    )(stop, q, do, c_tot, kv, kv, tri_suffix, tri_prefix)


def _proj_bwd(x1, dx2, dq, dgate, dk, dv, w_in_b, w_kv, gain_b, gain_kv):
    s, d = x1.shape
    a = dq.shape[1]
    tm = min(ROW_TILE, s)

    def body(x1_ref, dx2_ref, dq_ref, dgate_ref, dk_ref, dv_ref, wib_ref, wkv_ref, gb_ref, gkv_ref,
             dx1_ref, gwib_ref, gwkv_ref, gnb_ref, gnkv_ref):
        @pl.when(pl.program_id(0) == 0)
        def _():
            for ref in (gwib_ref, gwkv_ref, gnb_ref, gnkv_ref):
                ref[...] = jnp.zeros_like(ref)

        r, xh = _rms_stats(x1_ref[...])
        gb, gkv = gb_ref[...], gkv_ref[...]
        hb, hk = _mx(xh * gb), _mx(xh * gkv)
        dq, dgate, dk, dv = dq_ref[...], dgate_ref[...], dk_ref[...], dv_ref[...]
        gwib_ref[:, :a] += _dot_tn(hb, dq)
        gwib_ref[:, a:] += _dot_tn(hb, dgate)
        gwkv_ref[:, :a] += _dot_tn(hk, dk)
        gwkv_ref[:, a:] += _dot_tn(hk, dv)
        d_hb = _dot_nt(dq, wib_ref[:, :a]) + _dot_nt(dgate, wib_ref[:, a:])
        d_hk = _dot_nt(dk, wkv_ref[:, :a]) + _dot_nt(dv, wkv_ref[:, a:])
        gnb_ref[...] += jnp.sum(d_hb * xh, axis=0, keepdims=True)
        gnkv_ref[...] += jnp.sum(d_hk * xh, axis=0, keepdims=True)
        dx1_ref[...] = dx2_ref[...] + _rms_bwd(r, xh, d_hb * gb + d_hk * gkv)

    return pl.pallas_call(
        body, name="proj_bwd", grid=(s // tm,),
        in_specs=[_row_spec(tm, d), _row_spec(tm, d)] + [_row_spec(tm, a)] * 4
        + [_full_spec((d, 2 * a)), _full_spec((d, 2 * a)), _full_spec((1, d)), _full_spec((1, d))],
        out_specs=[_row_spec(tm, d), _full_spec((d, 2 * a)), _full_spec((d, 2 * a)), _full_spec((1, d)),
                   _full_spec((1, d))],
        out_shape=[jax.ShapeDtypeStruct((s, d), _F32), jax.ShapeDtypeStruct((d, 2 * a), _F32),
                   jax.ShapeDtypeStruct((d, 2 * a), _F32), jax.ShapeDtypeStruct((1, d), _F32),
                   jax.ShapeDtypeStruct((1, d), _F32)],
        compiler_params=_params(1),
    )(x1, dx2, dq, dgate, dk, dv, w_in_b, w_kv, gain_b, gain_kv)


def _layer_a_bwd(u, dx1, conv, w_out_a):
    s, d = dx1.shape
    tm = min(ROW_TILE, s)
    n = s // tm
    per8 = tm // 8

    def body(u_ref, uprev_ref, dx1_ref, conv_ref, woa_ref, du_ref, gwoa_ref, gconv_ref, halo_ref):
        step = pl.program_id(0)

        @pl.when(step == 0)
        def _():
            for ref in (gwoa_ref, gconv_ref, halo_ref):
                ref[...] = jnp.zeros_like(ref)

        b, c, xin, g = (u_ref[:, k * d:(k + 1) * d] for k in range(4))
        p = c * xin
        before = uprev_ref[:, d:2 * d] * uprev_ref[:, 2 * d:3 * d]
        before = jnp.where(step == n - 1, jnp.zeros_like(before), before)
        p1, p2 = _shift_rows_down(p, before, tm)
        w = conv_ref[...]
        cv = w[0:1, :] * p2 + w[1:2, :] * p1 + w[2:3, :] * p
        sl, dsl = _silu_and_grad(g)
        y = b * cv
        dxb = _mx(dx1_ref[...])
        gwoa_ref[...] += _dot_tn(_mx(y * sl), dxb)
        d_ya = _dot_nt(dxb, woa_ref[...])
        d_y = d_ya * sl
        d_cv = d_y * b
        after = halo_ref[...]
        row = lax.broadcasted_iota(jnp.int32, p.shape, 0)
        n1 = jnp.where(row == tm - 1, after[0:1, :], pltpu.roll(d_cv, tm - 1, 0))
        n2 = jnp.where(row == tm - 2, after[0:1, :],
                       jnp.where(row == tm - 1, after[1:2, :], pltpu.roll(d_cv, tm - 2, 0)))
        halo_ref[...] = d_cv[0:8, :]
        d_p = w[2:3, :] * d_cv + w[1:2, :] * n1 + w[0:1, :] * n2
        gconv_ref[0:1, :] += jnp.sum(d_cv * p2, axis=0, keepdims=True)
        gconv_ref[1:2, :] += jnp.sum(d_cv * p1, axis=0, keepdims=True)
        gconv_ref[2:3, :] += jnp.sum(d_cv * p, axis=0, keepdims=True)
        du_ref[:, 0:d] = (d_y * cv).astype(du_ref.dtype)
        du_ref[:, d:2 * d] = (d_p * xin).astype(du_ref.dtype)
        du_ref[:, 2 * d:3 * d] = (d_p * c).astype(du_ref.dtype)
        du_ref[:, 3 * d:4 * d] = (d_ya * y * dsl).astype(du_ref.dtype)

    def rev(i):
        return (n - 1 - i, 0)

    return pl.pallas_call(
        body, name="layer_a_bwd", grid=(n,),
        in_specs=[pl.BlockSpec((tm, 4 * d), rev),
                  pl.BlockSpec((8, 4 * d), lambda i: (jnp.maximum((n - 1 - i) * per8 - 1, 0), 0)),
                  pl.BlockSpec((tm, d), rev), _full_spec((8, d)), _full_spec((d, d))],
        out_specs=[pl.BlockSpec((tm, 4 * d), rev), _full_spec((d, d)), _full_spec((8, d))],
        out_shape=[jax.ShapeDtypeStruct((s, 4 * d), _MXU_DTYPE), jax.ShapeDtypeStruct((d, d), _F32),
                   jax.ShapeDtypeStruct((8, d), _F32)],
        scratch_shapes=[pltpu.VMEM((8, d), _F32)],
        compiler_params=_params(1),
    )(u, u, dx1, conv, w_out_a)


def _grad_w_in_a(x, gain, du, nb):
    s, d = x.shape
    bn = du.shape[1] // nb
    tm = min(ROW_TILE, s)

    def body(x_ref, g_ref, du_ref, gw_ref):
        @pl.when(pl.program_id(1) == 0)
        def _():
            gw_ref[...] = jnp.zeros_like(gw_ref)

        _, xh = _rms_stats(x_ref[...])
        gw_ref[0] += _dot_tn(_mx(xh * g_ref[...]), du_ref[...])

    return pl.pallas_call(
        body, name="grad_w_in_a", grid=(nb, s // tm),
        in_specs=[pl.BlockSpec((tm, d), lambda j, i: (i, 0)), pl.BlockSpec((1, d), lambda j, i: (0, 0)),
                  pl.BlockSpec((tm, bn), lambda j, i: (i, j))],
        out_specs=pl.BlockSpec((1, d, bn), lambda j, i: (j, 0, 0)),
        out_shape=jax.ShapeDtypeStruct((nb, d, bn), _F32),
        compiler_params=_params(2),
    )(x, gain, du)


def _input_grad(x, dx1, du, w_blk, gain):
    s, d = x.shape
    nb, _, bn = w_blk.shape
    tm = min(ROW_TILE, s)

    def body(x_ref, dx1_ref, du_ref, w_ref, g_ref, dx_ref, gn_ref):
        @pl.when(pl.program_id(0) == 0)
        def _():
            gn_ref[...] = jnp.zeros_like(gn_ref)

        r, xh = _rms_stats(x_ref[...])
        d_h = _dot_nt(du_ref[:, 0:bn], w_ref[0])
        for j in range(1, nb):
            d_h = d_h + _dot_nt(du_ref[:, j * bn:(j + 1) * bn], w_ref[j])
        gn_ref[...] += jnp.sum(d_h * xh, axis=0, keepdims=True)
        dx_ref[...] = dx1_ref[...] + _rms_bwd(r, xh, d_h * g_ref[...])

    return pl.pallas_call(
        body, name="input_grad", grid=(s // tm,),
        in_specs=[_row_spec(tm, d), _row_spec(tm, d), _row_spec(tm, nb * bn), _full_spec((nb, d, bn)),
                  _full_spec((1, d))],
        out_specs=[_row_spec(tm, d), _full_spec((1, d))],
        out_shape=[jax.ShapeDtypeStruct((s, d), _F32), jax.ShapeDtypeStruct((1, d), _F32)],
        compiler_params=_params(1),
    )(x, dx1, du, w_blk, gain)


def _reduce_adamw(name, parts, w, m, v):
    rows, cols = w.shape
    tr = min(256, rows)
    c1 = 1.0 - ADAM_B1 ** ADAM_STEP
    c2 = 1.0 - ADAM_B2 ** ADAM_STEP

    def body(p_ref, w_ref, m_ref, v_ref, g_ref, d_ref, nm_ref, nv_ref):
        g = p_ref[0]
        for k in range(1, N_DEV):
            g = g + p_ref[k]
        nm = ADAM_B1 * m_ref[...] + (1.0 - ADAM_B1) * g
        nv = ADAM_B2 * v_ref[...] + (1.0 - ADAM_B2) * (g * g)
        g_ref[...] = g
        nm_ref[...] = nm
        nv_ref[...] = nv
        d_ref[...] = -ADAM_LR * ((nm / c1) / (jnp.sqrt(nv / c2) + ADAM_EPS) + ADAM_WD * w_ref[...])

    tile = _row_spec(tr, cols)
    return pl.pallas_call(
        body, name=name, grid=(rows // tr,),
        in_specs=[pl.BlockSpec((N_DEV, tr, cols), lambda i: (0, i, 0)), tile, tile, tile],
        out_specs=[tile] * 4,
        out_shape=[jax.ShapeDtypeStruct((rows, cols), _F32)] * 4,
        compiler_params=_params(1),
    )(parts, w, m, v)


def _pad_rows(a, rows=8):
    return jnp.pad(a, ((0, rows - a.shape[0]), (0, 0)))


def kernel(x, norm_a, w_in_a, conv_a, w_out_a, norm_kv, w_kv, norm_b, w_in_b, w_out_b, norm_f, loss_target, m_norm_a, m_w_in_a, m_conv_a, m_w_out_a, m_norm_kv, m_w_kv, m_norm_b, m_w_in_b, m_w_out_b, m_norm_f, v_norm_a, v_w_in_a, v_conv_a, v_w_out_a, v_norm_kv, v_w_kv, v_norm_b, v_w_in_b, v_w_out_b, v_norm_f):
    x0 = x[0]
    s, d = x0.shape
    a = d // 2
    sh = d // N_DEV
    me = 4 * lax.axis_index("x") + 2 * lax.axis_index("y") + lax.axis_index("c")

    small_a = _pad_rows(jnp.concatenate([norm_a, conv_a[0]], axis=0))
    wia_g, woa_g, wkv_g, wib_g, wob_g, small_g = _exchange(
        "exchange_gather",
        [_mx(w_in_a[0]), _mx(w_out_a[0]), _mx(w_kv), _mx(w_in_b[0]), _mx(w_out_b[0]), small_a], scatter=False)
    woa_f = woa_g.reshape(d, d)
    wkv_f = wkv_g.reshape(d, 2 * a)
    wib_f = wib_g.reshape(d, 2 * a)
    wob_f = wob_g.transpose(1, 0, 2).reshape(a, d)
    small_f = small_g.transpose(1, 0, 2).reshape(8, d)
    gain_a = small_f[0:1]
    conv_f = _pad_rows(small_f[1:4])
    gain_kv, gain_b, gain_f = norm_kv.reshape(1, d), norm_b.reshape(1, d), norm_f.reshape(1, d)

    u = _ln_matmul_in_a(x0, gain_a, wia_g)
    x1, kv, q, gate = _layer_a_out(u, x0, conv_f, woa_f, gain_kv, gain_b, wkv_f, wib_f)
    tq = min(ATT_TILE, s)
    idx = jnp.arange(tq)
    tri_suffix = _mx(idx[:, None] >= idx[None, :])
    tri_prefix = _mx(idx[:, None] <= idx[None, :])
    o, c_tot, stop = _attn_fwd(q, kv, tri_suffix)
    dx2, g_norm_f, loss_part = _layer_b_out_loss(o, gate, x1, wob_f, gain_f, loss_target[0])

    do, dgate, g_wob = _layer_b_bwd(dx2, o, gate, wob_f)
    dq, dk, dv = _attn_bwd(q, kv, do, c_tot, stop, tri_suffix, tri_prefix)
    dx1, g_wib, g_wkv, g_norm_b, g_norm_kv = _proj_bwd(x1, dx2, dq, dgate, dk, dv, wib_f, wkv_f, gain_b, gain_kv)
    du, g_woa, g_conv = _layer_a_bwd(u, dx1, conv_f, woa_f)
    g_wia = _grad_w_in_a(x0, gain_a, du, N_DEV)
    dx0, g_norm_a = _input_grad(x0, dx1, du, wia_g, gain_a)

    small_grads = jnp.concatenate(
        [g_norm_a, g_conv[0:3], g_norm_kv, g_norm_b, g_norm_f, jnp.pad(loss_part, ((0, 0), (0, d - LANES)))], axis=0)
    p_wia, p_woa, p_wkv, p_wib, p_wob = _exchange(
        "exchange_scatter",
        [g_wia, g_woa.reshape(N_DEV, sh, d), g_wkv.reshape(N_DEV, sh, 2 * a), g_wib.reshape(N_DEV, sh, 2 * a),
         g_wob.reshape(a, N_DEV, sh).transpose(1, 0, 2)], scatter=True)
    (p_small,) = _exchange("exchange_small", [small_grads], scatter=False)

    upd_wia = _reduce_adamw("adamw_w_in_a", p_wia, w_in_a[0], m_w_in_a[0], v_w_in_a[0])
    upd_woa = _reduce_adamw("adamw_w_out_a", p_woa, w_out_a[0], m_w_out_a[0], v_w_out_a[0])
    upd_wkv = _reduce_adamw("adamw_w_kv", p_wkv, w_kv, m_w_kv, v_w_kv)
    upd_wib = _reduce_adamw("adamw_w_in_b", p_wib, w_in_b[0], m_w_in_b[0], v_w_in_b[0])
    upd_wob = _reduce_adamw("adamw_w_out_b", p_wob, w_out_b[0], m_w_out_b[0], v_w_out_b[0])

    def rep(a1, a2, a3):
        return jnp.concatenate([jnp.zeros((4, d), _F32), a1.reshape(1, d), a2.reshape(1, d), a3.reshape(1, d),
                                jnp.zeros((1, d), _F32)], axis=0)

    upd_rep = _reduce_adamw("adamw_replicated", p_small, rep(norm_kv, norm_b, norm_f),
                            rep(m_norm_kv, m_norm_b, m_norm_f), rep(v_norm_kv, v_norm_b, v_norm_f))

    def mine(n1, cv1):
        return _pad_rows(jnp.concatenate([n1, cv1[0]], axis=0))

    p_mine = lax.dynamic_slice(p_small, (0, 0, me * sh), (N_DEV, 8, sh))
    upd_mine = _reduce_adamw("adamw_sharded_small", p_mine, mine(norm_a, conv_a), mine(m_norm_a, m_conv_a),
                             mine(v_norm_a, v_conv_a))

    loss = upd_rep[0][7, 0]
    groups = []
    for k in range(4):
        groups.append([
            upd_mine[k][0:1], upd_wia[k][None], upd_mine[k][1:4][None], upd_woa[k][None], upd_rep[k][4],
            upd_wkv[k], upd_rep[k][5:6], upd_wib[k][None], upd_wob[k][None], upd_rep[k][6]])
    return (loss, dx0[None], *groups[0], *groups[1], *groups[2], *groups[3])
```

```python
import functools

import jax
import jax.numpy as jnp
from jax import lax
from jax.experimental import pallas as pl
from jax.experimental.pallas import tpu as pltpu

_MXU_DTYPE = jnp.bfloat16
_F32 = jnp.float32

RMS_EPS = 1e-6
HEAD_DIM = 64
LANES = 128
N_DEV = 8
ATT_TILE = 256
ROW_TILE = 256
VMEM_LIMIT = 56 * 1024 * 1024
SKIP_LOG = -110.0

ADAM_LR = 0.001
ADAM_B1 = 0.9
ADAM_B2 = 0.999
ADAM_EPS = 1e-08
ADAM_WD = 0.01
ADAM_STEP = 10

_NT = (((1,), (1,)), ((), ()))
_TN = (((0,), (0,)), ((), ()))


def _params(n_grid):
    return pltpu.CompilerParams(dimension_semantics=("arbitrary",) * n_grid, vmem_limit_bytes=VMEM_LIMIT)


def _dot(a, b):
    return jnp.dot(a, b, preferred_element_type=_F32)


def _dot_nt(a, b):
    return lax.dot_general(a, b, _NT, preferred_element_type=_F32)


def _dot_tn(a, b):
    return lax.dot_general(a, b, _TN, preferred_element_type=_F32)


def _mx(a):
    return a.astype(_MXU_DTYPE)


def _sigmoid(a):
    return 1.0 / (1.0 + jnp.exp(-a))


def _rms_stats(xv):
    r = lax.rsqrt(jnp.mean(xv * xv, axis=-1, keepdims=True) + RMS_EPS)
    return r, xv * r


def _rms_bwd(r, xh, dyg):
    return r * (dyg - xh * jnp.mean(dyg * xh, axis=-1, keepdims=True))


def _row_spec(tm, width):
    return pl.BlockSpec((tm, width), lambda i: (i, 0))


def _full_spec(shape):
    zeros = (0,) * len(shape)
    return pl.BlockSpec(shape, lambda *_: zeros)


def _exchange_ops(ins, outs, send_sems, recv_sems, local_sems, scatter):
    n = len(ins)
    x, y, c = lax.axis_index("x"), lax.axis_index("y"), lax.axis_index("c")
    me = 4 * x + 2 * y + c

    def remote(t, m, landed):
        px = 1 - x if m & 4 else x
        py = 1 - y if m & 2 else y
        pc = 1 - c if m & 1 else c
        idx = 4 * px + 2 * py + pc
        k = t * (N_DEV - 1) + m - 1
        return pltpu.make_async_remote_copy(
            src_ref=ins[t].at[idx] if scatter else ins[t], dst_ref=outs[t].at[idx if landed else me],
            send_sem=send_sems.at[k], recv_sem=recv_sems.at[k],
            device_id=(px, py, pc), device_id_type=pl.DeviceIdType.MESH)

    def local(t):
        return pltpu.make_async_copy(ins[t].at[me] if scatter else ins[t], outs[t].at[me], local_sems.at[t])

    def start():
        for t in range(n):
            local(t).start()
        for m in range(1, N_DEV):
            for t in range(n):
                remote(t, m, False).start()

    def wait():
        for m in range(1, N_DEV):
            for t in range(n):
                remote(t, m, True).wait_recv()
        for m in range(1, N_DEV):
            for t in range(n):
                remote(t, m, False).wait_send()
        for t in range(n):
            local(t).wait()

    return start, wait


def _exchange_shapes(arrays, scatter):
    return [jax.ShapeDtypeStruct((N_DEV,) + (a.shape[1:] if scatter else a.shape), a.dtype) for a in arrays]


def _exchange_sems(n):
    return [pltpu.SemaphoreType.DMA((n * (N_DEV - 1),)), pltpu.SemaphoreType.DMA((n * (N_DEV - 1),)),
            pltpu.SemaphoreType.DMA((n,))]


_ANY = pl.BlockSpec(memory_space=pl.ANY)


def _exchange(name, arrays, scatter):
    n = len(arrays)

    def body(*refs):
        start, wait = _exchange_ops(refs[:n], refs[n:2 * n], *refs[2 * n:], scatter)
        start()
        wait()

    return pl.pallas_call(
        body, name=name, in_specs=[_ANY] * n, out_specs=[_ANY] * n,
        out_shape=_exchange_shapes(arrays, scatter), scratch_shapes=_exchange_sems(n),
    )(*arrays)


def _ln_matmul_in_a(x, gain, w_blk, shards):
    s, d = x.shape
    nb, _, bn = w_blk.shape
    tm = min(ROW_TILE, s)
    n = len(shards)
    last = s // tm - 1

    def body(x_ref, g_ref, w_ref, *refs):
        u_ref, h_ref = refs[n:n + 2]
        start, wait = _exchange_ops(refs[:n], refs[n + 2:2 * n + 2], *refs[2 * n + 2:], False)
        pl.when(pl.program_id(0) == 0)(start)
        _, xh = _rms_stats(x_ref[...])
        h = _mx(xh * g_ref[...])
        h_ref[...] = h
        for j in range(nb):
            u_ref[:, j * bn:(j + 1) * bn] = _dot(h, w_ref[j])
        pl.when(pl.program_id(0) == last)(wait)

    outs = pl.pallas_call(
        body, name="ln_matmul_in_a", grid=(s // tm,),
        in_specs=[_row_spec(tm, d), _full_spec((1, d)), _full_spec((nb, d, bn))] + [_ANY] * n,
        out_specs=[_row_spec(tm, nb * bn), _row_spec(tm, d)] + [_ANY] * n,
        out_shape=[jax.ShapeDtypeStruct((s, nb * bn), _F32), jax.ShapeDtypeStruct((s, d), _MXU_DTYPE)]
        + _exchange_shapes(shards, False),
        scratch_shapes=_exchange_sems(n),
        compiler_params=_params(1),
    )(x, gain, w_blk, *shards)
    return outs[0], outs[1], outs[2:]


def _shift_rows_down(p, before, tm):
    row = lax.broadcasted_iota(jnp.int32, p.shape, 0)
    p1 = jnp.where(row == 0, before[7:8, :], pltpu.roll(p, 1, 0))
    p2 = jnp.where(row == 0, before[6:7, :], jnp.where(row == 1, before[7:8, :], pltpu.roll(p, 2, 0)))
    return p1, p2


def _layer_a_out(u, x, conv, w_out_a, gain_kv, gain_b, w_kv, w_in_b):
    s, d = x.shape
    a = w_kv.shape[1] // 2
    tm = min(ROW_TILE, s)
    scale = HEAD_DIM ** -0.5

    def body(u_ref, x_ref, conv_ref, woa_ref, gkv_ref, gb_ref, wkv_ref, wib_ref,
             x1_ref, kv_ref, q_ref, gate_ref, halo_ref):
        @pl.when(pl.program_id(0) == 0)
        def _():
            halo_ref[...] = jnp.zeros_like(halo_ref)

        b, c, xin, g = (u_ref[:, k * d:(k + 1) * d] for k in range(4))
        p = c * xin
        p1, p2 = _shift_rows_down(p, halo_ref[...], tm)
        halo_ref[...] = p[tm - 8:tm, :]
        w = conv_ref[...]
        cv = w[0:1, :] * p2 + w[1:2, :] * p1 + w[2:3, :] * p
        ya = (b * cv) * (g * _sigmoid(g))
        x1 = x_ref[...] + _dot(_mx(ya), woa_ref[...])
        x1_ref[...] = x1
        _, xh = _rms_stats(x1)
        kv_ref[...] = _dot(_mx(xh * gkv_ref[...]), wkv_ref[...]).astype(kv_ref.dtype)
        qg = _dot(_mx(xh * gb_ref[...]), wib_ref[...])
        q_ref[...] = (qg[:, :a] * scale).astype(q_ref.dtype)
        gate_ref[...] = qg[:, a:]

    return pl.pallas_call(
        body, name="layer_a_out", grid=(s // tm,),
        in_specs=[_row_spec(tm, 4 * d), _row_spec(tm, d), _full_spec((8, d)), _full_spec((d, d)),
                  _full_spec((1, d)), _full_spec((1, d)), _full_spec((d, 2 * a)), _full_spec((d, 2 * a))],
        out_specs=[_row_spec(tm, d), _row_spec(tm, 2 * a), _row_spec(tm, a), _row_spec(tm, a)],
        out_shape=[jax.ShapeDtypeStruct((s, d), _F32), jax.ShapeDtypeStruct((s, 2 * a), _MXU_DTYPE),
                   jax.ShapeDtypeStruct((s, a), _MXU_DTYPE), jax.ShapeDtypeStruct((s, a), _F32)],
        scratch_shapes=[pltpu.VMEM((8, d), _F32)],
        compiler_params=_params(1),
    )(u, x, conv, w_out_a, gain_kv, gain_b, w_kv, w_in_b)


def _neg_softplus(z):
    e = jnp.exp(-jnp.abs(z))
    return -(jnp.maximum(z, 0.0) + jnp.log(1.0 + e)), e


def _split_dot(val, tri):
    hi = _mx(val)
    lo = _mx(val - hi.astype(_F32))
    return _dot(hi, tri) + _dot(lo, tri)


def _head_masks():
    lane = lax.broadcasted_iota(jnp.int32, (1, LANES), 1)
    return [lane < HEAD_DIM, lane >= HEAD_DIM]


def _attn_fwd(q, kv, tri_suffix):
    s, a = q.shape
    n_hp = a // LANES
    tq = min(ATT_TILE, s)
    nq = s // tq

    def body(q_ref, k_ref, v_ref, tri_ref, o_ref, c_ref, stop_ref):
        hp, i = pl.program_id(0), pl.program_id(1)
        qv = q_ref[...]
        tri = tri_ref[...]
        row = lax.broadcasted_iota(jnp.int32, (tq, tq), 0)
        col = lax.broadcasted_iota(jnp.int32, (tq, tq), 1)
        causal = col < row

        def block(j, qh, c, acc, diag=False, live=None):
            off = pl.multiple_of(j * tq, tq)
            kb = k_ref[pl.ds(off, tq), :]
            vb = v_ref[pl.ds(off, tq), :]
            z = _dot_nt(qh, kb)
            lg, _ = _neg_softplus(z)
            if diag:
                lg = jnp.where(causal, lg, 0.0)
            tot = jnp.sum(lg, axis=1, keepdims=True)
            c_in = c
            if live is not None:
                c_in = c + (live - 1.0) * 1e30
                tot = tot * live
            w = jnp.exp(z + _split_dot(lg, tri) + c_in)
            if diag:
                w = jnp.where(causal, w, 0.0)
            return c + tot, acc + _dot(_mx(w), vb)

        qhs = [jnp.where(hmask, qv, jnp.zeros_like(qv)) for hmask in _head_masks()]
        st = [block(i, qh, jnp.zeros((tq, 1), _F32), jnp.zeros((tq, LANES), _F32), diag=True) for qh in qhs]
        live = jnp.where(i >= 1, 1.0, 0.0)
        st = [block(jnp.maximum(i - 1, 0), qh, c, acc, live=live) for qh, (c, acc) in zip(qhs, st)]

        def cond(carry):
            alive = jnp.maximum(jnp.max(carry[1]), jnp.max(carry[3])) > SKIP_LOG
            return jnp.logical_and(carry[0] >= 0, alive)

        def step(carry):
            j, c0, acc0, c1, acc1 = carry
            c0, acc0 = block(j, qhs[0], c0, acc0)
            c1, acc1 = block(j, qhs[1], c1, acc1)
            return j - 1, c0, acc0, c1, acc1

        j, c0, acc0, c1, acc1 = lax.while_loop(cond, step, (i - 2, *st[0], *st[1]))
        stop_ref[hp, i] = jnp.maximum(jnp.minimum(j + 1, i - 1), 0)
        first = _head_masks()[0]
        o_ref[...] = jnp.where(first, acc0, acc1)
        c_ref[...] = jnp.where(first, c0, c1)

    q_spec = pl.BlockSpec((tq, LANES), lambda h, i: (i, h))
    return pl.pallas_call(
        body, name="attn_fwd", grid=(n_hp, nq),
        in_specs=[q_spec, pl.BlockSpec((s, LANES), lambda h, i: (0, h)),
                  pl.BlockSpec((s, LANES), lambda h, i: (0, n_hp + h)), _full_spec((tq, tq))],
        out_specs=[q_spec, q_spec, pl.BlockSpec(memory_space=pltpu.SMEM)],
        out_shape=[jax.ShapeDtypeStruct((s, a), _F32), jax.ShapeDtypeStruct((s, a), _F32),
                   jax.ShapeDtypeStruct((n_hp, nq), jnp.int32)],
        compiler_params=_params(2),
    )(q, kv, kv, tri_suffix)


def _layer_b_out_loss(o, gate, x1, w_out_b, gain_f, target):
    s, d = x1.shape
    a = o.shape[1]
    tm = min(ROW_TILE, s)

    def body(o_ref, gate_ref, x1_ref, wob_ref, gf_ref, tgt_ref, dx2_ref, gnf_ref, loss_ref):
        @pl.when(pl.program_id(0) == 0)
        def _():
            gnf_ref[...] = jnp.zeros_like(gnf_ref)
            loss_ref[...] = jnp.zeros_like(loss_ref)

        gt = gate_ref[...]
        ob = o_ref[...] * (gt * _sigmoid(gt))
        x2 = x1_ref[...] + _dot(_mx(ob), wob_ref[...])
        r, xh = _rms_stats(x2)
        gf = gf_ref[...]
        err = xh * gf - tgt_ref[...]
        part = jnp.sum(jnp.sum(err * err, axis=1, keepdims=True), axis=0, keepdims=True)
        loss_ref[...] += part * (0.5 / d)
        dy = err * (1.0 / d)
        gnf_ref[...] += jnp.sum(dy * xh, axis=0, keepdims=True)
        dx2_ref[...] = _rms_bwd(r, xh, dy * gf)

    return pl.pallas_call(
        body, name="layer_b_out_loss", grid=(s // tm,),
        in_specs=[_row_spec(tm, a), _row_spec(tm, a), _row_spec(tm, d), _full_spec((a, d)), _full_spec((1, d)),
                  _row_spec(tm, d)],
        out_specs=[_row_spec(tm, d), _full_spec((1, d)), _full_spec((1, LANES))],
        out_shape=[jax.ShapeDtypeStruct((s, d), _F32), jax.ShapeDtypeStruct((1, d), _F32),
                   jax.ShapeDtypeStruct((1, LANES), _F32)],
        compiler_params=_params(1),
    )(o, gate, x1, w_out_b, gain_f, target)


def _silu_and_grad(g):
    sg = _sigmoid(g)
    return g * sg, sg * (1.0 + g * (1.0 - sg))


def _layer_b_bwd(dx2, o, gate, w_out_b):
    s, d = dx2.shape
    a = o.shape[1]
    tm = min(ROW_TILE, s)

    def body(dx2_ref, o_ref, gate_ref, wob_ref, do_ref, dgate_ref, gw_ref):
        @pl.when(pl.program_id(0) == 0)
        def _():
            gw_ref[...] = jnp.zeros_like(gw_ref)

        dxb = _mx(dx2_ref[...])
        sl, dsl = _silu_and_grad(gate_ref[...])
        ov = o_ref[...]
        d_ob = _dot_nt(dxb, wob_ref[...])
        gw_ref[...] += _dot_tn(_mx(ov * sl), dxb)
        do_ref[...] = (d_ob * sl).astype(do_ref.dtype)
        dgate_ref[...] = (d_ob * ov * dsl).astype(dgate_ref.dtype)

    return pl.pallas_call(
        body, name="layer_b_bwd", grid=(s // tm,),
        in_specs=[_row_spec(tm, d), _row_spec(tm, a), _row_spec(tm, a), _full_spec((a, d))],
        out_specs=[_row_spec(tm, a), _row_spec(tm, a), _full_spec((a, d))],
        out_shape=[jax.ShapeDtypeStruct((s, a), _MXU_DTYPE), jax.ShapeDtypeStruct((s, a), _MXU_DTYPE),
                   jax.ShapeDtypeStruct((a, d), _F32)],
        compiler_params=_params(1),
    )(dx2, o, gate, w_out_b)


def _attn_bwd(q, kv, do, c_tot, stop, tri_suffix, tri_prefix):
    s, a = q.shape
    n_hp = a // LANES
    tq = min(ATT_TILE, s)
    nq = s // tq
    scale = HEAD_DIM ** -0.5
    chunk = min(1024, s)

    def body(stop_ref, q_ref, do_ref, c_ref, k_ref, v_ref, ts_ref, tp_ref, dq_ref, dk_hbm, dv_hbm,
             dk_acc, dv_acc, stage, sem):
        hp, i = pl.program_id(0), pl.program_id(1)

        @pl.when(i == 0)
        def _():
            dk_acc[...] = jnp.zeros_like(dk_acc)
            dv_acc[...] = jnp.zeros_like(dv_acc)

        qv, dov, cv = q_ref[...], do_ref[...], c_ref[...]
        ts, tp = ts_ref[...], tp_ref[...]
        row = lax.broadcasted_iota(jnp.int32, (tq, tq), 0)
        col = lax.broadcasted_iota(jnp.int32, (tq, tq), 1)
        causal = col < row

        def block(j, h2, st, diag=False, live=None):
            asc, pre, dq = st
            qh, doh, ch = heads[h2]
            off = pl.multiple_of(j * tq, tq)
            kb = k_ref[pl.ds(off, tq), :]
            vb = v_ref[pl.ds(off, tq), :]
            z = _dot_nt(qh, kb)
            lg, e = _neg_softplus(z)
            inv = 1.0 / (1.0 + e)
            sig = jnp.where(z >= 0.0, inv, e * inv)
            if diag:
                lg = jnp.where(causal, lg, 0.0)
            tot = jnp.sum(lg, axis=1, keepdims=True)
            newer = jnp.zeros_like(tot) if diag else ch - asc - tot
            if live is not None:
                newer = newer + (live - 1.0) * 1e30
                tot = tot * live
            wgt = jnp.exp(z + _split_dot(lg, ts) + newer)
            if diag:
                wgt = jnp.where(causal, wgt, 0.0)
            g = wgt * _dot_nt(doh, vb)
            dz = g - sig * (_split_dot(g, tp) + pre)
            if diag:
                dz = jnp.where(causal, dz, 0.0)
            dzb = _mx(dz)
            carry = (asc + tot, pre + jnp.sum(g, axis=1, keepdims=True), dq + _dot(dzb, kb))
            return carry, _dot_tn(dzb, qh), _dot_tn(_mx(wgt), doh)

        def add_kv(j, parts):
            rows = pl.ds(pl.multiple_of(j * tq, tq), tq)
            dk_acc[rows, :] += parts[0][1] + parts[1][1]
            dv_acc[rows, :] += parts[0][2] + parts[1][2]

        heads = []
        for h2, hmask in enumerate(_head_masks()):
            heads.append((jnp.where(hmask, qv, jnp.zeros_like(qv)), jnp.where(hmask, dov, jnp.zeros_like(dov)),
                          cv[:, h2 * HEAD_DIM:h2 * HEAD_DIM + 1]))
        zero = (jnp.zeros((tq, 1), _F32), jnp.zeros((tq, 1), _F32), jnp.zeros((tq, LANES), _F32))

        def step(j, carry):
            parts = [block(j, h2, carry[h2]) for h2 in range(2)]
            add_kv(j, parts)
            return parts[0][0], parts[1][0]

        first = jnp.clip(stop_ref[hp, i], 0, jnp.maximum(i - 1, 0))
        carry = lax.fori_loop(first, i - 1, step, (zero, zero))
        live = jnp.where(i >= 1, 1.0, 0.0)
        before = jnp.maximum(i - 1, 0)
        parts = [block(before, h2, carry[h2], live=live) for h2 in range(2)]
        add_kv(before, parts)
        parts = [block(i, h2, parts[h2][0], diag=True) for h2 in range(2)]
        add_kv(i, parts)
        dq_ref[...] = (jnp.where(_head_masks()[0], parts[0][0][2], parts[1][0][2]) * scale).astype(dq_ref.dtype)

        @pl.when(i == nq - 1)
        def _():
            cols = pl.ds(pl.multiple_of(hp * LANES, LANES), LANES)
            for acc, out in ((dk_acc, dk_hbm), (dv_acc, dv_hbm)):
                def cast(n, carry, acc=acc):
                    rows = pl.ds(pl.multiple_of(n * chunk, chunk), chunk)
                    stage[rows, :] = acc[rows, :].astype(stage.dtype)
                    return carry
                lax.fori_loop(0, s // chunk, cast, 0)
                cp = pltpu.make_async_copy(stage, out.at[:, cols], sem)
                cp.start()
                cp.wait()

    q_spec = pl.BlockSpec((tq, LANES), lambda h, i, *_: (i, h))
    grid_spec = pltpu.PrefetchScalarGridSpec(
        num_scalar_prefetch=1, grid=(n_hp, nq),
        in_specs=[q_spec, q_spec, q_spec, pl.BlockSpec((s, LANES), lambda h, i, *_: (0, h)),
                  pl.BlockSpec((s, LANES), lambda h, i, *_: (0, n_hp + h)),
                  pl.BlockSpec((tq, tq), lambda h, i, *_: (0, 0)), pl.BlockSpec((tq, tq), lambda h, i, *_: (0, 0))],
        out_specs=[q_spec, pl.BlockSpec(memory_space=pl.ANY), pl.BlockSpec(memory_space=pl.ANY)],
        scratch_shapes=[pltpu.VMEM((s, LANES), _F32), pltpu.VMEM((s, LANES), _F32), pltpu.VMEM((s, LANES), _MXU_DTYPE),
                        pltpu.SemaphoreType.DMA],
    )
    return pl.pallas_call(
        body, name="attn_bwd", grid_spec=grid_spec,
        out_shape=[jax.ShapeDtypeStruct((s, a), _MXU_DTYPE)] * 3,
        compiler_params=_params(2),
    )(stop, q, do, c_tot, kv, kv, tri_suffix, tri_prefix)


def _proj_bwd(x1, dx2, dq, dgate, dk, dv, w_in_b, w_kv, gain_b, gain_kv):
    s, d = x1.shape
    a = dq.shape[1]
    tm = min(ROW_TILE, s)

    def body(x1_ref, dx2_ref, dq_ref, dgate_ref, dk_ref, dv_ref, wib_ref, wkv_ref, gb_ref, gkv_ref,
             dx1_ref, gwib_ref, gwkv_ref, gnb_ref, gnkv_ref):
        @pl.when(pl.program_id(0) == 0)
        def _():
            for ref in (gwib_ref, gwkv_ref, gnb_ref, gnkv_ref):
                ref[...] = jnp.zeros_like(ref)

        r, xh = _rms_stats(x1_ref[...])
        gb, gkv = gb_ref[...], gkv_ref[...]
        hb, hk = _mx(xh * gb), _mx(xh * gkv)
        dq, dgate, dk, dv = dq_ref[...], dgate_ref[...], dk_ref[...], dv_ref[...]
        gwib_ref[:, :a] += _dot_tn(hb, dq)
        gwib_ref[:, a:] += _dot_tn(hb, dgate)
        gwkv_ref[:, :a] += _dot_tn(hk, dk)
        gwkv_ref[:, a:] += _dot_tn(hk, dv)
        d_hb = _dot_nt(dq, wib_ref[:, :a]) + _dot_nt(dgate, wib_ref[:, a:])
        d_hk = _dot_nt(dk, wkv_ref[:, :a]) + _dot_nt(dv, wkv_ref[:, a:])
        gnb_ref[...] += jnp.sum(d_hb * xh, axis=0, keepdims=True)
        gnkv_ref[...] += jnp.sum(d_hk * xh, axis=0, keepdims=True)
        dx1_ref[...] = dx2_ref[...] + _rms_bwd(r, xh, d_hb * gb + d_hk * gkv)

    return pl.pallas_call(
        body, name="proj_bwd", grid=(s // tm,),
        in_specs=[_row_spec(tm, d), _row_spec(tm, d)] + [_row_spec(tm, a)] * 4
        + [_full_spec((d, 2 * a)), _full_spec((d, 2 * a)), _full_spec((1, d)), _full_spec((1, d))],
        out_specs=[_row_spec(tm, d), _full_spec((d, 2 * a)), _full_spec((d, 2 * a)), _full_spec((1, d)),
                   _full_spec((1, d))],
        out_shape=[jax.ShapeDtypeStruct((s, d), _F32), jax.ShapeDtypeStruct((d, 2 * a), _F32),
                   jax.ShapeDtypeStruct((d, 2 * a), _F32), jax.ShapeDtypeStruct((1, d), _F32),
                   jax.ShapeDtypeStruct((1, d), _F32)],
        compiler_params=_params(1),
    )(x1, dx2, dq, dgate, dk, dv, w_in_b, w_kv, gain_b, gain_kv)


def _layer_a_bwd(u, dx1, conv, w_out_a):
    s, d = dx1.shape
    tm = min(ROW_TILE, s)
    n = s // tm
    per8 = tm // 8

    def body(u_ref, uprev_ref, dx1_ref, conv_ref, woa_ref, du_ref, gwoa_ref, gconv_ref, halo_ref):
        step = pl.program_id(0)

        @pl.when(step == 0)
        def _():
            for ref in (gwoa_ref, gconv_ref, halo_ref):
                ref[...] = jnp.zeros_like(ref)

        b, c, xin, g = (u_ref[:, k * d:(k + 1) * d] for k in range(4))
        p = c * xin
        before = uprev_ref[:, d:2 * d] * uprev_ref[:, 2 * d:3 * d]
        before = jnp.where(step == n - 1, jnp.zeros_like(before), before)
        p1, p2 = _shift_rows_down(p, before, tm)
        w = conv_ref[...]
        cv = w[0:1, :] * p2 + w[1:2, :] * p1 + w[2:3, :] * p
        sl, dsl = _silu_and_grad(g)
        y = b * cv
        dxb = _mx(dx1_ref[...])
        gwoa_ref[...] += _dot_tn(_mx(y * sl), dxb)
        d_ya = _dot_nt(dxb, woa_ref[...])
        d_y = d_ya * sl
        d_cv = d_y * b
        after = halo_ref[...]
        row = lax.broadcasted_iota(jnp.int32, p.shape, 0)
        n1 = jnp.where(row == tm - 1, after[0:1, :], pltpu.roll(d_cv, tm - 1, 0))
        n2 = jnp.where(row == tm - 2, after[0:1, :],
                       jnp.where(row == tm - 1, after[1:2, :], pltpu.roll(d_cv, tm - 2, 0)))
        halo_ref[...] = d_cv[0:8, :]
        d_p = w[2:3, :] * d_cv + w[1:2, :] * n1 + w[0:1, :] * n2
        gconv_ref[0:1, :] += jnp.sum(d_cv * p2, axis=0, keepdims=True)
        gconv_ref[1:2, :] += jnp.sum(d_cv * p1, axis=0, keepdims=True)
        gconv_ref[2:3, :] += jnp.sum(d_cv * p, axis=0, keepdims=True)
        du_ref[:, 0:d] = (d_y * cv).astype(du_ref.dtype)
        du_ref[:, d:2 * d] = (d_p * xin).astype(du_ref.dtype)
        du_ref[:, 2 * d:3 * d] = (d_p * c).astype(du_ref.dtype)
        du_ref[:, 3 * d:4 * d] = (d_ya * y * dsl).astype(du_ref.dtype)

    def rev(i):
        return (n - 1 - i, 0)

    return pl.pallas_call(
        body, name="layer_a_bwd", grid=(n,),
        in_specs=[pl.BlockSpec((tm, 4 * d), rev),
                  pl.BlockSpec((8, 4 * d), lambda i: (jnp.maximum((n - 1 - i) * per8 - 1, 0), 0)),
                  pl.BlockSpec((tm, d), rev), _full_spec((8, d)), _full_spec((d, d))],
        out_specs=[pl.BlockSpec((tm, 4 * d), rev), _full_spec((d, d)), _full_spec((8, d))],
        out_shape=[jax.ShapeDtypeStruct((s, 4 * d), _MXU_DTYPE), jax.ShapeDtypeStruct((d, d), _F32),
                   jax.ShapeDtypeStruct((8, d), _F32)],
        scratch_shapes=[pltpu.VMEM((8, d), _F32)],
        compiler_params=_params(1),
    )(u, u, dx1, conv, w_out_a)


def _grad_w_in_a(h, du, nb, grads):
    s, d = h.shape
    bn = du.shape[1] // nb
    tm = min(2 * ROW_TILE, s)
    half = nb // 2
    n = len(grads)
    steps = s // tm

    def body(h_ref, du_ref, *refs):
        gw_ref = refs[n]
        start, wait = _exchange_ops(refs[:n], refs[n + 1:2 * n + 1], *refs[2 * n + 1:], True)
        jh, i = pl.program_id(0), pl.program_id(1)
        pl.when(jnp.logical_and(jh == 0, i == 0))(start)

        @pl.when(i == 0)
        def _():
            gw_ref[...] = jnp.zeros_like(gw_ref)

        hv = h_ref[...]
        for j in range(half):
            gw_ref[j] += _dot_tn(hv, du_ref[:, j * bn:(j + 1) * bn])
        pl.when(jnp.logical_and(jh == 1, i == steps - 1))(wait)

    outs = pl.pallas_call(
        body, name="grad_w_in_a", grid=(2, steps),
        in_specs=[pl.BlockSpec((tm, d), lambda jh, i: (i, 0)), pl.BlockSpec((tm, half * bn), lambda jh, i: (i, jh))]
        + [_ANY] * n,
        out_specs=[pl.BlockSpec((half, d, bn), lambda jh, i: (jh, 0, 0))] + [_ANY] * n,
        out_shape=[jax.ShapeDtypeStruct((nb, d, bn), _F32)] + _exchange_shapes(grads, True),
        scratch_shapes=_exchange_sems(n),
        compiler_params=_params(2),
    )(h, du, *grads)
    return outs[0], outs[1:]


def _input_grad(x, dx1, du, w_blk, gain):
    s, d = x.shape
    nb, _, bn = w_blk.shape
    tm = min(ROW_TILE, s)

    def body(x_ref, dx1_ref, du_ref, w_ref, g_ref, dx_ref, gn_ref):
        @pl.when(pl.program_id(0) == 0)
        def _():
            gn_ref[...] = jnp.zeros_like(gn_ref)

        r, xh = _rms_stats(x_ref[...])
        d_h = _dot_nt(du_ref[:, 0:bn], w_ref[0])
        for j in range(1, nb):
            d_h = d_h + _dot_nt(du_ref[:, j * bn:(j + 1) * bn], w_ref[j])
        gn_ref[...] += jnp.sum(d_h * xh, axis=0, keepdims=True)
        dx_ref[...] = dx1_ref[...] + _rms_bwd(r, xh, d_h * g_ref[...])

    return pl.pallas_call(
        body, name="input_grad", grid=(s // tm,),
        in_specs=[_row_spec(tm, d), _row_spec(tm, d), _row_spec(tm, nb * bn), _full_spec((nb, d, bn)),
                  _full_spec((1, d))],
        out_specs=[_row_spec(tm, d), _full_spec((1, d))],
        out_shape=[jax.ShapeDtypeStruct((s, d), _F32), jax.ShapeDtypeStruct((1, d), _F32)],
        compiler_params=_params(1),
    )(x, dx1, du, w_blk, gain)


def _reduce_adamw(name, parts, w, m, v):
    rows, cols = w.shape
    tr = min(256, rows)
    c1 = 1.0 - ADAM_B1 ** ADAM_STEP
    c2 = 1.0 - ADAM_B2 ** ADAM_STEP

    def body(p_ref, w_ref, m_ref, v_ref, g_ref, d_ref, nm_ref, nv_ref):
        g = p_ref[0].astype(_F32)
        for k in range(1, N_DEV):
            g = g + p_ref[k].astype(_F32)
        nm = ADAM_B1 * m_ref[...] + (1.0 - ADAM_B1) * g
        nv = ADAM_B2 * v_ref[...] + (1.0 - ADAM_B2) * (g * g)
        g_ref[...] = g
        nm_ref[...] = nm
        nv_ref[...] = nv
        d_ref[...] = -ADAM_LR * ((nm / c1) / (jnp.sqrt(nv / c2) + ADAM_EPS) + ADAM_WD * w_ref[...])

    tile = _row_spec(tr, cols)
    return pl.pallas_call(
        body, name=name, grid=(rows // tr,),
        in_specs=[pl.BlockSpec((N_DEV, tr, cols), lambda i: (0, i, 0)), tile, tile, tile],
        out_specs=[tile] * 4,
        out_shape=[jax.ShapeDtypeStruct((rows, cols), _F32)] * 4,
        compiler_params=_params(1),
    )(parts, w, m, v)


def _pad_rows(a, rows=8):
    return jnp.pad(a, ((0, rows - a.shape[0]), (0, 0)))


def kernel(x, norm_a, w_in_a, conv_a, w_out_a, norm_kv, w_kv, norm_b, w_in_b, w_out_b, norm_f, loss_target, m_norm_a, m_w_in_a, m_conv_a, m_w_out_a, m_norm_kv, m_w_kv, m_norm_b, m_w_in_b, m_w_out_b, m_norm_f, v_norm_a, v_w_in_a, v_conv_a, v_w_out_a, v_norm_kv, v_w_kv, v_norm_b, v_w_in_b, v_w_out_b, v_norm_f):
    x0 = x[0]
    s, d = x0.shape
    a = d // 2
    sh = d // N_DEV
    me = 4 * lax.axis_index("x") + 2 * lax.axis_index("y") + lax.axis_index("c")

    small_a = _pad_rows(jnp.concatenate([norm_a, conv_a[0]], axis=0))
    wia_g, small_g = _exchange("exchange_gather", [_mx(w_in_a[0]), small_a], scatter=False)
    small_f = small_g.transpose(1, 0, 2).reshape(8, d)
    gain_a = small_f[0:1]
    conv_f = _pad_rows(small_f[1:4])
    gain_kv, gain_b, gain_f = norm_kv.reshape(1, d), norm_b.reshape(1, d), norm_f.reshape(1, d)

    u, h, (woa_g, wkv_g, wib_g, wob_g) = _ln_matmul_in_a(
        x0, gain_a, wia_g, [_mx(w_out_a[0]), _mx(w_kv), _mx(w_in_b[0]), _mx(w_out_b[0])])
    woa_f = woa_g.reshape(d, d)
    wkv_f = wkv_g.reshape(d, 2 * a)
    wib_f = wib_g.reshape(d, 2 * a)
    wob_f = wob_g.transpose(1, 0, 2).reshape(a, d)
    x1, kv, q, gate = _layer_a_out(u, x0, conv_f, woa_f, gain_kv, gain_b, wkv_f, wib_f)
    tq = min(ATT_TILE, s)
    idx = jnp.arange(tq)
    tri_suffix = _mx(idx[:, None] >= idx[None, :])
    tri_prefix = _mx(idx[:, None] <= idx[None, :])
    o, c_tot, stop = _attn_fwd(q, kv, tri_suffix)
    dx2, g_norm_f, loss_part = _layer_b_out_loss(o, gate, x1, wob_f, gain_f, loss_target[0])

    do, dgate, g_wob = _layer_b_bwd(dx2, o, gate, wob_f)
    dq, dk, dv = _attn_bwd(q, kv, do, c_tot, stop, tri_suffix, tri_prefix)
    dx1, g_wib, g_wkv, g_norm_b, g_norm_kv = _proj_bwd(x1, dx2, dq, dgate, dk, dv, wib_f, wkv_f, gain_b, gain_kv)
    du, g_woa, g_conv = _layer_a_bwd(u, dx1, conv_f, woa_f)
    dx0, g_norm_a = _input_grad(x0, dx1, du, wia_g, gain_a)

    g_wia, (p_woa, p_wkv, p_wib, p_wob) = _grad_w_in_a(
        h, du, N_DEV,
        [_mx(g_woa).reshape(N_DEV, sh, d), _mx(g_wkv).reshape(N_DEV, sh, 2 * a), _mx(g_wib).reshape(N_DEV, sh, 2 * a),
         _mx(g_wob).reshape(a, N_DEV, sh).transpose(1, 0, 2)])
    (p_wia,) = _exchange("exchange_scatter", [_mx(g_wia)], scatter=True)
    small_grads = jnp.concatenate(
        [g_norm_a, g_conv[0:3], g_norm_kv, g_norm_b, g_norm_f, jnp.pad(loss_part, ((0, 0), (0, d - LANES)))], axis=0)
    (p_small,) = _exchange("exchange_small", [small_grads], scatter=False)

    upd_wia = _reduce_adamw("adamw_w_in_a", p_wia, w_in_a[0], m_w_in_a[0], v_w_in_a[0])
    upd_woa = _reduce_adamw("adamw_w_out_a", p_woa, w_out_a[0], m_w_out_a[0], v_w_out_a[0])
    upd_wkv = _reduce_adamw("adamw_w_kv", p_wkv, w_kv, m_w_kv, v_w_kv)
    upd_wib = _reduce_adamw("adamw_w_in_b", p_wib, w_in_b[0], m_w_in_b[0], v_w_in_b[0])
    upd_wob = _reduce_adamw("adamw_w_out_b", p_wob, w_out_b[0], m_w_out_b[0], v_w_out_b[0])

    def rep(a1, a2, a3):
        return jnp.concatenate([jnp.zeros((4, d), _F32), a1.reshape(1, d), a2.reshape(1, d), a3.reshape(1, d),
                                jnp.zeros((1, d), _F32)], axis=0)

    upd_rep = _reduce_adamw("adamw_replicated", p_small, rep(norm_kv, norm_b, norm_f),
                            rep(m_norm_kv, m_norm_b, m_norm_f), rep(v_norm_kv, v_norm_b, v_norm_f))

    def mine(n1, cv1):
        return _pad_rows(jnp.concatenate([n1, cv1[0]], axis=0))

    p_mine = lax.dynamic_slice(p_small, (0, 0, me * sh), (N_DEV, 8, sh))
    upd_mine = _reduce_adamw("adamw_sharded_small", p_mine, mine(norm_a, conv_a), mine(m_norm_a, m_conv_a),
                             mine(v_norm_a, v_conv_a))

    loss = upd_rep[0][7, 0]
    groups = []
    for k in range(4):
        groups.append([
            upd_mine[k][0:1], upd_wia[k][None], upd_mine[k][1:4][None], upd_woa[k][None], upd_rep[k][4],
            upd_wkv[k], upd_rep[k][5:6], upd_wib[k][None], upd_wob[k][None], upd_rep[k][6]])
    return (loss, dx0[None], *groups[0], *groups[1], *groups[2], *groups[3])
```

```python
import functools

import jax
import jax.numpy as jnp
from jax import lax
from jax.experimental import pallas as pl
from jax.experimental.pallas import tpu as pltpu

_MXU_DTYPE = jnp.bfloat16
_F32 = jnp.float32

RMS_EPS = 1e-6
HEAD_DIM = 64
LANES = 128
N_DEV = 8
ATT_TILE = 256
ROW_TILE = 256
VMEM_LIMIT = 56 * 1024 * 1024
SKIP_LOG = -110.0

ADAM_LR = 0.001
ADAM_B1 = 0.9
ADAM_B2 = 0.999
ADAM_EPS = 1e-08
ADAM_WD = 0.01
ADAM_STEP = 10

_NT = (((1,), (1,)), ((), ()))
_TN = (((0,), (0,)), ((), ()))


def _params(n_grid):
    return pltpu.CompilerParams(dimension_semantics=("arbitrary",) * n_grid, vmem_limit_bytes=VMEM_LIMIT)


def _dot(a, b):
    return jnp.dot(a, b, preferred_element_type=_F32)


def _dot_nt(a, b):
    return lax.dot_general(a, b, _NT, preferred_element_type=_F32)


def _dot_tn(a, b):
    return lax.dot_general(a, b, _TN, preferred_element_type=_F32)


def _mx(a):
    return a.astype(_MXU_DTYPE)


def _sigmoid(a):
    return 1.0 / (1.0 + jnp.exp(-a))


def _rms_stats(xv):
    r = lax.rsqrt(jnp.mean(xv * xv, axis=-1, keepdims=True) + RMS_EPS)
    return r, xv * r


def _rms_bwd(r, xh, dyg):
    return r * (dyg - xh * jnp.mean(dyg * xh, axis=-1, keepdims=True))


def _row_spec(tm, width):
    return pl.BlockSpec((tm, width), lambda i: (i, 0))


def _full_spec(shape):
    zeros = (0,) * len(shape)
    return pl.BlockSpec(shape, lambda *_: zeros)


def _exchange_ops(ins, outs, send_sems, recv_sems, local_sems, scatter):
    n = len(ins)
    x, y, c = lax.axis_index("x"), lax.axis_index("y"), lax.axis_index("c")
    me = 4 * x + 2 * y + c

    def remote(t, m, landed):
        px = 1 - x if m & 4 else x
        py = 1 - y if m & 2 else y
        pc = 1 - c if m & 1 else c
        idx = 4 * px + 2 * py + pc
        k = t * (N_DEV - 1) + m - 1
        return pltpu.make_async_remote_copy(
            src_ref=ins[t].at[idx] if scatter else ins[t], dst_ref=outs[t].at[idx if landed else me],
            send_sem=send_sems.at[k], recv_sem=recv_sems.at[k],
            device_id=(px, py, pc), device_id_type=pl.DeviceIdType.MESH)

    def local(t):
        return pltpu.make_async_copy(ins[t].at[me] if scatter else ins[t], outs[t].at[me], local_sems.at[t])

    def start():
        for t in range(n):
            local(t).start()
        for m in range(1, N_DEV):
            for t in range(n):
                remote(t, m, False).start()

    def wait():
        for m in range(1, N_DEV):
            for t in range(n):
                remote(t, m, True).wait_recv()
        for m in range(1, N_DEV):
            for t in range(n):
                remote(t, m, False).wait_send()
        for t in range(n):
            local(t).wait()

    return start, wait


def _exchange_shapes(arrays, scatter):
    return [jax.ShapeDtypeStruct((N_DEV,) + (a.shape[1:] if scatter else a.shape), a.dtype) for a in arrays]


def _exchange_sems(n):
    return [pltpu.SemaphoreType.DMA((n * (N_DEV - 1),)), pltpu.SemaphoreType.DMA((n * (N_DEV - 1),)),
            pltpu.SemaphoreType.DMA((n,))]


_ANY = pl.BlockSpec(memory_space=pl.ANY)


def _exchange(name, arrays, scatter):
    n = len(arrays)

    def body(*refs):
        start, wait = _exchange_ops(refs[:n], refs[n:2 * n], *refs[2 * n:], scatter)
        start()
        wait()

    return pl.pallas_call(
        body, name=name, in_specs=[_ANY] * n, out_specs=[_ANY] * n,
        out_shape=_exchange_shapes(arrays, scatter), scratch_shapes=_exchange_sems(n),
    )(*arrays)


def _ln_matmul_in_a(x, gain, w_blk, shards):
    s, d = x.shape
    nb, _, bn = w_blk.shape
    tm = min(ROW_TILE, s)
    n = len(shards)
    last = s // tm - 1

    def body(x_ref, g_ref, w_ref, *refs):
        u_ref, h_ref = refs[n:n + 2]
        start, wait = _exchange_ops(refs[:n], refs[n + 2:2 * n + 2], *refs[2 * n + 2:], False)
        pl.when(pl.program_id(0) == 0)(start)
        _, xh = _rms_stats(x_ref[...])
        h = _mx(xh * g_ref[...])
        h_ref[...] = h
        for j in range(nb):
            u_ref[:, j * bn:(j + 1) * bn] = _dot(h, w_ref[j])
        pl.when(pl.program_id(0) == last)(wait)

    outs = pl.pallas_call(
        body, name="ln_matmul_in_a", grid=(s // tm,),
        in_specs=[_row_spec(tm, d), _full_spec((1, d)), _full_spec((nb, d, bn))] + [_ANY] * n,
        out_specs=[_row_spec(tm, nb * bn), _row_spec(tm, d)] + [_ANY] * n,
        out_shape=[jax.ShapeDtypeStruct((s, nb * bn), _F32), jax.ShapeDtypeStruct((s, d), _MXU_DTYPE)]
        + _exchange_shapes(shards, False),
        scratch_shapes=_exchange_sems(n),
        compiler_params=_params(1),
    )(x, gain, w_blk, *shards)
    return outs[0], outs[1], outs[2:]


def _shift_rows_down(p, before, tm):
    row = lax.broadcasted_iota(jnp.int32, (8, p.shape[1]), 0)
    r1, r2 = pltpu.roll(p, 1, 0), pltpu.roll(p, 2, 0)
    top1 = jnp.where(row == 0, before[7:8, :], r1[0:8, :])
    top2 = jnp.where(row == 0, before[6:7, :], jnp.where(row == 1, before[7:8, :], r2[0:8, :]))
    return jnp.concatenate([top1, r1[8:, :]], axis=0), jnp.concatenate([top2, r2[8:, :]], axis=0)


def _shift_rows_up(p, after, tm):
    row = lax.broadcasted_iota(jnp.int32, (8, p.shape[1]), 0)
    r1, r2 = pltpu.roll(p, tm - 1, 0), pltpu.roll(p, tm - 2, 0)
    end1 = jnp.where(row == 7, after[0:1, :], r1[tm - 8:, :])
    end2 = jnp.where(row == 6, after[0:1, :], jnp.where(row == 7, after[1:2, :], r2[tm - 8:, :]))
    return jnp.concatenate([r1[:tm - 8, :], end1], axis=0), jnp.concatenate([r2[:tm - 8, :], end2], axis=0)


def _layer_a_out(u, x, conv, w_out_a, gain_kv, gain_b, w_kv, w_in_b):
    s, d = x.shape
    a = w_kv.shape[1] // 2
    tm = min(ROW_TILE, s)
    scale = HEAD_DIM ** -0.5

    def body(u_ref, x_ref, conv_ref, woa_ref, gkv_ref, gb_ref, wkv_ref, wib_ref,
             x1_ref, kv_ref, q_ref, gate_ref, halo_ref):
        @pl.when(pl.program_id(0) == 0)
        def _():
            halo_ref[...] = jnp.zeros_like(halo_ref)

        b, c, xin, g = (u_ref[:, k * d:(k + 1) * d] for k in range(4))
        p = c * xin
        p1, p2 = _shift_rows_down(p, halo_ref[...], tm)
        halo_ref[...] = p[tm - 8:tm, :]
        w = conv_ref[...]
        cv = w[0:1, :] * p2 + w[1:2, :] * p1 + w[2:3, :] * p
        ya = (b * cv) * (g * _sigmoid(g))
        x1 = x_ref[...] + _dot(_mx(ya), woa_ref[...])
        x1_ref[...] = x1
        _, xh = _rms_stats(x1)
        kv_ref[...] = _dot(_mx(xh * gkv_ref[...]), wkv_ref[...]).astype(kv_ref.dtype)
        qg = _dot(_mx(xh * gb_ref[...]), wib_ref[...])
        q_ref[...] = (qg[:, :a] * scale).astype(q_ref.dtype)
        gate_ref[...] = qg[:, a:]

    return pl.pallas_call(
        body, name="layer_a_out", grid=(s // tm,),
        in_specs=[_row_spec(tm, 4 * d), _row_spec(tm, d), _full_spec((8, d)), _full_spec((d, d)),
                  _full_spec((1, d)), _full_spec((1, d)), _full_spec((d, 2 * a)), _full_spec((d, 2 * a))],
        out_specs=[_row_spec(tm, d), _row_spec(tm, 2 * a), _row_spec(tm, a), _row_spec(tm, a)],
        out_shape=[jax.ShapeDtypeStruct((s, d), _F32), jax.ShapeDtypeStruct((s, 2 * a), _MXU_DTYPE),
                   jax.ShapeDtypeStruct((s, a), _MXU_DTYPE), jax.ShapeDtypeStruct((s, a), _F32)],
        scratch_shapes=[pltpu.VMEM((8, d), _F32)],
        compiler_params=_params(1),
    )(u, x, conv, w_out_a, gain_kv, gain_b, w_kv, w_in_b)


def _neg_softplus(z):
    e = jnp.exp(-jnp.abs(z))
    return -(jnp.maximum(z, 0.0) + jnp.log(1.0 + e)), e


def _scan_dot(val, tri):
    return _dot(_mx(val), tri)


def _head_masks():
    lane = lax.broadcasted_iota(jnp.int32, (1, LANES), 1)
    return [lane < HEAD_DIM, lane >= HEAD_DIM]


def _stacked_causal(tq):
    row = lax.broadcasted_iota(jnp.int32, (2 * tq, tq), 0)
    col = lax.broadcasted_iota(jnp.int32, (2 * tq, tq), 1)
    return col < jnp.where(row >= tq, row - tq, row)


def _attn_fwd(q, kv, tri_suffix):
    s, a = q.shape
    n_hp = a // LANES
    tq = min(ATT_TILE, s)
    nq = s // tq

    def body(q_ref, k_ref, v_ref, tri_ref, o_ref, c_ref, stop_ref):
        hp, i = pl.program_id(0), pl.program_id(1)
        qv = q_ref[...]
        tri = tri_ref[...]
        causal = _stacked_causal(tq)
        qs = jnp.concatenate([jnp.where(hmask, qv, jnp.zeros_like(qv)) for hmask in _head_masks()], axis=0)

        def block(j, c, acc, diag=False, live=None):
            off = pl.multiple_of(j * tq, tq)
            kb = k_ref[pl.ds(off, tq), :]
            vb = v_ref[pl.ds(off, tq), :]
            z = _dot_nt(qs, kb)
            lg, _ = _neg_softplus(z)
            if diag:
                lg = jnp.where(causal, lg, 0.0)
            tot = jnp.sum(lg, axis=1, keepdims=True)
            c_in = c
            if live is not None:
                c_in = c + (live - 1.0) * 1e30
                tot = tot * live
            w = jnp.exp(z + _scan_dot(lg, tri) + c_in)
            if diag:
                w = jnp.where(causal, w, 0.0)
            return c + tot, acc + _dot(_mx(w), vb)

        c, acc = block(i, jnp.zeros((2 * tq, 1), _F32), jnp.zeros((2 * tq, LANES), _F32), diag=True)
        c, acc = block(jnp.maximum(i - 1, 0), c, acc, live=jnp.where(i >= 1, 1.0, 0.0))

        def cond(carry):
            return jnp.logical_and(carry[0] >= 0, jnp.max(carry[1]) > SKIP_LOG)

        def step(carry):
            c, acc = block(carry[0], carry[1], carry[2])
            return carry[0] - 1, c, acc

        j, c, acc = lax.while_loop(cond, step, (i - 2, c, acc))
        stop_ref[hp, i] = jnp.maximum(jnp.minimum(j + 1, i - 1), 0)
        first = _head_masks()[0]
        o_ref[...] = jnp.where(first, acc[:tq], acc[tq:])
        c_ref[...] = jnp.where(first, c[:tq], c[tq:])

    q_spec = pl.BlockSpec((tq, LANES), lambda h, i: (i, h))
    return pl.pallas_call(
        body, name="attn_fwd", grid=(n_hp, nq),
        in_specs=[q_spec, pl.BlockSpec((s, LANES), lambda h, i: (0, h)),
                  pl.BlockSpec((s, LANES), lambda h, i: (0, n_hp + h)), _full_spec((tq, tq))],
        out_specs=[q_spec, q_spec, pl.BlockSpec(memory_space=pltpu.SMEM)],
        out_shape=[jax.ShapeDtypeStruct((s, a), _F32), jax.ShapeDtypeStruct((s, a), _F32),
                   jax.ShapeDtypeStruct((n_hp, nq), jnp.int32)],
        compiler_params=_params(2),
    )(q, kv, kv, tri_suffix)


def _layer_b_out_loss(o, gate, x1, w_out_b, gain_f, target):
    s, d = x1.shape
    a = o.shape[1]
    tm = min(ROW_TILE, s)

    def body(o_ref, gate_ref, x1_ref, wob_ref, gf_ref, tgt_ref, dx2_ref, gnf_ref, loss_ref):
        @pl.when(pl.program_id(0) == 0)
        def _():
            gnf_ref[...] = jnp.zeros_like(gnf_ref)
            loss_ref[...] = jnp.zeros_like(loss_ref)

        gt = gate_ref[...]
        ob = o_ref[...] * (gt * _sigmoid(gt))
        x2 = x1_ref[...] + _dot(_mx(ob), wob_ref[...])
        r, xh = _rms_stats(x2)
        gf = gf_ref[...]
        err = xh * gf - tgt_ref[...]
        part = jnp.sum(jnp.sum(err * err, axis=1, keepdims=True), axis=0, keepdims=True)
        loss_ref[...] += part * (0.5 / d)
        dy = err * (1.0 / d)
        gnf_ref[...] += jnp.sum(dy * xh, axis=0, keepdims=True)
        dx2_ref[...] = _rms_bwd(r, xh, dy * gf)

    return pl.pallas_call(
        body, name="layer_b_out_loss", grid=(s // tm,),
        in_specs=[_row_spec(tm, a), _row_spec(tm, a), _row_spec(tm, d), _full_spec((a, d)), _full_spec((1, d)),
                  _row_spec(tm, d)],
        out_specs=[_row_spec(tm, d), _full_spec((1, d)), _full_spec((1, LANES))],
        out_shape=[jax.ShapeDtypeStruct((s, d), _F32), jax.ShapeDtypeStruct((1, d), _F32),
                   jax.ShapeDtypeStruct((1, LANES), _F32)],
        compiler_params=_params(1),
    )(o, gate, x1, w_out_b, gain_f, target)


def _silu_and_grad(g):
    sg = _sigmoid(g)
    return g * sg, sg * (1.0 + g * (1.0 - sg))


def _layer_b_bwd(dx2, o, gate, w_out_b):
    s, d = dx2.shape
    a = o.shape[1]
    tm = min(ROW_TILE, s)

    def body(dx2_ref, o_ref, gate_ref, wob_ref, do_ref, dgate_ref, gw_ref):
        @pl.when(pl.program_id(0) == 0)
        def _():
            gw_ref[...] = jnp.zeros_like(gw_ref)

        dxb = _mx(dx2_ref[...])
        sl, dsl = _silu_and_grad(gate_ref[...])
        ov = o_ref[...]
        d_ob = _dot_nt(dxb, wob_ref[...])
        gw_ref[...] += _dot_tn(_mx(ov * sl), dxb)
        do_ref[...] = (d_ob * sl).astype(do_ref.dtype)
        dgate_ref[...] = (d_ob * ov * dsl).astype(dgate_ref.dtype)

    return pl.pallas_call(
        body, name="layer_b_bwd", grid=(s // tm,),
        in_specs=[_row_spec(tm, d), _row_spec(tm, a), _row_spec(tm, a), _full_spec((a, d))],
        out_specs=[_row_spec(tm, a), _row_spec(tm, a), _full_spec((a, d))],
        out_shape=[jax.ShapeDtypeStruct((s, a), _MXU_DTYPE), jax.ShapeDtypeStruct((s, a), _MXU_DTYPE),
                   jax.ShapeDtypeStruct((a, d), _F32)],
        compiler_params=_params(1),
    )(dx2, o, gate, w_out_b)


def _attn_bwd(q, kv, do, c_tot, stop, tri_suffix, tri_prefix):
    s, a = q.shape
    n_hp = a // LANES
    tq = min(ATT_TILE, s)
    nq = s // tq
    scale = HEAD_DIM ** -0.5
    chunk = min(1024, s)

    def body(stop_ref, q_ref, do_ref, c_ref, k_ref, v_ref, ts_ref, tp_ref, dq_ref, dk_hbm, dv_hbm,
             dk_acc, dv_acc, stage, sem):
        hp, i = pl.program_id(0), pl.program_id(1)

        @pl.when(i == 0)
        def _():
            dk_acc[...] = jnp.zeros_like(dk_acc)
            dv_acc[...] = jnp.zeros_like(dv_acc)

        qv, dov, cv = q_ref[...], do_ref[...], c_ref[...]
        ts, tp = ts_ref[...], tp_ref[...]
        causal = _stacked_causal(tq)
        masks = _head_masks()
        qs = jnp.concatenate([jnp.where(hmask, qv, jnp.zeros_like(qv)) for hmask in masks], axis=0)
        dos = jnp.concatenate([jnp.where(hmask, dov, jnp.zeros_like(dov)) for hmask in masks], axis=0)
        cs = jnp.concatenate([cv[:, 0:1], cv[:, HEAD_DIM:HEAD_DIM + 1]], axis=0)

        def block(j, st, diag=False, live=None):
            asc, pre, dq = st
            rows = pl.ds(pl.multiple_of(j * tq, tq), tq)
            kb = k_ref[rows, :]
            vb = v_ref[rows, :]
            z = _dot_nt(qs, kb)
            lg, _ = _neg_softplus(z)
            sig = jnp.exp(z + lg)
            if diag:
                lg = jnp.where(causal, lg, 0.0)
            tot = jnp.sum(lg, axis=1, keepdims=True)
            newer = jnp.zeros_like(tot) if diag else cs - asc - tot
            if live is not None:
                newer = newer + (live - 1.0) * 1e30
                tot = tot * live
            wgt = jnp.exp(z + _scan_dot(lg, ts) + newer)
            if diag:
                wgt = jnp.where(causal, wgt, 0.0)
            g = wgt * _dot_nt(dos, vb)
            dz = g - sig * (_scan_dot(g, tp) + pre)
            if diag:
                dz = jnp.where(causal, dz, 0.0)
            dzb = _mx(dz)
            dk_acc[rows, :] += _dot_tn(dzb, qs)
            dv_acc[rows, :] += _dot_tn(_mx(wgt), dos)
            return asc + tot, pre + jnp.sum(g, axis=1, keepdims=True), dq + _dot(dzb, kb)

        first = jnp.clip(stop_ref[hp, i], 0, jnp.maximum(i - 1, 0))
        st = (jnp.zeros((2 * tq, 1), _F32), jnp.zeros((2 * tq, 1), _F32), jnp.zeros((2 * tq, LANES), _F32))
        st = lax.fori_loop(first, i - 1, block, st)
        st = block(jnp.maximum(i - 1, 0), st, live=jnp.where(i >= 1, 1.0, 0.0))
        dq = block(i, st, diag=True)[2]
        dq_ref[...] = (jnp.where(masks[0], dq[:tq], dq[tq:]) * scale).astype(dq_ref.dtype)

        @pl.when(i == nq - 1)
        def _():
            cols = pl.ds(pl.multiple_of(hp * LANES, LANES), LANES)
            for acc, out in ((dk_acc, dk_hbm), (dv_acc, dv_hbm)):
                def cast(n, carry, acc=acc):
                    rows = pl.ds(pl.multiple_of(n * chunk, chunk), chunk)
                    stage[rows, :] = acc[rows, :].astype(stage.dtype)
                    return carry
                lax.fori_loop(0, s // chunk, cast, 0)
                cp = pltpu.make_async_copy(stage, out.at[:, cols], sem)
                cp.start()
                cp.wait()

    q_spec = pl.BlockSpec((tq, LANES), lambda h, i, *_: (i, h))
    grid_spec = pltpu.PrefetchScalarGridSpec(
        num_scalar_prefetch=1, grid=(n_hp, nq),
        in_specs=[q_spec, q_spec, q_spec, pl.BlockSpec((s, LANES), lambda h, i, *_: (0, h)),
                  pl.BlockSpec((s, LANES), lambda h, i, *_: (0, n_hp + h)),
                  pl.BlockSpec((tq, tq), lambda h, i, *_: (0, 0)), pl.BlockSpec((tq, tq), lambda h, i, *_: (0, 0))],
        out_specs=[q_spec, pl.BlockSpec(memory_space=pl.ANY), pl.BlockSpec(memory_space=pl.ANY)],
        scratch_shapes=[pltpu.VMEM((s, LANES), _F32), pltpu.VMEM((s, LANES), _F32), pltpu.VMEM((s, LANES), _MXU_DTYPE),
                        pltpu.SemaphoreType.DMA],
    )
    return pl.pallas_call(
        body, name="attn_bwd", grid_spec=grid_spec,
        out_shape=[jax.ShapeDtypeStruct((s, a), _MXU_DTYPE)] * 3,
        compiler_params=_params(2),
    )(stop, q, do, c_tot, kv, kv, tri_suffix, tri_prefix)


def _proj_bwd(x1, dx2, dq, dgate, dk, dv, w_in_b, w_kv, gain_b, gain_kv):
    s, d = x1.shape
    a = dq.shape[1]
    tm = min(ROW_TILE, s)

    def body(x1_ref, dx2_ref, dq_ref, dgate_ref, dk_ref, dv_ref, wib_ref, wkv_ref, gb_ref, gkv_ref,
             dx1_ref, gwib_ref, gwkv_ref, gnb_ref, gnkv_ref):
        @pl.when(pl.program_id(0) == 0)
        def _():
            for ref in (gwib_ref, gwkv_ref, gnb_ref, gnkv_ref):
                ref[...] = jnp.zeros_like(ref)

        r, xh = _rms_stats(x1_ref[...])
        gb, gkv = gb_ref[...], gkv_ref[...]
        hb, hk = _mx(xh * gb), _mx(xh * gkv)
        dq, dgate, dk, dv = dq_ref[...], dgate_ref[...], dk_ref[...], dv_ref[...]
        gwib_ref[:, :a] += _dot_tn(hb, dq)
        gwib_ref[:, a:] += _dot_tn(hb, dgate)
        gwkv_ref[:, :a] += _dot_tn(hk, dk)
        gwkv_ref[:, a:] += _dot_tn(hk, dv)
        d_hb = _dot_nt(dq, wib_ref[:, :a]) + _dot_nt(dgate, wib_ref[:, a:])
        d_hk = _dot_nt(dk, wkv_ref[:, :a]) + _dot_nt(dv, wkv_ref[:, a:])
        gnb_ref[...] += jnp.sum(d_hb * xh, axis=0, keepdims=True)
        gnkv_ref[...] += jnp.sum(d_hk * xh, axis=0, keepdims=True)
        dx1_ref[...] = dx2_ref[...] + _rms_bwd(r, xh, d_hb * gb + d_hk * gkv)

    return pl.pallas_call(
        body, name="proj_bwd", grid=(s // tm,),
        in_specs=[_row_spec(tm, d), _row_spec(tm, d)] + [_row_spec(tm, a)] * 4
        + [_full_spec((d, 2 * a)), _full_spec((d, 2 * a)), _full_spec((1, d)), _full_spec((1, d))],
        out_specs=[_row_spec(tm, d), _full_spec((d, 2 * a)), _full_spec((d, 2 * a)), _full_spec((1, d)),
                   _full_spec((1, d))],
        out_shape=[jax.ShapeDtypeStruct((s, d), _F32), jax.ShapeDtypeStruct((d, 2 * a), _F32),
                   jax.ShapeDtypeStruct((d, 2 * a), _F32), jax.ShapeDtypeStruct((1, d), _F32),
                   jax.ShapeDtypeStruct((1, d), _F32)],
        compiler_params=_params(1),
    )(x1, dx2, dq, dgate, dk, dv, w_in_b, w_kv, gain_b, gain_kv)


def _layer_a_bwd(u, dx1, conv, w_out_a):
    s, d = dx1.shape
    tm = min(ROW_TILE, s)
    n = s // tm
    per8 = tm // 8

    def body(u_ref, uprev_ref, dx1_ref, conv_ref, woa_ref, du_ref, gwoa_ref, gconv_ref, halo_ref):
        step = pl.program_id(0)

        @pl.when(step == 0)
        def _():
            for ref in (gwoa_ref, gconv_ref, halo_ref):
                ref[...] = jnp.zeros_like(ref)

        b, c, xin, g = (u_ref[:, k * d:(k + 1) * d] for k in range(4))
        p = c * xin
        before = uprev_ref[:, d:2 * d] * uprev_ref[:, 2 * d:3 * d]
        before = jnp.where(step == n - 1, jnp.zeros_like(before), before)
        p1, p2 = _shift_rows_down(p, before, tm)
        w = conv_ref[...]
        cv = w[0:1, :] * p2 + w[1:2, :] * p1 + w[2:3, :] * p
        sl, dsl = _silu_and_grad(g)
        y = b * cv
        dxb = _mx(dx1_ref[...])
        gwoa_ref[...] += _dot_tn(_mx(y * sl), dxb)
        d_ya = _dot_nt(dxb, woa_ref[...])
        d_y = d_ya * sl
        d_cv = d_y * b
        n1, n2 = _shift_rows_up(d_cv, halo_ref[...], tm)
        halo_ref[...] = d_cv[0:8, :]
        d_p = w[2:3, :] * d_cv + w[1:2, :] * n1 + w[0:1, :] * n2
        gconv_ref[0:1, :] += jnp.sum(d_cv * p2, axis=0, keepdims=True)
        gconv_ref[1:2, :] += jnp.sum(d_cv * p1, axis=0, keepdims=True)
        gconv_ref[2:3, :] += jnp.sum(d_cv * p, axis=0, keepdims=True)
        du_ref[:, 0:d] = (d_y * cv).astype(du_ref.dtype)
        du_ref[:, d:2 * d] = (d_p * xin).astype(du_ref.dtype)
        du_ref[:, 2 * d:3 * d] = (d_p * c).astype(du_ref.dtype)
        du_ref[:, 3 * d:4 * d] = (d_ya * y * dsl).astype(du_ref.dtype)

    def rev(i):
        return (n - 1 - i, 0)

    return pl.pallas_call(
        body, name="layer_a_bwd", grid=(n,),
        in_specs=[pl.BlockSpec((tm, 4 * d), rev),
                  pl.BlockSpec((8, 4 * d), lambda i: (jnp.maximum((n - 1 - i) * per8 - 1, 0), 0)),
                  pl.BlockSpec((tm, d), rev), _full_spec((8, d)), _full_spec((d, d))],
        out_specs=[pl.BlockSpec((tm, 4 * d), rev), _full_spec((d, d)), _full_spec((8, d))],
        out_shape=[jax.ShapeDtypeStruct((s, 4 * d), _MXU_DTYPE), jax.ShapeDtypeStruct((d, d), _F32),
                   jax.ShapeDtypeStruct((8, d), _F32)],
        scratch_shapes=[pltpu.VMEM((8, d), _F32)],
        compiler_params=_params(1),
    )(u, u, dx1, conv, w_out_a)


def _grad_w_in_a(h, du, nb, grads):
    s, d = h.shape
    bn = du.shape[1] // nb
    tm = min(2 * ROW_TILE, s)
    half = nb // 2
    n = len(grads)
    steps = s // tm

    def body(h_ref, du_ref, *refs):
        gw_ref = refs[n]
        start, wait = _exchange_ops(refs[:n], refs[n + 1:2 * n + 1], *refs[2 * n + 1:], True)
        jh, i = pl.program_id(0), pl.program_id(1)
        pl.when(jnp.logical_and(jh == 0, i == 0))(start)

        @pl.when(i == 0)
        def _():
            gw_ref[...] = jnp.zeros_like(gw_ref)

        hv = h_ref[...]
        for j in range(half):
            gw_ref[j] += _dot_tn(hv, du_ref[:, j * bn:(j + 1) * bn])
        pl.when(jnp.logical_and(jh == 1, i == steps - 1))(wait)

    outs = pl.pallas_call(
        body, name="grad_w_in_a", grid=(2, steps),
        in_specs=[pl.BlockSpec((tm, d), lambda jh, i: (i, 0)), pl.BlockSpec((tm, half * bn), lambda jh, i: (i, jh))]
        + [_ANY] * n,
        out_specs=[pl.BlockSpec((half, d, bn), lambda jh, i: (jh, 0, 0))] + [_ANY] * n,
        out_shape=[jax.ShapeDtypeStruct((nb, d, bn), _F32)] + _exchange_shapes(grads, True),
        scratch_shapes=_exchange_sems(n),
        compiler_params=_params(2),
    )(h, du, *grads)
    return outs[0], outs[1:]


def _input_grad(x, dx1, du, w_blk, gain):
    s, d = x.shape
    nb, _, bn = w_blk.shape
    tm = min(ROW_TILE, s)

    def body(x_ref, dx1_ref, du_ref, w_ref, g_ref, dx_ref, gn_ref):
        @pl.when(pl.program_id(0) == 0)
        def _():
            gn_ref[...] = jnp.zeros_like(gn_ref)

        r, xh = _rms_stats(x_ref[...])
        d_h = _dot_nt(du_ref[:, 0:bn], w_ref[0])
        for j in range(1, nb):
            d_h = d_h + _dot_nt(du_ref[:, j * bn:(j + 1) * bn], w_ref[j])
        gn_ref[...] += jnp.sum(d_h * xh, axis=0, keepdims=True)
        dx_ref[...] = dx1_ref[...] + _rms_bwd(r, xh, d_h * g_ref[...])

    return pl.pallas_call(
        body, name="input_grad", grid=(s // tm,),
        in_specs=[_row_spec(tm, d), _row_spec(tm, d), _row_spec(tm, nb * bn), _full_spec((nb, d, bn)),
                  _full_spec((1, d))],
        out_specs=[_row_spec(tm, d), _full_spec((1, d))],
        out_shape=[jax.ShapeDtypeStruct((s, d), _F32), jax.ShapeDtypeStruct((1, d), _F32)],
        compiler_params=_params(1),
    )(x, dx1, du, w_blk, gain)


def _reduce_adamw(name, parts, w, m, v):
    rows, cols = w.shape
    tr = min(256, rows)
    c1 = 1.0 - ADAM_B1 ** ADAM_STEP
    c2 = 1.0 - ADAM_B2 ** ADAM_STEP

    def body(p_ref, w_ref, m_ref, v_ref, g_ref, d_ref, nm_ref, nv_ref):
        g = p_ref[0].astype(_F32)
        for k in range(1, N_DEV):
            g = g + p_ref[k].astype(_F32)
        nm = ADAM_B1 * m_ref[...] + (1.0 - ADAM_B1) * g
        nv = ADAM_B2 * v_ref[...] + (1.0 - ADAM_B2) * (g * g)
        g_ref[...] = g
        nm_ref[...] = nm
        nv_ref[...] = nv
        d_ref[...] = -ADAM_LR * ((nm / c1) / (jnp.sqrt(nv / c2) + ADAM_EPS) + ADAM_WD * w_ref[...])

    tile = _row_spec(tr, cols)
    return pl.pallas_call(
        body, name=name, grid=(rows // tr,),
        in_specs=[pl.BlockSpec((N_DEV, tr, cols), lambda i: (0, i, 0)), tile, tile, tile],
        out_specs=[tile] * 4,
        out_shape=[jax.ShapeDtypeStruct((rows, cols), _F32)] * 4,
        compiler_params=_params(1),
    )(parts, w, m, v)


def _pad_rows(a, rows=8):
    return jnp.pad(a, ((0, rows - a.shape[0]), (0, 0)))


def kernel(x, norm_a, w_in_a, conv_a, w_out_a, norm_kv, w_kv, norm_b, w_in_b, w_out_b, norm_f, loss_target, m_norm_a, m_w_in_a, m_conv_a, m_w_out_a, m_norm_kv, m_w_kv, m_norm_b, m_w_in_b, m_w_out_b, m_norm_f, v_norm_a, v_w_in_a, v_conv_a, v_w_out_a, v_norm_kv, v_w_kv, v_norm_b, v_w_in_b, v_w_out_b, v_norm_f):
    x0 = x[0]
    s, d = x0.shape
    a = d // 2
    sh = d // N_DEV
    me = 4 * lax.axis_index("x") + 2 * lax.axis_index("y") + lax.axis_index("c")

    small_a = _pad_rows(jnp.concatenate([norm_a, conv_a[0]], axis=0))
    wia_g, small_g = _exchange("exchange_gather", [_mx(w_in_a[0]), small_a], scatter=False)
    small_f = small_g.transpose(1, 0, 2).reshape(8, d)
    gain_a = small_f[0:1]
    conv_f = _pad_rows(small_f[1:4])
    gain_kv, gain_b, gain_f = norm_kv.reshape(1, d), norm_b.reshape(1, d), norm_f.reshape(1, d)

    u, h, (woa_g, wkv_g, wib_g, wob_g) = _ln_matmul_in_a(
        x0, gain_a, wia_g, [_mx(w_out_a[0]), _mx(w_kv), _mx(w_in_b[0]), _mx(w_out_b[0])])
    woa_f = woa_g.reshape(d, d)
    wkv_f = wkv_g.reshape(d, 2 * a)
    wib_f = wib_g.reshape(d, 2 * a)
    wob_f = wob_g.transpose(1, 0, 2).reshape(a, d)
    x1, kv, q, gate = _layer_a_out(u, x0, conv_f, woa_f, gain_kv, gain_b, wkv_f, wib_f)
    tq = min(ATT_TILE, s)
    idx = jnp.arange(tq)
    tri_suffix = _mx(idx[:, None] >= idx[None, :])
    tri_prefix = _mx(idx[:, None] <= idx[None, :])
    o, c_tot, stop = _attn_fwd(q, kv, tri_suffix)
    dx2, g_norm_f, loss_part = _layer_b_out_loss(o, gate, x1, wob_f, gain_f, loss_target[0])

    do, dgate, g_wob = _layer_b_bwd(dx2, o, gate, wob_f)
    dq, dk, dv = _attn_bwd(q, kv, do, c_tot, stop, tri_suffix, tri_prefix)
    dx1, g_wib, g_wkv, g_norm_b, g_norm_kv = _proj_bwd(x1, dx2, dq, dgate, dk, dv, wib_f, wkv_f, gain_b, gain_kv)
    du, g_woa, g_conv = _layer_a_bwd(u, dx1, conv_f, woa_f)
    dx0, g_norm_a = _input_grad(x0, dx1, du, wia_g, gain_a)

    g_wia, (p_woa, p_wkv, p_wib, p_wob) = _grad_w_in_a(
        h, du, N_DEV,
        [_mx(g_woa).reshape(N_DEV, sh, d), _mx(g_wkv).reshape(N_DEV, sh, 2 * a), _mx(g_wib).reshape(N_DEV, sh, 2 * a),
         _mx(g_wob).reshape(a, N_DEV, sh).transpose(1, 0, 2)])
    (p_wia,) = _exchange("exchange_scatter", [_mx(g_wia)], scatter=True)
    small_grads = jnp.concatenate(
        [g_norm_a, g_conv[0:3], g_norm_kv, g_norm_b, g_norm_f, jnp.pad(loss_part, ((0, 0), (0, d - LANES)))], axis=0)
    (p_small,) = _exchange("exchange_small", [small_grads], scatter=False)

    upd_wia = _reduce_adamw("adamw_w_in_a", p_wia, w_in_a[0], m_w_in_a[0], v_w_in_a[0])
    upd_woa = _reduce_adamw("adamw_w_out_a", p_woa, w_out_a[0], m_w_out_a[0], v_w_out_a[0])
    upd_wkv = _reduce_adamw("adamw_w_kv", p_wkv, w_kv, m_w_kv, v_w_kv)
    upd_wib = _reduce_adamw("adamw_w_in_b", p_wib, w_in_b[0], m_w_in_b[0], v_w_in_b[0])
    upd_wob = _reduce_adamw("adamw_w_out_b", p_wob, w_out_b[0], m_w_out_b[0], v_w_out_b[0])

    def rep(a1, a2, a3):
        return jnp.concatenate([jnp.zeros((4, d), _F32), a1.reshape(1, d), a2.reshape(1, d), a3.reshape(1, d),
                                jnp.zeros((1, d), _F32)], axis=0)

    upd_rep = _reduce_adamw("adamw_replicated", p_small, rep(norm_kv, norm_b, norm_f),
                            rep(m_norm_kv, m_norm_b, m_norm_f), rep(v_norm_kv, v_norm_b, v_norm_f))

    def mine(n1, cv1):
        return _pad_rows(jnp.concatenate([n1, cv1[0]], axis=0))

    p_mine = lax.dynamic_slice(p_small, (0, 0, me * sh), (N_DEV, 8, sh))
    upd_mine = _reduce_adamw("adamw_sharded_small", p_mine, mine(norm_a, conv_a), mine(m_norm_a, m_conv_a),
                             mine(v_norm_a, v_conv_a))

    loss = upd_rep[0][7, 0]
    groups = []
    for k in range(4):
        groups.append([
            upd_mine[k][0:1], upd_wia[k][None], upd_mine[k][1:4][None], upd_woa[k][None], upd_rep[k][4],
            upd_wkv[k], upd_rep[k][5:6], upd_wib[k][None], upd_wob[k][None], upd_rep[k][6]])
    return (loss, dx0[None], *groups[0], *groups[1], *groups[2], *groups[3])
```

```python
import jax
import jax.numpy as jnp
from jax import lax
from jax.experimental import pallas as pl
from jax.experimental.pallas import tpu as pltpu

_MXU_DTYPE = jnp.bfloat16
_F32 = jnp.float32

RMS_EPS = 1e-6
HEAD_DIM = 64
LANES = 128
N_DEV = 8
ATT_TILE = 256
ROW_TILE = 256
BIG_ROW_TILE = 512
VMEM_LIMIT = 56 * 1024 * 1024
SKIP_LOG = -110.0

ADAM_LR = 0.001
ADAM_B1 = 0.9
ADAM_B2 = 0.999
ADAM_EPS = 1e-08
ADAM_WD = 0.01
ADAM_STEP = 10

_NT = (((1,), (1,)), ((), ()))
_TN = (((0,), (0,)), ((), ()))


def _params(n_grid):
    return pltpu.CompilerParams(dimension_semantics=("arbitrary",) * n_grid, vmem_limit_bytes=VMEM_LIMIT)


def _dot(a, b):
    return jnp.dot(a, b, preferred_element_type=_F32)


def _dot_nt(a, b):
    return lax.dot_general(a, b, _NT, preferred_element_type=_F32)


def _dot_tn(a, b):
    return lax.dot_general(a, b, _TN, preferred_element_type=_F32)


def _mx(a):
    return a.astype(_MXU_DTYPE)


def _sigmoid(a):
    return 1.0 / (1.0 + jnp.exp(-a))


def _rms_stats(xv):
    r = lax.rsqrt(jnp.mean(xv * xv, axis=-1, keepdims=True) + RMS_EPS)
    return r, xv * r


def _rms_bwd(r, xh, dyg):
    return r * (dyg - xh * jnp.mean(dyg * xh, axis=-1, keepdims=True))


def _row_spec(tm, width):
    return pl.BlockSpec((tm, width), lambda i: (i, 0))


def _full_spec(shape):
    zeros = (0,) * len(shape)
    return pl.BlockSpec(shape, lambda *_: zeros)


def _zero_at_first(step, *refs):
    @pl.when(step == 0)
    def _():
        for ref in refs:
            ref[...] = jnp.zeros_like(ref)


def _emit_at_last(step, last, pairs):
    @pl.when(step == last)
    def _():
        for acc, out in pairs:
            out[...] = acc[...].astype(out.dtype)


def _exchange_ops(ins, outs, send_sems, recv_sems, local_sems, scatter):
    n = len(ins)
    x, y, c = lax.axis_index("x"), lax.axis_index("y"), lax.axis_index("c")
    me = 4 * x + 2 * y + c

    def remote(t, m, landed):
        px = 1 - x if m & 4 else x
        py = 1 - y if m & 2 else y
        pc = 1 - c if m & 1 else c
        idx = 4 * px + 2 * py + pc
        k = t * (N_DEV - 1) + m - 1
        return pltpu.make_async_remote_copy(
            src_ref=ins[t].at[idx] if scatter else ins[t], dst_ref=outs[t].at[idx if landed else me],
            send_sem=send_sems.at[k], recv_sem=recv_sems.at[k],
            device_id=(px, py, pc), device_id_type=pl.DeviceIdType.MESH)

    def local(t):
        return pltpu.make_async_copy(ins[t].at[me] if scatter else ins[t], outs[t].at[me], local_sems.at[t])

    def start():
        for t in range(n):
            local(t).start()
        for m in range(1, N_DEV):
            for t in range(n):
                remote(t, m, False).start()

    def wait():
        for m in range(1, N_DEV):
            for t in range(n):
                remote(t, m, True).wait_recv()
        for m in range(1, N_DEV):
            for t in range(n):
                remote(t, m, False).wait_send()
        for t in range(n):
            local(t).wait()

    return start, wait


def _exchange_shapes(arrays, scatter):
    return [jax.ShapeDtypeStruct((N_DEV,) + (a.shape[1:] if scatter else a.shape), a.dtype) for a in arrays]


def _exchange_sems(n):
    return [pltpu.SemaphoreType.DMA((n * (N_DEV - 1),)), pltpu.SemaphoreType.DMA((n * (N_DEV - 1),)),
            pltpu.SemaphoreType.DMA((n,))]


_ANY = pl.BlockSpec(memory_space=pl.ANY)


def _exchange(name, arrays, scatter):
    n = len(arrays)

    def body(*refs):
        start, wait = _exchange_ops(refs[:n], refs[n:2 * n], *refs[2 * n:], scatter)
        start()
        wait()

    return pl.pallas_call(
        body, name=name, in_specs=[_ANY] * n, out_specs=[_ANY] * n,
        out_shape=_exchange_shapes(arrays, scatter), scratch_shapes=_exchange_sems(n),
    )(*arrays)


def _ln_matmul_in_a(x, gain, w_in, shards):
    s, d = x.shape
    f = w_in.shape[1]
    tm = min(BIG_ROW_TILE, s)
    n = len(shards)
    last = s // tm - 1

    def body(x_ref, g_ref, w_ref, *refs):
        u_ref, h_ref = refs[n:n + 2]
        start, wait = _exchange_ops(refs[:n], refs[n + 2:2 * n + 2], *refs[2 * n + 2:], False)
        pl.when(pl.program_id(0) == 0)(start)
        _, xh = _rms_stats(x_ref[...])
        h = _mx(xh * g_ref[...])
        h_ref[...] = h
        u_ref[...] = _dot(h, w_ref[...])
        pl.when(pl.program_id(0) == last)(wait)

    outs = pl.pallas_call(
        body, name="ln_matmul_in_a", grid=(s // tm,),
        in_specs=[_row_spec(tm, d), _full_spec((1, d)), _full_spec((d, f))] + [_ANY] * n,
        out_specs=[_row_spec(tm, f), _row_spec(tm, d)] + [_ANY] * n,
        out_shape=[jax.ShapeDtypeStruct((s, f), _F32), jax.ShapeDtypeStruct((s, d), _MXU_DTYPE)]
        + _exchange_shapes(shards, False),
        scratch_shapes=_exchange_sems(n),
        compiler_params=_params(1),
    )(x, gain, w_in, *shards)
    return outs[0], outs[1], outs[2:]


def _shift_rows_down(p, before, tm):
    row = lax.broadcasted_iota(jnp.int32, (8, p.shape[1]), 0)
    r1, r2 = pltpu.roll(p, 1, 0), pltpu.roll(p, 2, 0)
    top1 = jnp.where(row == 0, before[7:8, :], r1[0:8, :])
    top2 = jnp.where(row == 0, before[6:7, :], jnp.where(row == 1, before[7:8, :], r2[0:8, :]))
    return jnp.concatenate([top1, r1[8:, :]], axis=0), jnp.concatenate([top2, r2[8:, :]], axis=0)


def _shift_rows_up(p, after, tm):
    row = lax.broadcasted_iota(jnp.int32, (8, p.shape[1]), 0)
    r1, r2 = pltpu.roll(p, tm - 1, 0), pltpu.roll(p, tm - 2, 0)
    end1 = jnp.where(row == 7, after[0:1, :], r1[tm - 8:, :])
    end2 = jnp.where(row == 6, after[0:1, :], jnp.where(row == 7, after[1:2, :], r2[tm - 8:, :]))
    return jnp.concatenate([r1[:tm - 8, :], end1], axis=0), jnp.concatenate([r2[:tm - 8, :], end2], axis=0)


def _layer_a_out(u, x, conv, w_out_a, gain_kv, gain_b, w_kv, w_in_b):
    s, d = x.shape
    a = w_kv.shape[1] // 2
    tm = min(ROW_TILE, s)
    scale = HEAD_DIM ** -0.5

    def body(u_ref, x_ref, conv_ref, woa_ref, gkv_ref, gb_ref, wkv_ref, wib_ref,
             x1_ref, kv_ref, q_ref, gate_ref, halo_ref):
        _zero_at_first(pl.program_id(0), halo_ref)
        b, c, xin, g = (u_ref[:, k * d:(k + 1) * d] for k in range(4))
        p = c * xin
        p1, p2 = _shift_rows_down(p, halo_ref[...], tm)
        halo_ref[...] = p[tm - 8:tm, :]
        w = conv_ref[...]
        cv = w[0:1, :] * p2 + w[1:2, :] * p1 + w[2:3, :] * p
        ya = (b * cv) * (g * _sigmoid(g))
        x1 = x_ref[...] + _dot(_mx(ya), woa_ref[...])
        x1_ref[...] = x1
        _, xh = _rms_stats(x1)
        kv_ref[...] = _dot(_mx(xh * gkv_ref[...]), wkv_ref[...]).astype(kv_ref.dtype)
        qg = _dot(_mx(xh * gb_ref[...]), wib_ref[...])
        q_ref[...] = (qg[:, :a] * scale).astype(q_ref.dtype)
        gate_ref[...] = qg[:, a:]

    return pl.pallas_call(
        body, name="layer_a_out", grid=(s // tm,),
        in_specs=[_row_spec(tm, 4 * d), _row_spec(tm, d), _full_spec((8, d)), _full_spec((d, d)),
                  _full_spec((1, d)), _full_spec((1, d)), _full_spec((d, 2 * a)), _full_spec((d, 2 * a))],
        out_specs=[_row_spec(tm, d), _row_spec(tm, 2 * a), _row_spec(tm, a), _row_spec(tm, a)],
        out_shape=[jax.ShapeDtypeStruct((s, d), _F32), jax.ShapeDtypeStruct((s, 2 * a), _MXU_DTYPE),
                   jax.ShapeDtypeStruct((s, a), _MXU_DTYPE), jax.ShapeDtypeStruct((s, a), _F32)],
        scratch_shapes=[pltpu.VMEM((8, d), _F32)],
        compiler_params=_params(1),
    )(u, x, conv, w_out_a, gain_kv, gain_b, w_kv, w_in_b)


def _neg_softplus(z):
    return -(jnp.maximum(z, 0.0) + jnp.log(1.0 + jnp.exp(-jnp.abs(z))))


def _scan_dot(val, tri):
    return _dot(_mx(val), tri)


def _head_masks():
    lane = lax.broadcasted_iota(jnp.int32, (1, LANES), 1)
    return [lane < HEAD_DIM, lane >= HEAD_DIM]


def _stacked_causal(tq):
    row = lax.broadcasted_iota(jnp.int32, (2 * tq, tq), 0)
    col = lax.broadcasted_iota(jnp.int32, (2 * tq, tq), 1)
    return col < jnp.where(row >= tq, row - tq, row)


def _attn_fwd(q, kv, tri_suffix):
    s, a = q.shape
    n_hp = a // LANES
    tq = min(ATT_TILE, s)
    nq = s // tq

    def body(q_ref, k_ref, v_ref, tri_ref, o_ref, c_ref, stop_ref):
        hp, i = pl.program_id(0), pl.program_id(1)
        qv = q_ref[...]
        tri = tri_ref[...]
        causal = _stacked_causal(tq)
        qs = jnp.concatenate([jnp.where(hmask, qv, jnp.zeros_like(qv)) for hmask in _head_masks()], axis=0)

        def block(j, c, acc, diag=False, live=None):
            off = pl.multiple_of(j * tq, tq)
            kb = k_ref[pl.ds(off, tq), :]
            vb = v_ref[pl.ds(off, tq), :]
            z = _dot_nt(qs, kb)
            lg = _neg_softplus(z)
            if diag:
                lg = jnp.where(causal, lg, 0.0)
            tot = jnp.sum(lg, axis=1, keepdims=True)
            c_in = c
            if live is not None:
                c_in = c + (live - 1.0) * 1e30
                tot = tot * live
            w = jnp.exp(z + _scan_dot(lg, tri) + c_in)
            if diag:
                w = jnp.where(causal, w, 0.0)
            return c + tot, acc + _dot(_mx(w), vb)

        c, acc = block(i, jnp.zeros((2 * tq, 1), _F32), jnp.zeros((2 * tq, LANES), _F32), diag=True)
        c, acc = block(jnp.maximum(i - 1, 0), c, acc, live=jnp.where(i >= 1, 1.0, 0.0))

        def cond(carry):
            return jnp.logical_and(carry[0] >= 0, jnp.max(carry[1]) > SKIP_LOG)

        def step(carry):
            c, acc = block(carry[0], carry[1], carry[2])
            return carry[0] - 1, c, acc

        j, c, acc = lax.while_loop(cond, step, (i - 2, c, acc))
        stop_ref[hp, i] = jnp.maximum(jnp.minimum(j + 1, i - 1), 0)
        first = _head_masks()[0]
        o_ref[...] = jnp.where(first, acc[:tq], acc[tq:])
        c_ref[...] = jnp.where(first, c[:tq], c[tq:])

    q_spec = pl.BlockSpec((tq, LANES), lambda h, i: (i, h))
    return pl.pallas_call(
        body, name="attn_fwd", grid=(n_hp, nq),
        in_specs=[q_spec, pl.BlockSpec((s, LANES), lambda h, i: (0, h)),
                  pl.BlockSpec((s, LANES), lambda h, i: (0, n_hp + h)), _full_spec((tq, tq))],
        out_specs=[q_spec, q_spec, pl.BlockSpec(memory_space=pltpu.SMEM)],
        out_shape=[jax.ShapeDtypeStruct((s, a), _F32), jax.ShapeDtypeStruct((s, a), _F32),
                   jax.ShapeDtypeStruct((n_hp, nq), jnp.int32)],
        compiler_params=_params(2),
    )(q, kv, kv, tri_suffix)


def _silu_and_grad(g):
    sg = _sigmoid(g)
    return g * sg, sg * (1.0 + g * (1.0 - sg))


def _layer_b_out_loss_bwd(o, gate, x1, w_out_b, gain_f, target):
    s, d = x1.shape
    a = o.shape[1]
    tm = min(BIG_ROW_TILE, s)
    last = s // tm - 1

    def body(o_ref, gate_ref, x1_ref, wob_ref, gf_ref, tgt_ref,
             dx2_ref, do_ref, dgate_ref, gw_ref, gnf_ref, loss_ref, gw_acc):
        step = pl.program_id(0)
        _zero_at_first(step, gw_acc, gnf_ref, loss_ref)
        sl, dsl = _silu_and_grad(gate_ref[...])
        ov = o_ref[...]
        ob = _mx(ov * sl)
        wob = wob_ref[...]
        x2 = x1_ref[...] + _dot(ob, wob)
        r, xh = _rms_stats(x2)
        gf = gf_ref[...]
        err = xh * gf - tgt_ref[...]
        part = jnp.sum(jnp.sum(err * err, axis=1, keepdims=True), axis=0, keepdims=True)
        loss_ref[...] += part * (0.5 / d)
        dy = err * (1.0 / d)
        gnf_ref[...] += jnp.sum(dy * xh, axis=0, keepdims=True)
        dx2 = _rms_bwd(r, xh, dy * gf)
        dx2_ref[...] = dx2
        dxb = _mx(dx2)
        d_ob = _dot_nt(dxb, wob)
        gw_acc[...] += _dot_tn(ob, dxb)
        do_ref[...] = (d_ob * sl).astype(do_ref.dtype)
        dgate_ref[...] = (d_ob * ov * dsl).astype(dgate_ref.dtype)
        _emit_at_last(step, last, [(gw_acc, gw_ref)])

    return pl.pallas_call(
        body, name="layer_b_out_loss_bwd", grid=(s // tm,),
        in_specs=[_row_spec(tm, a), _row_spec(tm, a), _row_spec(tm, d), _full_spec((a, d)), _full_spec((1, d)),
                  _row_spec(tm, d)],
        out_specs=[_row_spec(tm, d), _row_spec(tm, a), _row_spec(tm, a), _full_spec((a, d)), _full_spec((1, d)),
                   _full_spec((1, LANES))],
        out_shape=[jax.ShapeDtypeStruct((s, d), _F32), jax.ShapeDtypeStruct((s, a), _MXU_DTYPE),
                   jax.ShapeDtypeStruct((s, a), _MXU_DTYPE), jax.ShapeDtypeStruct((a, d), _MXU_DTYPE),
                   jax.ShapeDtypeStruct((1, d), _F32), jax.ShapeDtypeStruct((1, LANES), _F32)],
        scratch_shapes=[pltpu.VMEM((a, d), _F32)],
        compiler_params=_params(1),
    )(o, gate, x1, w_out_b, gain_f, target)


def _attn_bwd(q, kv, do, c_tot, stop, tri_suffix, tri_prefix):
    s, a = q.shape
    n_hp = a // LANES
    tq = min(ATT_TILE, s)
    nq = s // tq
    scale = HEAD_DIM ** -0.5
    chunk = min(1024, s)

    def body(stop_ref, q_ref, do_ref, c_ref, k_ref, v_ref, ts_ref, tp_ref, dq_ref, dk_hbm, dv_hbm,
             dk_acc, dv_acc, stage, sem):
        hp, i = pl.program_id(0), pl.program_id(1)
        _zero_at_first(i, dk_acc, dv_acc)
        qv, dov, cv = q_ref[...], do_ref[...], c_ref[...]
        ts, tp = ts_ref[...], tp_ref[...]
        causal = _stacked_causal(tq)
        masks = _head_masks()
        qs = jnp.concatenate([jnp.where(hmask, qv, jnp.zeros_like(qv)) for hmask in masks], axis=0)
        dos = jnp.concatenate([jnp.where(hmask, dov, jnp.zeros_like(dov)) for hmask in masks], axis=0)
        cs = jnp.concatenate([cv[:, 0:1], cv[:, HEAD_DIM:HEAD_DIM + 1]], axis=0)

        def block(j, st, diag=False, live=None):
            asc, pre, dq = st
            rows = pl.ds(pl.multiple_of(j * tq, tq), tq)
            kb = k_ref[rows, :]
            vb = v_ref[rows, :]
            z = _dot_nt(qs, kb)
            lg = _neg_softplus(z)
            sig = jnp.exp(z + lg)
            if diag:
                lg = jnp.where(causal, lg, 0.0)
            tot = jnp.sum(lg, axis=1, keepdims=True)
            newer = jnp.zeros_like(tot) if diag else cs - asc - tot
            if live is not None:
                newer = newer + (live - 1.0) * 1e30
                tot = tot * live
            wgt = jnp.exp(z + _scan_dot(lg, ts) + newer)
            if diag:
                wgt = jnp.where(causal, wgt, 0.0)
            g = wgt * _dot_nt(dos, vb)
            dz = g - sig * (_scan_dot(g, tp) + pre)
            if diag:
                dz = jnp.where(causal, dz, 0.0)
            dzb = _mx(dz)
            dk_acc[rows, :] += _dot_tn(dzb, qs)
            dv_acc[rows, :] += _dot_tn(_mx(wgt), dos)
            return asc + tot, pre + jnp.sum(g, axis=1, keepdims=True), dq + _dot(dzb, kb)

        first = jnp.clip(stop_ref[hp, i], 0, jnp.maximum(i - 1, 0))
        st = (jnp.zeros((2 * tq, 1), _F32), jnp.zeros((2 * tq, 1), _F32), jnp.zeros((2 * tq, LANES), _F32))
        st = lax.fori_loop(first, i - 1, block, st)
        st = block(jnp.maximum(i - 1, 0), st, live=jnp.where(i >= 1, 1.0, 0.0))
        dq = block(i, st, diag=True)[2]
        dq_ref[...] = (jnp.where(masks[0], dq[:tq], dq[tq:]) * scale).astype(dq_ref.dtype)

        @pl.when(i == nq - 1)
        def _():
            cols = pl.ds(pl.multiple_of(hp * LANES, LANES), LANES)
            for acc, out in ((dk_acc, dk_hbm), (dv_acc, dv_hbm)):
                def cast(n, carry, acc=acc):
                    rows = pl.ds(pl.multiple_of(n * chunk, chunk), chunk)
                    stage[rows, :] = acc[rows, :].astype(stage.dtype)
                    return carry
                lax.fori_loop(0, s // chunk, cast, 0)
                cp = pltpu.make_async_copy(stage, out.at[:, cols], sem)
                cp.start()
                cp.wait()

    q_spec = pl.BlockSpec((tq, LANES), lambda h, i, *_: (i, h))
    grid_spec = pltpu.PrefetchScalarGridSpec(
        num_scalar_prefetch=1, grid=(n_hp, nq),
        in_specs=[q_spec, q_spec, q_spec, pl.BlockSpec((s, LANES), lambda h, i, *_: (0, h)),
                  pl.BlockSpec((s, LANES), lambda h, i, *_: (0, n_hp + h)),
                  pl.BlockSpec((tq, tq), lambda h, i, *_: (0, 0)), pl.BlockSpec((tq, tq), lambda h, i, *_: (0, 0))],
        out_specs=[q_spec, pl.BlockSpec(memory_space=pl.ANY), pl.BlockSpec(memory_space=pl.ANY)],
        scratch_shapes=[pltpu.VMEM((s, LANES), _F32), pltpu.VMEM((s, LANES), _F32), pltpu.VMEM((s, LANES), _MXU_DTYPE),
                        pltpu.SemaphoreType.DMA],
    )
    return pl.pallas_call(
        body, name="attn_bwd", grid_spec=grid_spec,
        out_shape=[jax.ShapeDtypeStruct((s, a), _MXU_DTYPE)] * 3,
        compiler_params=_params(2),
    )(stop, q, do, c_tot, kv, kv, tri_suffix, tri_prefix)


def _proj_bwd(x1, dx2, dq, dgate, dk, dv, w_in_b, w_kv, gain_b, gain_kv):
    s, d = x1.shape
    a = dq.shape[1]
    tm = min(ROW_TILE, s)

    def body(x1_ref, dx2_ref, dq_ref, dgate_ref, dk_ref, dv_ref, wib_ref, wkv_ref, gb_ref, gkv_ref,
             dx1_ref, gwib_out, gwkv_out, gnb_ref, gnkv_ref, gwib_ref, gwkv_ref):
        step = pl.program_id(0)
        _zero_at_first(step, gwib_ref, gwkv_ref, gnb_ref, gnkv_ref)
        r, xh = _rms_stats(x1_ref[...])
        gb, gkv = gb_ref[...], gkv_ref[...]
        hb, hk = _mx(xh * gb), _mx(xh * gkv)
        dq, dgate, dk, dv = dq_ref[...], dgate_ref[...], dk_ref[...], dv_ref[...]
        gwib_ref[:, :a] += _dot_tn(hb, dq)
        gwib_ref[:, a:] += _dot_tn(hb, dgate)
        gwkv_ref[:, :a] += _dot_tn(hk, dk)
        gwkv_ref[:, a:] += _dot_tn(hk, dv)
        d_hb = _dot_nt(dq, wib_ref[:, :a]) + _dot_nt(dgate, wib_ref[:, a:])
        d_hk = _dot_nt(dk, wkv_ref[:, :a]) + _dot_nt(dv, wkv_ref[:, a:])
        gnb_ref[...] += jnp.sum(d_hb * xh, axis=0, keepdims=True)
        gnkv_ref[...] += jnp.sum(d_hk * xh, axis=0, keepdims=True)
        dx1_ref[...] = dx2_ref[...] + _rms_bwd(r, xh, d_hb * gb + d_hk * gkv)
        _emit_at_last(step, s // tm - 1, [(gwib_ref, gwib_out), (gwkv_ref, gwkv_out)])

    return pl.pallas_call(
        body, name="proj_bwd", grid=(s // tm,),
        in_specs=[_row_spec(tm, d), _row_spec(tm, d)] + [_row_spec(tm, a)] * 4
        + [_full_spec((d, 2 * a)), _full_spec((d, 2 * a)), _full_spec((1, d)), _full_spec((1, d))],
        out_specs=[_row_spec(tm, d), _full_spec((d, 2 * a)), _full_spec((d, 2 * a)), _full_spec((1, d)),
                   _full_spec((1, d))],
        out_shape=[jax.ShapeDtypeStruct((s, d), _F32), jax.ShapeDtypeStruct((d, 2 * a), _MXU_DTYPE),
                   jax.ShapeDtypeStruct((d, 2 * a), _MXU_DTYPE), jax.ShapeDtypeStruct((1, d), _F32),
                   jax.ShapeDtypeStruct((1, d), _F32)],
        scratch_shapes=[pltpu.VMEM((d, 2 * a), _F32), pltpu.VMEM((d, 2 * a), _F32)],
        compiler_params=_params(1),
    )(x1, dx2, dq, dgate, dk, dv, w_in_b, w_kv, gain_b, gain_kv)


def _layer_a_bwd(u, dx1, conv, w_out_a):
    s, d = dx1.shape
    tm = min(ROW_TILE, s)
    n = s // tm
    per8 = tm // 8

    def body(u_ref, uprev_ref, dx1_ref, conv_ref, woa_ref, du_ref, gwoa_out, gconv_ref, halo_ref, gwoa_ref):
        step = pl.program_id(0)
        _zero_at_first(step, gwoa_ref, gconv_ref, halo_ref)
        b, c, xin, g = (u_ref[:, k * d:(k + 1) * d] for k in range(4))
        p = c * xin
        before = uprev_ref[:, d:2 * d] * uprev_ref[:, 2 * d:3 * d]
        before = jnp.where(step == n - 1, jnp.zeros_like(before), before)
        p1, p2 = _shift_rows_down(p, before, tm)
        w = conv_ref[...]
        cv = w[0:1, :] * p2 + w[1:2, :] * p1 + w[2:3, :] * p
        sl, dsl = _silu_and_grad(g)
        y = b * cv
        dxb = _mx(dx1_ref[...])
        gwoa_ref[...] += _dot_tn(_mx(y * sl), dxb)
        d_ya = _dot_nt(dxb, woa_ref[...])
        d_y = d_ya * sl
        d_cv = d_y * b
        n1, n2 = _shift_rows_up(d_cv, halo_ref[...], tm)
        halo_ref[...] = d_cv[0:8, :]
        d_p = w[2:3, :] * d_cv + w[1:2, :] * n1 + w[0:1, :] * n2
        gconv_ref[0:1, :] += jnp.sum(d_cv * p2, axis=0, keepdims=True)
        gconv_ref[1:2, :] += jnp.sum(d_cv * p1, axis=0, keepdims=True)
        gconv_ref[2:3, :] += jnp.sum(d_cv * p, axis=0, keepdims=True)
        du_ref[:, 0:d] = (d_y * cv).astype(du_ref.dtype)
        du_ref[:, d:2 * d] = (d_p * xin).astype(du_ref.dtype)
        du_ref[:, 2 * d:3 * d] = (d_p * c).astype(du_ref.dtype)
        du_ref[:, 3 * d:4 * d] = (d_ya * y * dsl).astype(du_ref.dtype)
        _emit_at_last(step, n - 1, [(gwoa_ref, gwoa_out)])

    def rev(i):
        return (n - 1 - i, 0)

    return pl.pallas_call(
        body, name="layer_a_bwd", grid=(n,),
        in_specs=[pl.BlockSpec((tm, 4 * d), rev),
                  pl.BlockSpec((8, 4 * d), lambda i: (jnp.maximum((n - 1 - i) * per8 - 1, 0), 0)),
                  pl.BlockSpec((tm, d), rev), _full_spec((8, d)), _full_spec((d, d))],
        out_specs=[pl.BlockSpec((tm, 4 * d), rev), _full_spec((d, d)), _full_spec((8, d))],
        out_shape=[jax.ShapeDtypeStruct((s, 4 * d), _MXU_DTYPE), jax.ShapeDtypeStruct((d, d), _MXU_DTYPE),
                   jax.ShapeDtypeStruct((8, d), _F32)],
        scratch_shapes=[pltpu.VMEM((8, d), _F32), pltpu.VMEM((d, d), _F32)],
        compiler_params=_params(1),
    )(u, u, dx1, conv, w_out_a)


def _grad_w_in_a(h, du, nb, grads):
    s, d = h.shape
    bn = du.shape[1] // nb
    tm = min(BIG_ROW_TILE, s)
    half = nb // 2
    n = len(grads)
    steps = s // tm

    def body(h_ref, du_ref, *refs):
        gw_out, gw_ref = refs[n], refs[-1]
        start, wait = _exchange_ops(refs[:n], refs[n + 1:2 * n + 1], *refs[2 * n + 1:-1], True)
        jh, i = pl.program_id(0), pl.program_id(1)
        pl.when(jnp.logical_and(jh == 0, i == 0))(start)
        _zero_at_first(i, gw_ref)
        hv = h_ref[...]
        for j in range(half):
            gw_ref[j] += _dot_tn(hv, du_ref[:, j * bn:(j + 1) * bn])
        _emit_at_last(i, steps - 1, [(gw_ref, gw_out)])
        pl.when(jnp.logical_and(jh == 1, i == steps - 1))(wait)

    outs = pl.pallas_call(
        body, name="grad_w_in_a", grid=(2, steps),
        in_specs=[pl.BlockSpec((tm, d), lambda jh, i: (i, 0)), pl.BlockSpec((tm, half * bn), lambda jh, i: (i, jh))]
        + [_ANY] * n,
        out_specs=[pl.BlockSpec((half, d, bn), lambda jh, i: (jh, 0, 0))] + [_ANY] * n,
        out_shape=[jax.ShapeDtypeStruct((nb, d, bn), _MXU_DTYPE)] + _exchange_shapes(grads, True),
        scratch_shapes=_exchange_sems(n) + [pltpu.VMEM((half, d, bn), _F32)],
        compiler_params=_params(2),
    )(h, du, *grads)
    return outs[0], outs[1:]


def _input_grad(x, dx1, du, w_in, gain):
    s, d = x.shape
    f = w_in.shape[1]
    tm = min(BIG_ROW_TILE, s)

    def body(x_ref, dx1_ref, du_ref, w_ref, g_ref, dx_ref, gn_ref):
        _zero_at_first(pl.program_id(0), gn_ref)
        r, xh = _rms_stats(x_ref[...])
        d_h = _dot_nt(du_ref[...], w_ref[...])
        gn_ref[...] += jnp.sum(d_h * xh, axis=0, keepdims=True)
        dx_ref[...] = dx1_ref[...] + _rms_bwd(r, xh, d_h * g_ref[...])

    return pl.pallas_call(
        body, name="input_grad", grid=(s // tm,),
        in_specs=[_row_spec(tm, d), _row_spec(tm, d), _row_spec(tm, f), _full_spec((d, f)), _full_spec((1, d))],
        out_specs=[_row_spec(tm, d), _full_spec((1, d))],
        out_shape=[jax.ShapeDtypeStruct((s, d), _F32), jax.ShapeDtypeStruct((1, d), _F32)],
        compiler_params=_params(1),
    )(x, dx1, du, w_in, gain)


def _reduce_adamw(name, parts, w, m, v):
    rows, cols = w.shape
    tr = min(256, rows)
    c1 = 1.0 - ADAM_B1 ** ADAM_STEP
    c2 = 1.0 - ADAM_B2 ** ADAM_STEP

    def body(p_ref, w_ref, m_ref, v_ref, g_ref, d_ref, nm_ref, nv_ref):
        g = p_ref[0].astype(_F32)
        for k in range(1, N_DEV):
            g = g + p_ref[k].astype(_F32)
        nm = ADAM_B1 * m_ref[...] + (1.0 - ADAM_B1) * g
        nv = ADAM_B2 * v_ref[...] + (1.0 - ADAM_B2) * (g * g)
        g_ref[...] = g
        nm_ref[...] = nm
        nv_ref[...] = nv
        d_ref[...] = -ADAM_LR * ((nm / c1) / (jnp.sqrt(nv / c2) + ADAM_EPS) + ADAM_WD * w_ref[...])

    tile = _row_spec(tr, cols)
    return pl.pallas_call(
        body, name=name, grid=(rows // tr,),
        in_specs=[pl.BlockSpec((N_DEV, tr, cols), lambda i: (0, i, 0)), tile, tile, tile],
        out_specs=[tile] * 4,
        out_shape=[jax.ShapeDtypeStruct((rows, cols), _F32)] * 4,
        compiler_params=_params(1),
    )(parts, w, m, v)


def _pad_rows(a, rows=8):
    return jnp.pad(a, ((0, rows - a.shape[0]), (0, 0)))


def kernel(x, norm_a, w_in_a, conv_a, w_out_a, norm_kv, w_kv, norm_b, w_in_b, w_out_b, norm_f, loss_target, m_norm_a, m_w_in_a, m_conv_a, m_w_out_a, m_norm_kv, m_w_kv, m_norm_b, m_w_in_b, m_w_out_b, m_norm_f, v_norm_a, v_w_in_a, v_conv_a, v_w_out_a, v_norm_kv, v_w_kv, v_norm_b, v_w_in_b, v_w_out_b, v_norm_f):
    x0 = x[0]
    s, d = x0.shape
    a = d // 2
    sh = d // N_DEV
    me = 4 * lax.axis_index("x") + 2 * lax.axis_index("y") + lax.axis_index("c")

    small_a = _pad_rows(jnp.concatenate([norm_a, conv_a[0]], axis=0))
    wia_g, small_g = _exchange("exchange_gather", [_mx(w_in_a[0]), small_a], scatter=False)
    small_f = small_g.transpose(1, 0, 2).reshape(8, d)
    gain_a = small_f[0:1]
    conv_f = _pad_rows(small_f[1:4])
    gain_kv, gain_b, gain_f = norm_kv.reshape(1, d), norm_b.reshape(1, d), norm_f.reshape(1, d)

    wia_f = wia_g.transpose(1, 0, 2).reshape(d, 4 * d)
    u, h, (woa_g, wkv_g, wib_g, wob_g) = _ln_matmul_in_a(
        x0, gain_a, wia_f, [_mx(w_out_a[0]), _mx(w_kv), _mx(w_in_b[0]), _mx(w_out_b[0])])
    woa_f = woa_g.reshape(d, d)
    wkv_f = wkv_g.reshape(d, 2 * a)
    wib_f = wib_g.reshape(d, 2 * a)
    wob_f = wob_g.transpose(1, 0, 2).reshape(a, d)
    x1, kv, q, gate = _layer_a_out(u, x0, conv_f, woa_f, gain_kv, gain_b, wkv_f, wib_f)
    tq = min(ATT_TILE, s)
    idx = jnp.arange(tq)
    tri_suffix = _mx(idx[:, None] >= idx[None, :])
    tri_prefix = _mx(idx[:, None] <= idx[None, :])
    o, c_tot, stop = _attn_fwd(q, kv, tri_suffix)
    dx2, do, dgate, g_wob, g_norm_f, loss_part = _layer_b_out_loss_bwd(o, gate, x1, wob_f, gain_f, loss_target[0])

    dq, dk, dv = _attn_bwd(q, kv, do, c_tot, stop, tri_suffix, tri_prefix)
    dx1, g_wib, g_wkv, g_norm_b, g_norm_kv = _proj_bwd(x1, dx2, dq, dgate, dk, dv, wib_f, wkv_f, gain_b, gain_kv)
    du, g_woa, g_conv = _layer_a_bwd(u, dx1, conv_f, woa_f)
    dx0, g_norm_a = _input_grad(x0, dx1, du, wia_f, gain_a)

    g_wia, (p_woa, p_wkv, p_wib, p_wob) = _grad_w_in_a(
        h, du, N_DEV,
        [g_woa.reshape(N_DEV, sh, d), g_wkv.reshape(N_DEV, sh, 2 * a), g_wib.reshape(N_DEV, sh, 2 * a),
         g_wob.reshape(a, N_DEV, sh).transpose(1, 0, 2)])
    (p_wia,) = _exchange("exchange_scatter", [g_wia], scatter=True)
    small_grads = jnp.concatenate(
        [g_norm_a, g_conv[0:3], g_norm_kv, g_norm_b, g_norm_f, jnp.pad(loss_part, ((0, 0), (0, d - LANES)))], axis=0)
    (p_small,) = _exchange("exchange_small", [small_grads], scatter=False)

    upd_wia = _reduce_adamw("adamw_w_in_a", p_wia, w_in_a[0], m_w_in_a[0], v_w_in_a[0])
    upd_woa = _reduce_adamw("adamw_w_out_a", p_woa, w_out_a[0], m_w_out_a[0], v_w_out_a[0])
    upd_wkv = _reduce_adamw("adamw_w_kv", p_wkv, w_kv, m_w_kv, v_w_kv)
    upd_wib = _reduce_adamw("adamw_w_in_b", p_wib, w_in_b[0], m_w_in_b[0], v_w_in_b[0])
    upd_wob = _reduce_adamw("adamw_w_out_b", p_wob, w_out_b[0], m_w_out_b[0], v_w_out_b[0])

    def rep(a1, a2, a3):
        return jnp.concatenate([jnp.zeros((4, d), _F32), a1.reshape(1, d), a2.reshape(1, d), a3.reshape(1, d),
                                jnp.zeros((1, d), _F32)], axis=0)

    upd_rep = _reduce_adamw("adamw_replicated", p_small, rep(norm_kv, norm_b, norm_f),
                            rep(m_norm_kv, m_norm_b, m_norm_f), rep(v_norm_kv, v_norm_b, v_norm_f))

    def mine(n1, cv1):
        return _pad_rows(jnp.concatenate([n1, cv1[0]], axis=0))

    p_mine = lax.dynamic_slice(p_small, (0, 0, me * sh), (N_DEV, 8, sh))
    upd_mine = _reduce_adamw("adamw_sharded_small", p_mine, mine(norm_a, conv_a), mine(m_norm_a, m_conv_a),
                             mine(v_norm_a, v_conv_a))

    loss = upd_rep[0][7, 0]
    groups = []
    for k in range(4):
        groups.append([
            upd_mine[k][0:1], upd_wia[k][None], upd_mine[k][1:4][None], upd_woa[k][None], upd_rep[k][4],
            upd_wkv[k], upd_rep[k][5:6], upd_wib[k][None], upd_wob[k][None], upd_rep[k][6]])
    return (loss, dx0[None], *groups[0], *groups[1], *groups[2], *groups[3])
```

```python
import jax
import jax.numpy as jnp
from jax import lax
from jax.experimental import pallas as pl
from jax.experimental.pallas import tpu as pltpu

_MXU_DTYPE = jnp.bfloat16
_F32 = jnp.float32

RMS_EPS = 1e-6
HEAD_DIM = 64
LANES = 128
N_DEV = 8
ATT_TILE = 256
ROW_TILE = 256
BIG_ROW_TILE = 512
VMEM_LIMIT = 56 * 1024 * 1024
SKIP_LOG = -110.0

ADAM_LR = 0.001
ADAM_B1 = 0.9
ADAM_B2 = 0.999
ADAM_EPS = 1e-08
ADAM_WD = 0.01
ADAM_STEP = 10

_NT = (((1,), (1,)), ((), ()))
_TN = (((0,), (0,)), ((), ()))


def _params(n_grid):
    return pltpu.CompilerParams(dimension_semantics=("arbitrary",) * n_grid, vmem_limit_bytes=VMEM_LIMIT)


def _dot(a, b):
    return jnp.dot(a, b, preferred_element_type=_F32)


def _dot_nt(a, b):
    return lax.dot_general(a, b, _NT, preferred_element_type=_F32)


def _dot_tn(a, b):
    return lax.dot_general(a, b, _TN, preferred_element_type=_F32)


def _mx(a):
    return a.astype(_MXU_DTYPE)


def _sigmoid(a):
    return 1.0 / (1.0 + jnp.exp(-a))


def _rms_stats(xv):
    r = lax.rsqrt(jnp.mean(xv * xv, axis=-1, keepdims=True) + RMS_EPS)
    return r, xv * r


def _rms_bwd(r, xh, dyg):
    return r * (dyg - xh * jnp.mean(dyg * xh, axis=-1, keepdims=True))


def _row_spec(tm, width):
    return pl.BlockSpec((tm, width), lambda i: (i, 0))


def _full_spec(shape):
    zeros = (0,) * len(shape)
    return pl.BlockSpec(shape, lambda *_: zeros)


def _zero_at_first(step, *refs):
    @pl.when(step == 0)
    def _():
        for ref in refs:
            ref[...] = jnp.zeros_like(ref)


def _emit_at_last(step, last, pairs):
    @pl.when(step == last)
    def _():
        for acc, out in pairs:
            out[...] = acc[...].astype(out.dtype)


def _slot(ref, idx, axis):
    if axis == 0:
        return ref.at[idx]
    width = ref.shape[1] // N_DEV
    return ref.at[:, pl.ds(pl.multiple_of(idx * width, width), width)]


def _exchange_ops(ins, outs, send_sems, recv_sems, local_sems, scatter, axes):
    n = len(ins)
    x, y, c = lax.axis_index("x"), lax.axis_index("y"), lax.axis_index("c")
    me = 4 * x + 2 * y + c

    def remote(t, m, landed):
        px = 1 - x if m & 4 else x
        py = 1 - y if m & 2 else y
        pc = 1 - c if m & 1 else c
        idx = 4 * px + 2 * py + pc
        k = t * (N_DEV - 1) + m - 1
        return pltpu.make_async_remote_copy(
            src_ref=_slot(ins[t], idx, axes[t]) if scatter else ins[t],
            dst_ref=_slot(outs[t], idx if landed else me, axes[t]),
            send_sem=send_sems.at[k], recv_sem=recv_sems.at[k],
            device_id=(px, py, pc), device_id_type=pl.DeviceIdType.MESH)

    def local(t):
        return pltpu.make_async_copy(_slot(ins[t], me, axes[t]) if scatter else ins[t], _slot(outs[t], me, axes[t]),
                                     local_sems.at[t])

    def start():
        for t in range(n):
            local(t).start()
        for m in range(1, N_DEV):
            for t in range(n):
                remote(t, m, False).start()

    def wait():
        for m in range(1, N_DEV):
            for t in range(n):
                remote(t, m, True).wait_recv()
        for m in range(1, N_DEV):
            for t in range(n):
                remote(t, m, False).wait_send()
        for t in range(n):
            local(t).wait()

    return start, wait


def _exchange_shapes(arrays, scatter, axes):
    if scatter:
        return [jax.ShapeDtypeStruct(a.shape, a.dtype) for a in arrays]
    return [jax.ShapeDtypeStruct((N_DEV,) + a.shape if ax == 0 else (a.shape[0], N_DEV * a.shape[1]), a.dtype)
            for a, ax in zip(arrays, axes)]


def _exchange_sems(n):
    return [pltpu.SemaphoreType.DMA((n * (N_DEV - 1),)), pltpu.SemaphoreType.DMA((n * (N_DEV - 1),)),
            pltpu.SemaphoreType.DMA((n,))]


_ANY = pl.BlockSpec(memory_space=pl.ANY)


def _exchange(name, arrays, scatter, axes):
    n = len(arrays)

    def body(*refs):
        start, wait = _exchange_ops(refs[:n], refs[n:2 * n], *refs[2 * n:], scatter, axes)
        start()
        wait()

    return pl.pallas_call(
        body, name=name, in_specs=[_ANY] * n, out_specs=[_ANY] * n,
        out_shape=_exchange_shapes(arrays, scatter, axes), scratch_shapes=_exchange_sems(n),
    )(*arrays)


def _gather_two_level(name, arrays, axes):
    n = len(arrays)

    def body(*refs):
        ins, outs = refs[:n], refs[n:2 * n]
        send_sems, recv_sems, local_sems = refs[2 * n:]
        x, y, c = lax.axis_index("x"), lax.axis_index("y"), lax.axis_index("c")
        me, sibling = (x, y, c), (x, y, 1 - c)
        chips = [(1 - x, y), (x, 1 - y), (1 - x, 1 - y)]

        def rows(t, dev):
            return _slot(outs[t], 4 * dev[0] + 2 * dev[1] + dev[2], axes[t])

        def copy(t, k, block, to, src=None):
            return pltpu.make_async_remote_copy(
                src_ref=rows(t, block) if src is None else src, dst_ref=rows(t, block),
                send_sem=send_sems.at[t * (N_DEV - 1) + k], recv_sem=recv_sems.at[t * (N_DEV - 1) + k],
                device_id=to, device_id_type=pl.DeviceIdType.MESH)

        sent = []
        for t in range(n):
            pltpu.make_async_copy(ins[t], rows(t, me), local_sems.at[t]).start()
            sent.append(copy(t, 0, me, sibling, src=ins[t]))
            sent += [copy(t, 1 + j, me, (*chip, c), src=ins[t]) for j, chip in enumerate(chips)]
        for cp in sent:
            cp.start()
        for j, chip in enumerate(chips):
            for t in range(n):
                copy(t, 1 + j, (*chip, c), me).wait_recv()
                passed = copy(t, 4 + j, (*chip, c), sibling)
                passed.start()
                sent.append(passed)
        for t in range(n):
            copy(t, 0, sibling, me).wait_recv()
            for j, chip in enumerate(chips):
                copy(t, 4 + j, (*chip, 1 - c), me).wait_recv()
        for cp in sent:
            cp.wait_send()
        for t in range(n):
            pltpu.make_async_copy(ins[t], rows(t, me), local_sems.at[t]).wait()

    return pl.pallas_call(
        body, name=name, in_specs=[_ANY] * n, out_specs=[_ANY] * n,
        out_shape=_exchange_shapes(arrays, False, axes), scratch_shapes=_exchange_sems(n),
    )(*arrays)


def _reduce_scatter_two_level(name, g):
    n_chip = N_DEV // 2
    _, r, cc = g.shape
    chunk = min(256, r)

    def body(g_ref, out_ref, mine, theirs, total, load_sems, pair_send, pair_recv, chip_send, chip_recv, keep_sem):
        x, y, c = lax.axis_index("x"), lax.axis_index("y"), lax.axis_index("c")
        my_chip = 2 * x + y

        def load(q):
            return pltpu.make_async_copy(g_ref.at[2 * q + c], mine.at[q], load_sems.at[q])

        def swap(q):
            return pltpu.make_async_remote_copy(
                src_ref=g_ref.at[2 * q + 1 - c], dst_ref=theirs.at[q], send_sem=pair_send.at[q],
                recv_sem=pair_recv.at[q], device_id=(x, y, 1 - c), device_id_type=pl.DeviceIdType.MESH)

        for q in range(n_chip):
            load(q).start()
            swap(q).start()
        for q in range(n_chip):
            load(q).wait()
            swap(q).wait_recv()

            def add(i, carry, q=q):
                rows = pl.ds(pl.multiple_of(i * chunk, chunk), chunk)
                total[q, rows, :] = (mine[q, rows, :].astype(_F32) + theirs[q, rows, :].astype(_F32)).astype(total.dtype)
                return carry
            lax.fori_loop(0, r // chunk, add, 0)

        def send(k, landed):
            px = 1 - x if k in (0, 2) else x
            py = 1 - y if k in (1, 2) else y
            peer_chip = 2 * px + py
            return pltpu.make_async_remote_copy(
                src_ref=total.at[peer_chip], dst_ref=out_ref.at[peer_chip if landed else my_chip],
                send_sem=chip_send.at[k], recv_sem=chip_recv.at[k],
                device_id=(px, py, c), device_id_type=pl.DeviceIdType.MESH)

        keep = pltpu.make_async_copy(total.at[my_chip], out_ref.at[my_chip], keep_sem)
        keep.start()
        for k in range(n_chip - 1):
            send(k, False).start()
        for k in range(n_chip - 1):
            send(k, True).wait_recv()
        for k in range(n_chip - 1):
            send(k, False).wait_send()
        for q in range(n_chip):
            swap(q).wait_send()
        keep.wait()

    slab = pltpu.VMEM((n_chip, r, cc), g.dtype)
    return pl.pallas_call(
        body, name=name, in_specs=[_ANY], out_specs=_ANY,
        out_shape=jax.ShapeDtypeStruct((n_chip, r, cc), g.dtype),
        scratch_shapes=[slab, slab, slab, pltpu.SemaphoreType.DMA((n_chip,)), pltpu.SemaphoreType.DMA((n_chip,)),
                        pltpu.SemaphoreType.DMA((n_chip,)), pltpu.SemaphoreType.DMA((n_chip - 1,)),
                        pltpu.SemaphoreType.DMA((n_chip - 1,)), pltpu.SemaphoreType.DMA],
        compiler_params=pltpu.CompilerParams(vmem_limit_bytes=VMEM_LIMIT),
    )(g)


def _ln_matmul_in_a(x, gain, w_in, shards, axes):
    s, d = x.shape
    f = w_in.shape[1]
    tm = min(BIG_ROW_TILE, s)
    n = len(shards)
    last = s // tm - 1

    def body(x_ref, g_ref, w_ref, *refs):
        u_ref, h_ref = refs[n:n + 2]
        start, wait = _exchange_ops(refs[:n], refs[n + 2:2 * n + 2], *refs[2 * n + 2:], False, axes)
        pl.when(pl.program_id(0) == 0)(start)
        _, xh = _rms_stats(x_ref[...])
        h = _mx(xh * g_ref[...])
        h_ref[...] = h
        u_ref[...] = _dot(h, w_ref[...])
        pl.when(pl.program_id(0) == last)(wait)

    outs = pl.pallas_call(
        body, name="ln_matmul_in_a", grid=(s // tm,),
        in_specs=[_row_spec(tm, d), _full_spec((1, d)), _full_spec((d, f))] + [_ANY] * n,
        out_specs=[_row_spec(tm, f), _row_spec(tm, d)] + [_ANY] * n,
        out_shape=[jax.ShapeDtypeStruct((s, f), _F32), jax.ShapeDtypeStruct((s, d), _MXU_DTYPE)]
        + _exchange_shapes(shards, False, axes),
        scratch_shapes=_exchange_sems(n),
        compiler_params=_params(1),
    )(x, gain, w_in, *shards)
    return outs[0], outs[1], outs[2:]


def _shift_rows_down(p, before, tm):
    row = lax.broadcasted_iota(jnp.int32, (8, p.shape[1]), 0)
    r1, r2 = pltpu.roll(p, 1, 0), pltpu.roll(p, 2, 0)
    top1 = jnp.where(row == 0, before[7:8, :], r1[0:8, :])
    top2 = jnp.where(row == 0, before[6:7, :], jnp.where(row == 1, before[7:8, :], r2[0:8, :]))
    return jnp.concatenate([top1, r1[8:, :]], axis=0), jnp.concatenate([top2, r2[8:, :]], axis=0)


def _shift_rows_up(p, after, tm):
    row = lax.broadcasted_iota(jnp.int32, (8, p.shape[1]), 0)
    r1, r2 = pltpu.roll(p, tm - 1, 0), pltpu.roll(p, tm - 2, 0)
    end1 = jnp.where(row == 7, after[0:1, :], r1[tm - 8:, :])
    end2 = jnp.where(row == 6, after[0:1, :], jnp.where(row == 7, after[1:2, :], r2[tm - 8:, :]))
    return jnp.concatenate([r1[:tm - 8, :], end1], axis=0), jnp.concatenate([r2[:tm - 8, :], end2], axis=0)


def _layer_a_out(u, x, conv, w_out_a, gain_kv, gain_b, w_kv, w_in_b):
    s, d = x.shape
    a = w_kv.shape[1] // 2
    tm = min(ROW_TILE, s)
    scale = HEAD_DIM ** -0.5

    def body(u_ref, x_ref, conv_ref, woa_ref, gkv_ref, gb_ref, wkv_ref, wib_ref,
             x1_ref, kv_ref, q_ref, gate_ref, halo_ref):
        _zero_at_first(pl.program_id(0), halo_ref)
        b, c, xin, g = (u_ref[:, k * d:(k + 1) * d] for k in range(4))
        p = c * xin
        p1, p2 = _shift_rows_down(p, halo_ref[...], tm)
        halo_ref[...] = p[tm - 8:tm, :]
        w = conv_ref[...]
        cv = w[0:1, :] * p2 + w[1:2, :] * p1 + w[2:3, :] * p
        ya = (b * cv) * (g * _sigmoid(g))
        x1 = x_ref[...] + _dot(_mx(ya), woa_ref[...])
        x1_ref[...] = x1
        _, xh = _rms_stats(x1)
        kv_ref[...] = _dot(_mx(xh * gkv_ref[...]), wkv_ref[...]).astype(kv_ref.dtype)
        qg = _dot(_mx(xh * gb_ref[...]), wib_ref[...])
        q_ref[...] = (qg[:, :a] * scale).astype(q_ref.dtype)
        gate_ref[...] = qg[:, a:]

    return pl.pallas_call(
        body, name="layer_a_out", grid=(s // tm,),
        in_specs=[_row_spec(tm, 4 * d), _row_spec(tm, d), _full_spec((8, d)), _full_spec((d, d)),
                  _full_spec((1, d)), _full_spec((1, d)), _full_spec((d, 2 * a)), _full_spec((d, 2 * a))],
        out_specs=[_row_spec(tm, d), _row_spec(tm, 2 * a), _row_spec(tm, a), _row_spec(tm, a)],
        out_shape=[jax.ShapeDtypeStruct((s, d), _F32), jax.ShapeDtypeStruct((s, 2 * a), _MXU_DTYPE),
                   jax.ShapeDtypeStruct((s, a), _MXU_DTYPE), jax.ShapeDtypeStruct((s, a), _F32)],
        scratch_shapes=[pltpu.VMEM((8, d), _F32)],
        compiler_params=_params(1),
    )(u, x, conv, w_out_a, gain_kv, gain_b, w_kv, w_in_b)


def _neg_softplus(z):
    return -(jnp.maximum(z, 0.0) + jnp.log(1.0 + jnp.exp(-jnp.abs(z))))


def _scan_dot(val, tri):
    return _dot(_mx(val), tri)


def _head_masks():
    lane = lax.broadcasted_iota(jnp.int32, (1, LANES), 1)
    return [lane < HEAD_DIM, lane >= HEAD_DIM]


def _stacked_causal(tq):
    row = lax.broadcasted_iota(jnp.int32, (2 * tq, tq), 0)
    col = lax.broadcasted_iota(jnp.int32, (2 * tq, tq), 1)
    return col < jnp.where(row >= tq, row - tq, row)


def _attn_fwd(q, kv, tri_suffix):
    s, a = q.shape
    n_hp = a // LANES
    tq = min(ATT_TILE, s)
    nq = s // tq

    def body(q_ref, k_ref, v_ref, tri_ref, o_ref, c_ref, stop_ref):
        hp, i = pl.program_id(0), pl.program_id(1)
        qv = q_ref[...]
        tri = tri_ref[...]
        causal = _stacked_causal(tq)
        qs = jnp.concatenate([jnp.where(hmask, qv, jnp.zeros_like(qv)) for hmask in _head_masks()], axis=0)

        def block(j, c, acc, diag=False, live=None):
            off = pl.multiple_of(j * tq, tq)
            kb = k_ref[pl.ds(off, tq), :]
            vb = v_ref[pl.ds(off, tq), :]
            z = _dot_nt(qs, kb)
            lg = _neg_softplus(z)
            if diag:
                lg = jnp.where(causal, lg, 0.0)
            tot = jnp.sum(lg, axis=1, keepdims=True)
            c_in = c
            if live is not None:
                c_in = c + (live - 1.0) * 1e30
                tot = tot * live
            w = jnp.exp(z + _scan_dot(lg, tri) + c_in)
            if diag:
                w = jnp.where(causal, w, 0.0)
            return c + tot, acc + _dot(_mx(w), vb)

        c, acc = block(i, jnp.zeros((2 * tq, 1), _F32), jnp.zeros((2 * tq, LANES), _F32), diag=True)
        c, acc = block(jnp.maximum(i - 1, 0), c, acc, live=jnp.where(i >= 1, 1.0, 0.0))

        def cond(carry):
            return jnp.logical_and(carry[0] >= 0, jnp.max(carry[1]) > SKIP_LOG)

        def step(carry):
            c, acc = block(carry[0], carry[1], carry[2])
            return carry[0] - 1, c, acc

        j, c, acc = lax.while_loop(cond, step, (i - 2, c, acc))
        stop_ref[hp, i] = jnp.maximum(jnp.minimum(j + 1, i - 1), 0)
        first = _head_masks()[0]
        o_ref[...] = jnp.where(first, acc[:tq], acc[tq:])
        c_ref[...] = jnp.where(first, c[:tq], c[tq:])

    q_spec = pl.BlockSpec((tq, LANES), lambda h, i: (i, h))
    return pl.pallas_call(
        body, name="attn_fwd", grid=(n_hp, nq),
        in_specs=[q_spec, pl.BlockSpec((s, LANES), lambda h, i: (0, h)),
                  pl.BlockSpec((s, LANES), lambda h, i: (0, n_hp + h)), _full_spec((tq, tq))],
        out_specs=[q_spec, q_spec, pl.BlockSpec(memory_space=pltpu.SMEM)],
        out_shape=[jax.ShapeDtypeStruct((s, a), _F32), jax.ShapeDtypeStruct((s, a), _F32),
                   jax.ShapeDtypeStruct((n_hp, nq), jnp.int32)],
        compiler_params=_params(2),
    )(q, kv, kv, tri_suffix)


def _silu_and_grad(g):
    sg = _sigmoid(g)
    return g * sg, sg * (1.0 + g * (1.0 - sg))


def _layer_b_out_loss_bwd(o, gate, x1, w_out_b, gain_f, target):
    s, d = x1.shape
    a = o.shape[1]
    tm = min(BIG_ROW_TILE, s)
    last = s // tm - 1

    def body(o_ref, gate_ref, x1_ref, wob_ref, gf_ref, tgt_ref,
             dx2_ref, do_ref, dgate_ref, gw_ref, gnf_ref, loss_ref, gw_acc):
        step = pl.program_id(0)
        _zero_at_first(step, gw_acc, gnf_ref, loss_ref)
        sl, dsl = _silu_and_grad(gate_ref[...])
        ov = o_ref[...]
        ob = _mx(ov * sl)
        wob = wob_ref[...]
        x2 = x1_ref[...] + _dot(ob, wob)
        r, xh = _rms_stats(x2)
        gf = gf_ref[...]
        err = xh * gf - tgt_ref[...]
        part = jnp.sum(jnp.sum(err * err, axis=1, keepdims=True), axis=0, keepdims=True)
        loss_ref[...] += part * (0.5 / d)
        dy = err * (1.0 / d)
        gnf_ref[...] += jnp.sum(dy * xh, axis=0, keepdims=True)
        dx2 = _rms_bwd(r, xh, dy * gf)
        dx2_ref[...] = dx2
        dxb = _mx(dx2)
        d_ob = _dot_nt(dxb, wob)
        gw_acc[...] += _dot_tn(ob, dxb)
        do_ref[...] = (d_ob * sl).astype(do_ref.dtype)
        dgate_ref[...] = (d_ob * ov * dsl).astype(dgate_ref.dtype)
        _emit_at_last(step, last, [(gw_acc, gw_ref)])

    return pl.pallas_call(
        body, name="layer_b_out_loss_bwd", grid=(s // tm,),
        in_specs=[_row_spec(tm, a), _row_spec(tm, a), _row_spec(tm, d), _full_spec((a, d)), _full_spec((1, d)),
                  _row_spec(tm, d)],
        out_specs=[_row_spec(tm, d), _row_spec(tm, a), _row_spec(tm, a), _full_spec((a, d)), _full_spec((1, d)),
                   _full_spec((1, LANES))],
        out_shape=[jax.ShapeDtypeStruct((s, d), _F32), jax.ShapeDtypeStruct((s, a), _MXU_DTYPE),
                   jax.ShapeDtypeStruct((s, a), _MXU_DTYPE), jax.ShapeDtypeStruct((a, d), _MXU_DTYPE),
                   jax.ShapeDtypeStruct((1, d), _F32), jax.ShapeDtypeStruct((1, LANES), _F32)],
        scratch_shapes=[pltpu.VMEM((a, d), _F32)],
        compiler_params=_params(1),
    )(o, gate, x1, w_out_b, gain_f, target)


def _attn_bwd(q, kv, do, c_tot, stop, tri_suffix, tri_prefix):
    s, a = q.shape
    n_hp = a // LANES
    tq = min(ATT_TILE, s)
    nq = s // tq
    scale = HEAD_DIM ** -0.5
    chunk = min(1024, s)

    def body(stop_ref, q_ref, do_ref, c_ref, k_ref, v_ref, ts_ref, tp_ref, dq_ref, dk_hbm, dv_hbm,
             dk_acc, dv_acc, stage, sem):
        hp, i = pl.program_id(0), pl.program_id(1)
        _zero_at_first(i, dk_acc, dv_acc)
        qv, dov, cv = q_ref[...], do_ref[...], c_ref[...]
        ts, tp = ts_ref[...], tp_ref[...]
        causal = _stacked_causal(tq)
        masks = _head_masks()
        qs = jnp.concatenate([jnp.where(hmask, qv, jnp.zeros_like(qv)) for hmask in masks], axis=0)
        dos = jnp.concatenate([jnp.where(hmask, dov, jnp.zeros_like(dov)) for hmask in masks], axis=0)
        cs = jnp.concatenate([cv[:, 0:1], cv[:, HEAD_DIM:HEAD_DIM + 1]], axis=0)

        def block(j, st, diag=False, live=None):
            asc, pre, dq = st
            rows = pl.ds(pl.multiple_of(j * tq, tq), tq)
            kb = k_ref[rows, :]
            vb = v_ref[rows, :]
            z = _dot_nt(qs, kb)
            lg = _neg_softplus(z)
            sig = jnp.exp(z + lg)
            if diag:
                lg = jnp.where(causal, lg, 0.0)
            tot = jnp.sum(lg, axis=1, keepdims=True)
            newer = jnp.zeros_like(tot) if diag else cs - asc - tot
            if live is not None:
                newer = newer + (live - 1.0) * 1e30
                tot = tot * live
            wgt = jnp.exp(z + _scan_dot(lg, ts) + newer)
            if diag:
                wgt = jnp.where(causal, wgt, 0.0)
            g = wgt * _dot_nt(dos, vb)
            dz = g - sig * (_scan_dot(g, tp) + pre)
            if diag:
                dz = jnp.where(causal, dz, 0.0)
            dzb = _mx(dz)
            dk_acc[rows, :] += _dot_tn(dzb, qs)
            dv_acc[rows, :] += _dot_tn(_mx(wgt), dos)
            return asc + tot, pre + jnp.sum(g, axis=1, keepdims=True), dq + _dot(dzb, kb)

        first = jnp.clip(stop_ref[hp, i], 0, jnp.maximum(i - 1, 0))
        st = (jnp.zeros((2 * tq, 1), _F32), jnp.zeros((2 * tq, 1), _F32), jnp.zeros((2 * tq, LANES), _F32))
        st = lax.fori_loop(first, i - 1, block, st)
        st = block(jnp.maximum(i - 1, 0), st, live=jnp.where(i >= 1, 1.0, 0.0))
        dq = block(i, st, diag=True)[2]
        dq_ref[...] = (jnp.where(masks[0], dq[:tq], dq[tq:]) * scale).astype(dq_ref.dtype)

        @pl.when(i == nq - 1)
        def _():
            cols = pl.ds(pl.multiple_of(hp * LANES, LANES), LANES)
            for acc, out in ((dk_acc, dk_hbm), (dv_acc, dv_hbm)):
                def cast(n, carry, acc=acc):
                    rows = pl.ds(pl.multiple_of(n * chunk, chunk), chunk)
                    stage[rows, :] = acc[rows, :].astype(stage.dtype)
                    return carry
                lax.fori_loop(0, s // chunk, cast, 0)
                cp = pltpu.make_async_copy(stage, out.at[:, cols], sem)
                cp.start()
                cp.wait()

    q_spec = pl.BlockSpec((tq, LANES), lambda h, i, *_: (i, h))
    grid_spec = pltpu.PrefetchScalarGridSpec(
        num_scalar_prefetch=1, grid=(n_hp, nq),
        in_specs=[q_spec, q_spec, q_spec, pl.BlockSpec((s, LANES), lambda h, i, *_: (0, h)),
                  pl.BlockSpec((s, LANES), lambda h, i, *_: (0, n_hp + h)),
                  pl.BlockSpec((tq, tq), lambda h, i, *_: (0, 0)), pl.BlockSpec((tq, tq), lambda h, i, *_: (0, 0))],
        out_specs=[q_spec, pl.BlockSpec(memory_space=pl.ANY), pl.BlockSpec(memory_space=pl.ANY)],
        scratch_shapes=[pltpu.VMEM((s, LANES), _F32), pltpu.VMEM((s, LANES), _F32), pltpu.VMEM((s, LANES), _MXU_DTYPE),
                        pltpu.SemaphoreType.DMA],
    )
    return pl.pallas_call(
        body, name="attn_bwd", grid_spec=grid_spec,
        out_shape=[jax.ShapeDtypeStruct((s, a), _MXU_DTYPE)] * 3,
        compiler_params=_params(2),
    )(stop, q, do, c_tot, kv, kv, tri_suffix, tri_prefix)


def _proj_bwd(x1, dx2, dq, dgate, dk, dv, w_in_b, w_kv, gain_b, gain_kv):
    s, d = x1.shape
    a = dq.shape[1]
    tm = min(ROW_TILE, s)

    def body(x1_ref, dx2_ref, dq_ref, dgate_ref, dk_ref, dv_ref, wib_ref, wkv_ref, gb_ref, gkv_ref,
             dx1_ref, gwib_out, gwkv_out, gnb_ref, gnkv_ref, gwib_ref, gwkv_ref):
        step = pl.program_id(0)
        _zero_at_first(step, gwib_ref, gwkv_ref, gnb_ref, gnkv_ref)
        r, xh = _rms_stats(x1_ref[...])
        gb, gkv = gb_ref[...], gkv_ref[...]
        hb, hk = _mx(xh * gb), _mx(xh * gkv)
        dq, dgate, dk, dv = dq_ref[...], dgate_ref[...], dk_ref[...], dv_ref[...]
        gwib_ref[:, :a] += _dot_tn(hb, dq)
        gwib_ref[:, a:] += _dot_tn(hb, dgate)
        gwkv_ref[:, :a] += _dot_tn(hk, dk)
        gwkv_ref[:, a:] += _dot_tn(hk, dv)
        d_hb = _dot_nt(dq, wib_ref[:, :a]) + _dot_nt(dgate, wib_ref[:, a:])
        d_hk = _dot_nt(dk, wkv_ref[:, :a]) + _dot_nt(dv, wkv_ref[:, a:])
        gnb_ref[...] += jnp.sum(d_hb * xh, axis=0, keepdims=True)
        gnkv_ref[...] += jnp.sum(d_hk * xh, axis=0, keepdims=True)
        dx1_ref[...] = dx2_ref[...] + _rms_bwd(r, xh, d_hb * gb + d_hk * gkv)
        _emit_at_last(step, s // tm - 1, [(gwib_ref, gwib_out), (gwkv_ref, gwkv_out)])

    return pl.pallas_call(
        body, name="proj_bwd", grid=(s // tm,),
        in_specs=[_row_spec(tm, d), _row_spec(tm, d)] + [_row_spec(tm, a)] * 4
        + [_full_spec((d, 2 * a)), _full_spec((d, 2 * a)), _full_spec((1, d)), _full_spec((1, d))],
        out_specs=[_row_spec(tm, d), _full_spec((d, 2 * a)), _full_spec((d, 2 * a)), _full_spec((1, d)),
                   _full_spec((1, d))],
        out_shape=[jax.ShapeDtypeStruct((s, d), _F32), jax.ShapeDtypeStruct((d, 2 * a), _MXU_DTYPE),
                   jax.ShapeDtypeStruct((d, 2 * a), _MXU_DTYPE), jax.ShapeDtypeStruct((1, d), _F32),
                   jax.ShapeDtypeStruct((1, d), _F32)],
        scratch_shapes=[pltpu.VMEM((d, 2 * a), _F32), pltpu.VMEM((d, 2 * a), _F32)],
        compiler_params=_params(1),
    )(x1, dx2, dq, dgate, dk, dv, w_in_b, w_kv, gain_b, gain_kv)


def _layer_a_bwd(u, dx1, conv, w_out_a):
    s, d = dx1.shape
    tm = min(ROW_TILE, s)
    n = s // tm
    per8 = tm // 8

    def body(u_ref, uprev_ref, dx1_ref, conv_ref, woa_ref, du_ref, gwoa_out, gconv_ref, halo_ref, gwoa_ref):
        step = pl.program_id(0)
        _zero_at_first(step, gwoa_ref, gconv_ref, halo_ref)
        b, c, xin, g = (u_ref[:, k * d:(k + 1) * d] for k in range(4))
        p = c * xin
        before = uprev_ref[:, d:2 * d] * uprev_ref[:, 2 * d:3 * d]
        before = jnp.where(step == n - 1, jnp.zeros_like(before), before)
        p1, p2 = _shift_rows_down(p, before, tm)
        w = conv_ref[...]
        cv = w[0:1, :] * p2 + w[1:2, :] * p1 + w[2:3, :] * p
        sl, dsl = _silu_and_grad(g)
        y = b * cv
        dxb = _mx(dx1_ref[...])
        gwoa_ref[...] += _dot_tn(_mx(y * sl), dxb)
        d_ya = _dot_nt(dxb, woa_ref[...])
        d_y = d_ya * sl
        d_cv = d_y * b
        n1, n2 = _shift_rows_up(d_cv, halo_ref[...], tm)
        halo_ref[...] = d_cv[0:8, :]
        d_p = w[2:3, :] * d_cv + w[1:2, :] * n1 + w[0:1, :] * n2
        gconv_ref[0:1, :] += jnp.sum(d_cv * p2, axis=0, keepdims=True)
        gconv_ref[1:2, :] += jnp.sum(d_cv * p1, axis=0, keepdims=True)
        gconv_ref[2:3, :] += jnp.sum(d_cv * p, axis=0, keepdims=True)
        du_ref[:, 0:d] = (d_y * cv).astype(du_ref.dtype)
        du_ref[:, d:2 * d] = (d_p * xin).astype(du_ref.dtype)
        du_ref[:, 2 * d:3 * d] = (d_p * c).astype(du_ref.dtype)
        du_ref[:, 3 * d:4 * d] = (d_ya * y * dsl).astype(du_ref.dtype)
        _emit_at_last(step, n - 1, [(gwoa_ref, gwoa_out)])

    def rev(i):
        return (n - 1 - i, 0)

    return pl.pallas_call(
        body, name="layer_a_bwd", grid=(n,),
        in_specs=[pl.BlockSpec((tm, 4 * d), rev),
                  pl.BlockSpec((8, 4 * d), lambda i: (jnp.maximum((n - 1 - i) * per8 - 1, 0), 0)),
                  pl.BlockSpec((tm, d), rev), _full_spec((8, d)), _full_spec((d, d))],
        out_specs=[pl.BlockSpec((tm, 4 * d), rev), _full_spec((d, d)), _full_spec((8, d))],
        out_shape=[jax.ShapeDtypeStruct((s, 4 * d), _MXU_DTYPE), jax.ShapeDtypeStruct((d, d), _MXU_DTYPE),
                   jax.ShapeDtypeStruct((8, d), _F32)],
        scratch_shapes=[pltpu.VMEM((8, d), _F32), pltpu.VMEM((d, d), _F32)],
        compiler_params=_params(1),
    )(u, u, dx1, conv, w_out_a)


def _grad_w_in_a(h, du, nb, grads, axes):
    s, d = h.shape
    bn = du.shape[1] // nb
    tm = min(BIG_ROW_TILE, s)
    half = nb // 2
    n = len(grads)
    steps = s // tm

    def body(h_ref, du_ref, *refs):
        gw_out, gw_ref = refs[n], refs[-1]
        start, wait = _exchange_ops(refs[:n], refs[n + 1:2 * n + 1], *refs[2 * n + 1:-1], True, axes)
        jh, i = pl.program_id(0), pl.program_id(1)
        pl.when(jnp.logical_and(jh == 0, i == 0))(start)
        _zero_at_first(i, gw_ref)
        hv = h_ref[...]
        for j in range(half):
            gw_ref[j] += _dot_tn(hv, du_ref[:, j * bn:(j + 1) * bn])
        _emit_at_last(i, steps - 1, [(gw_ref, gw_out)])
        pl.when(jnp.logical_and(jh == 1, i == steps - 1))(wait)

    outs = pl.pallas_call(
        body, name="grad_w_in_a", grid=(2, steps),
        in_specs=[pl.BlockSpec((tm, d), lambda jh, i: (i, 0)), pl.BlockSpec((tm, half * bn), lambda jh, i: (i, jh))]
        + [_ANY] * n,
        out_specs=[pl.BlockSpec((half, d, bn), lambda jh, i: (jh, 0, 0))] + [_ANY] * n,
        out_shape=[jax.ShapeDtypeStruct((nb, d, bn), _MXU_DTYPE)] + _exchange_shapes(grads, True, axes),
        scratch_shapes=_exchange_sems(n) + [pltpu.VMEM((half, d, bn), _F32)],
        compiler_params=_params(2),
    )(h, du, *grads)
    return outs[0], outs[1:]


def _input_grad(x, dx1, du, w_in, gain):
    s, d = x.shape
    f = w_in.shape[1]
    tm = min(BIG_ROW_TILE, s)

    def body(x_ref, dx1_ref, du_ref, w_ref, g_ref, dx_ref, gn_ref):
        _zero_at_first(pl.program_id(0), gn_ref)
        r, xh = _rms_stats(x_ref[...])
        d_h = _dot_nt(du_ref[...], w_ref[...])
        gn_ref[...] += jnp.sum(d_h * xh, axis=0, keepdims=True)
        dx_ref[...] = dx1_ref[...] + _rms_bwd(r, xh, d_h * g_ref[...])

    return pl.pallas_call(
        body, name="input_grad", grid=(s // tm,),
        in_specs=[_row_spec(tm, d), _row_spec(tm, d), _row_spec(tm, f), _full_spec((d, f)), _full_spec((1, d))],
        out_specs=[_row_spec(tm, d), _full_spec((1, d))],
        out_shape=[jax.ShapeDtypeStruct((s, d), _F32), jax.ShapeDtypeStruct((1, d), _F32)],
        compiler_params=_params(1),
    )(x, dx1, du, w_in, gain)


def _reduce_adamw(name, parts, w, m, v):
    rows, cols = w.shape
    n_parts = parts.shape[0]
    tr = min(256, rows)
    c1 = 1.0 - ADAM_B1 ** ADAM_STEP
    c2 = 1.0 - ADAM_B2 ** ADAM_STEP

    def body(p_ref, w_ref, m_ref, v_ref, g_ref, d_ref, nm_ref, nv_ref):
        g = p_ref[0].astype(_F32)
        for k in range(1, n_parts):
            g = g + p_ref[k].astype(_F32)
        nm = ADAM_B1 * m_ref[...] + (1.0 - ADAM_B1) * g
        nv = ADAM_B2 * v_ref[...] + (1.0 - ADAM_B2) * (g * g)
        g_ref[...] = g
        nm_ref[...] = nm
        nv_ref[...] = nv
        d_ref[...] = -ADAM_LR * ((nm / c1) / (jnp.sqrt(nv / c2) + ADAM_EPS) + ADAM_WD * w_ref[...])

    tile = _row_spec(tr, cols)
    return pl.pallas_call(
        body, name=name, grid=(rows // tr,),
        in_specs=[pl.BlockSpec((n_parts, tr, cols), lambda i: (0, i, 0)), tile, tile, tile],
        out_specs=[tile] * 4,
        out_shape=[jax.ShapeDtypeStruct((rows, cols), _F32)] * 4,
        compiler_params=_params(1),
    )(parts, w, m, v)


def _pad_rows(a, rows=8):
    return jnp.pad(a, ((0, rows - a.shape[0]), (0, 0)))


def kernel(x, norm_a, w_in_a, conv_a, w_out_a, norm_kv, w_kv, norm_b, w_in_b, w_out_b, norm_f, loss_target, m_norm_a, m_w_in_a, m_conv_a, m_w_out_a, m_norm_kv, m_w_kv, m_norm_b, m_w_in_b, m_w_out_b, m_norm_f, v_norm_a, v_w_in_a, v_conv_a, v_w_out_a, v_norm_kv, v_w_kv, v_norm_b, v_w_in_b, v_w_out_b, v_norm_f):
    x0 = x[0]
    s, d = x0.shape
    a = d // 2
    sh = d // N_DEV
    me = 4 * lax.axis_index("x") + 2 * lax.axis_index("y") + lax.axis_index("c")

    small_a = _pad_rows(jnp.concatenate([norm_a, conv_a[0]], axis=0))
    wia_g, small_g = _gather_two_level("exchange_gather", [_mx(w_in_a[0]), small_a], axes=[1, 1])
    small_f = small_g.reshape(8, d)
    gain_a = small_f[0:1]
    conv_f = _pad_rows(small_f[1:4])
    gain_kv, gain_b, gain_f = norm_kv.reshape(1, d), norm_b.reshape(1, d), norm_f.reshape(1, d)

    wia_f = wia_g.reshape(d, 4 * d)
    u, h, (woa_g, wkv_g, wib_g, wob_g) = _ln_matmul_in_a(
        x0, gain_a, wia_f, [_mx(w_out_a[0]), _mx(w_kv), _mx(w_in_b[0]), _mx(w_out_b[0])], axes=[0, 0, 0, 1])
    woa_f = woa_g.reshape(d, d)
    wkv_f = wkv_g.reshape(d, 2 * a)
    wib_f = wib_g.reshape(d, 2 * a)
    wob_f = wob_g.reshape(a, d)
    x1, kv, q, gate = _layer_a_out(u, x0, conv_f, woa_f, gain_kv, gain_b, wkv_f, wib_f)
    tq = min(ATT_TILE, s)
    idx = jnp.arange(tq)
    tri_suffix = _mx(idx[:, None] >= idx[None, :])
    tri_prefix = _mx(idx[:, None] <= idx[None, :])
    o, c_tot, stop = _attn_fwd(q, kv, tri_suffix)
    dx2, do, dgate, g_wob, g_norm_f, loss_part = _layer_b_out_loss_bwd(o, gate, x1, wob_f, gain_f, loss_target[0])

    dq, dk, dv = _attn_bwd(q, kv, do, c_tot, stop, tri_suffix, tri_prefix)
    dx1, g_wib, g_wkv, g_norm_b, g_norm_kv = _proj_bwd(x1, dx2, dq, dgate, dk, dv, wib_f, wkv_f, gain_b, gain_kv)
    du, g_woa, g_conv = _layer_a_bwd(u, dx1, conv_f, woa_f)
    dx0, g_norm_a = _input_grad(x0, dx1, du, wia_f, gain_a)

    g_wia, (p_woa, p_wkv, p_wib, p_wob) = _grad_w_in_a(
        h, du, N_DEV,
        [g_woa.reshape(N_DEV, sh, d), g_wkv.reshape(N_DEV, sh, 2 * a), g_wib.reshape(N_DEV, sh, 2 * a),
         g_wob.reshape(a, N_DEV, sh).transpose(1, 0, 2)], axes=[0, 0, 0, 0])
    p_wia = _reduce_scatter_two_level("exchange_scatter", g_wia)
    small_grads = jnp.concatenate(
        [g_norm_a, g_conv[0:3], g_norm_kv, g_norm_b, g_norm_f, jnp.pad(loss_part, ((0, 0), (0, d - LANES)))], axis=0)
    (p_small,) = _exchange("exchange_small", [small_grads], scatter=False, axes=[0])

    upd_wia = _reduce_adamw("adamw_w_in_a", p_wia, w_in_a[0], m_w_in_a[0], v_w_in_a[0])
    upd_woa = _reduce_adamw("adamw_w_out_a", p_woa, w_out_a[0], m_w_out_a[0], v_w_out_a[0])
    upd_wkv = _reduce_adamw("adamw_w_kv", p_wkv, w_kv, m_w_kv, v_w_kv)
    upd_wib = _reduce_adamw("adamw_w_in_b", p_wib, w_in_b[0], m_w_in_b[0], v_w_in_b[0])
    upd_wob = _reduce_adamw("adamw_w_out_b", p_wob, w_out_b[0], m_w_out_b[0], v_w_out_b[0])

    def rep(a1, a2, a3):
        return jnp.concatenate([jnp.zeros((4, d), _F32), a1.reshape(1, d), a2.reshape(1, d), a3.reshape(1, d),
                                jnp.zeros((1, d), _F32)], axis=0)

    upd_rep = _reduce_adamw("adamw_replicated", p_small, rep(norm_kv, norm_b, norm_f),
                            rep(m_norm_kv, m_norm_b, m_norm_f), rep(v_norm_kv, v_norm_b, v_norm_f))

    def mine(n1, cv1):
        return _pad_rows(jnp.concatenate([n1, cv1[0]], axis=0))

    p_mine = lax.dynamic_slice(p_small, (0, 0, me * sh), (N_DEV, 8, sh))
    upd_mine = _reduce_adamw("adamw_sharded_small", p_mine, mine(norm_a, conv_a), mine(m_norm_a, m_conv_a),
                             mine(v_norm_a, v_conv_a))

    loss = upd_rep[0][7, 0]
    groups = []
    for k in range(4):
        groups.append([
            upd_mine[k][0:1], upd_wia[k][None], upd_mine[k][1:4][None], upd_woa[k][None], upd_rep[k][4],
            upd_wkv[k], upd_rep[k][5:6], upd_wib[k][None], upd_wob[k][None], upd_rep[k][6]])
    return (loss, dx0[None], *groups[0], *groups[1], *groups[2], *groups[3])
```

```python
import jax
import jax.numpy as jnp
from jax import lax
from jax.experimental import pallas as pl
from jax.experimental.pallas import tpu as pltpu

_MXU_DTYPE = jnp.bfloat16
_F32 = jnp.float32

RMS_EPS = 1e-6
HEAD_DIM = 64
LANES = 128
N_DEV = 8
ATT_TILE = 256
ROW_TILE = 256
BIG_ROW_TILE = 512
VMEM_LIMIT = 56 * 1024 * 1024
SKIP_LOG = -110.0

ADAM_LR = 0.001
ADAM_B1 = 0.9
ADAM_B2 = 0.999
ADAM_EPS = 1e-08
ADAM_WD = 0.01
ADAM_STEP = 10

_NT = (((1,), (1,)), ((), ()))
_TN = (((0,), (0,)), ((), ()))


def _params(n_grid):
    return pltpu.CompilerParams(dimension_semantics=("arbitrary",) * n_grid, vmem_limit_bytes=VMEM_LIMIT)


def _dot(a, b):
    return jnp.dot(a, b, preferred_element_type=_F32)


def _dot_nt(a, b):
    return lax.dot_general(a, b, _NT, preferred_element_type=_F32)


def _dot_tn(a, b):
    return lax.dot_general(a, b, _TN, preferred_element_type=_F32)


def _mx(a):
    return a.astype(_MXU_DTYPE)


def _sigmoid(a):
    return 1.0 / (1.0 + jnp.exp(-a))


def _rms_stats(xv):
    r = lax.rsqrt(jnp.mean(xv * xv, axis=-1, keepdims=True) + RMS_EPS)
    return r, xv * r


def _rms_bwd(r, xh, dyg):
    return r * (dyg - xh * jnp.mean(dyg * xh, axis=-1, keepdims=True))


def _row_spec(tm, width):
    return pl.BlockSpec((tm, width), lambda i: (i, 0))


def _full_spec(shape):
    zeros = (0,) * len(shape)
    return pl.BlockSpec(shape, lambda *_: zeros)


def _zero_at_first(step, *refs):
    @pl.when(step == 0)
    def _():
        for ref in refs:
            ref[...] = jnp.zeros_like(ref)


def _emit_at_last(step, last, pairs):
    @pl.when(step == last)
    def _():
        for acc, out in pairs:
            out[...] = acc[...].astype(out.dtype)


def _slot(ref, idx, axis):
    if axis == 0:
        return ref.at[idx]
    width = ref.shape[1] // N_DEV
    return ref.at[:, pl.ds(pl.multiple_of(idx * width, width), width)]


def _exchange_ops(ins, outs, send_sems, recv_sems, local_sems, scatter, axes):
    n = len(ins)
    x, y, c = lax.axis_index("x"), lax.axis_index("y"), lax.axis_index("c")
    me = 4 * x + 2 * y + c

    def remote(t, m, landed):
        px = 1 - x if m & 4 else x
        py = 1 - y if m & 2 else y
        pc = 1 - c if m & 1 else c
        idx = 4 * px + 2 * py + pc
        k = t * (N_DEV - 1) + m - 1
        return pltpu.make_async_remote_copy(
            src_ref=_slot(ins[t], idx, axes[t]) if scatter else ins[t],
            dst_ref=_slot(outs[t], idx if landed else me, axes[t]),
            send_sem=send_sems.at[k], recv_sem=recv_sems.at[k],
            device_id=(px, py, pc), device_id_type=pl.DeviceIdType.MESH)

    def local(t):
        return pltpu.make_async_copy(_slot(ins[t], me, axes[t]) if scatter else ins[t], _slot(outs[t], me, axes[t]),
                                     local_sems.at[t])

    def start():
        for t in range(n):
            local(t).start()
        for m in range(1, N_DEV):
            for t in range(n):
                remote(t, m, False).start()

    def wait():
        for m in range(1, N_DEV):
            for t in range(n):
                remote(t, m, True).wait_recv()
        for m in range(1, N_DEV):
            for t in range(n):
                remote(t, m, False).wait_send()
        for t in range(n):
            local(t).wait()

    return start, wait


def _exchange_shapes(arrays, scatter, axes):
    if scatter:
        return [jax.ShapeDtypeStruct(a.shape, a.dtype) for a in arrays]
    return [jax.ShapeDtypeStruct((N_DEV,) + a.shape if ax == 0 else (a.shape[0], N_DEV * a.shape[1]), a.dtype)
            for a, ax in zip(arrays, axes)]


def _exchange_sems(n):
    return [pltpu.SemaphoreType.DMA((n * (N_DEV - 1),)), pltpu.SemaphoreType.DMA((n * (N_DEV - 1),)),
            pltpu.SemaphoreType.DMA((n,))]


_ANY = pl.BlockSpec(memory_space=pl.ANY)


def _exchange(name, arrays, scatter, axes):
    n = len(arrays)

    def body(*refs):
        start, wait = _exchange_ops(refs[:n], refs[n:2 * n], *refs[2 * n:], scatter, axes)
        start()
        wait()

    return pl.pallas_call(
        body, name=name, in_specs=[_ANY] * n, out_specs=[_ANY] * n,
        out_shape=_exchange_shapes(arrays, scatter, axes), scratch_shapes=_exchange_sems(n),
    )(*arrays)


def _gather_two_level(name, arrays, axes):
    n = len(arrays)

    def body(*refs):
        ins, outs = refs[:n], refs[n:2 * n]
        send_sems, recv_sems, local_sems = refs[2 * n:]
        x, y, c = lax.axis_index("x"), lax.axis_index("y"), lax.axis_index("c")
        me, sibling = (x, y, c), (x, y, 1 - c)
        chips = [(1 - x, y), (x, 1 - y), (1 - x, 1 - y)]

        def rows(t, dev):
            return _slot(outs[t], 4 * dev[0] + 2 * dev[1] + dev[2], axes[t])

        def copy(t, k, block, to, src=None):
            return pltpu.make_async_remote_copy(
                src_ref=rows(t, block) if src is None else src, dst_ref=rows(t, block),
                send_sem=send_sems.at[t * (N_DEV - 1) + k], recv_sem=recv_sems.at[t * (N_DEV - 1) + k],
                device_id=to, device_id_type=pl.DeviceIdType.MESH)

        sent = []
        for t in range(n):
            pltpu.make_async_copy(ins[t], rows(t, me), local_sems.at[t]).start()
            sent.append(copy(t, 0, me, sibling, src=ins[t]))
            sent += [copy(t, 1 + j, me, (*chip, c), src=ins[t]) for j, chip in enumerate(chips)]
        for cp in sent:
            cp.start()
        for j, chip in enumerate(chips):
            for t in range(n):
                copy(t, 1 + j, (*chip, c), me).wait_recv()
                passed = copy(t, 4 + j, (*chip, c), sibling)
                passed.start()
                sent.append(passed)
        for t in range(n):
            copy(t, 0, sibling, me).wait_recv()
            for j, chip in enumerate(chips):
                copy(t, 4 + j, (*chip, 1 - c), me).wait_recv()
        for cp in sent:
            cp.wait_send()
        for t in range(n):
            pltpu.make_async_copy(ins[t], rows(t, me), local_sems.at[t]).wait()

    return pl.pallas_call(
        body, name=name, in_specs=[_ANY] * n, out_specs=[_ANY] * n,
        out_shape=_exchange_shapes(arrays, False, axes), scratch_shapes=_exchange_sems(n),
    )(*arrays)


def _reduce_scatter_two_level(name, g):
    n_chip = N_DEV // 2
    _, r, cc = g.shape
    chunk = min(256, r)

    def body(g_ref, out_ref, mine, theirs, total, load_sems, pair_send, pair_recv, chip_send, chip_recv, keep_sem):
        x, y, c = lax.axis_index("x"), lax.axis_index("y"), lax.axis_index("c")
        my_chip = 2 * x + y

        def load(q):
            return pltpu.make_async_copy(g_ref.at[2 * q + c], mine.at[q], load_sems.at[q])

        def swap(q):
            return pltpu.make_async_remote_copy(
                src_ref=g_ref.at[2 * q + 1 - c], dst_ref=theirs.at[q], send_sem=pair_send.at[q],
                recv_sem=pair_recv.at[q], device_id=(x, y, 1 - c), device_id_type=pl.DeviceIdType.MESH)

        for q in range(n_chip):
            load(q).start()
            swap(q).start()
        for q in range(n_chip):
            load(q).wait()
            swap(q).wait_recv()

            def add(i, carry, q=q):
                rows = pl.ds(pl.multiple_of(i * chunk, chunk), chunk)
                total[q, rows, :] = (mine[q, rows, :].astype(_F32) + theirs[q, rows, :].astype(_F32)).astype(total.dtype)
                return carry
            lax.fori_loop(0, r // chunk, add, 0)

        def send(k, landed):
            px = 1 - x if k in (0, 2) else x
            py = 1 - y if k in (1, 2) else y
            peer_chip = 2 * px + py
            return pltpu.make_async_remote_copy(
                src_ref=total.at[peer_chip], dst_ref=out_ref.at[peer_chip if landed else my_chip],
                send_sem=chip_send.at[k], recv_sem=chip_recv.at[k],
                device_id=(px, py, c), device_id_type=pl.DeviceIdType.MESH)

        keep = pltpu.make_async_copy(total.at[my_chip], out_ref.at[my_chip], keep_sem)
        keep.start()
        for k in range(n_chip - 1):
            send(k, False).start()
        for k in range(n_chip - 1):
            send(k, True).wait_recv()
        for k in range(n_chip - 1):
            send(k, False).wait_send()
        for q in range(n_chip):
            swap(q).wait_send()
        keep.wait()

    slab = pltpu.VMEM((n_chip, r, cc), g.dtype)
    return pl.pallas_call(
        body, name=name, in_specs=[_ANY], out_specs=_ANY,
        out_shape=jax.ShapeDtypeStruct((n_chip, r, cc), g.dtype),
        scratch_shapes=[slab, slab, slab, pltpu.SemaphoreType.DMA((n_chip,)), pltpu.SemaphoreType.DMA((n_chip,)),
                        pltpu.SemaphoreType.DMA((n_chip,)), pltpu.SemaphoreType.DMA((n_chip - 1,)),
                        pltpu.SemaphoreType.DMA((n_chip - 1,)), pltpu.SemaphoreType.DMA],
        compiler_params=pltpu.CompilerParams(vmem_limit_bytes=VMEM_LIMIT),
    )(g)


def _ln_matmul_in_a(x, gain, w_in, shards, axes):
    s, d = x.shape
    f = w_in.shape[1]
    tm = min(BIG_ROW_TILE, s)
    n = len(shards)
    last = s // tm - 1

    def body(x_ref, g_ref, w_ref, *refs):
        u_ref, h_ref = refs[n:n + 2]
        start, wait = _exchange_ops(refs[:n], refs[n + 2:2 * n + 2], *refs[2 * n + 2:], False, axes)
        pl.when(pl.program_id(0) == 0)(start)
        _, xh = _rms_stats(x_ref[...])
        h = _mx(xh * g_ref[...])
        h_ref[...] = h
        u_ref[...] = _dot(h, w_ref[...])
        pl.when(pl.program_id(0) == last)(wait)

    outs = pl.pallas_call(
        body, name="ln_matmul_in_a", grid=(s // tm,),
        in_specs=[_row_spec(tm, d), _full_spec((1, d)), _full_spec((d, f))] + [_ANY] * n,
        out_specs=[_row_spec(tm, f), _row_spec(tm, d)] + [_ANY] * n,
        out_shape=[jax.ShapeDtypeStruct((s, f), _F32), jax.ShapeDtypeStruct((s, d), _MXU_DTYPE)]
        + _exchange_shapes(shards, False, axes),
        scratch_shapes=_exchange_sems(n),
        compiler_params=_params(1),
    )(x, gain, w_in, *shards)
    return outs[0], outs[1], outs[2:]


def _shift_rows_down(p, before, tm):
    row = lax.broadcasted_iota(jnp.int32, (8, p.shape[1]), 0)
    r1, r2 = pltpu.roll(p, 1, 0), pltpu.roll(p, 2, 0)
    top1 = jnp.where(row == 0, before[7:8, :], r1[0:8, :])
    top2 = jnp.where(row == 0, before[6:7, :], jnp.where(row == 1, before[7:8, :], r2[0:8, :]))
    return jnp.concatenate([top1, r1[8:, :]], axis=0), jnp.concatenate([top2, r2[8:, :]], axis=0)


def _shift_rows_up(p, after, tm):
    row = lax.broadcasted_iota(jnp.int32, (8, p.shape[1]), 0)
    r1, r2 = pltpu.roll(p, tm - 1, 0), pltpu.roll(p, tm - 2, 0)
    end1 = jnp.where(row == 7, after[0:1, :], r1[tm - 8:, :])
    end2 = jnp.where(row == 6, after[0:1, :], jnp.where(row == 7, after[1:2, :], r2[tm - 8:, :]))
    return jnp.concatenate([r1[:tm - 8, :], end1], axis=0), jnp.concatenate([r2[:tm - 8, :], end2], axis=0)


def _layer_a_out(u, x, conv, w_out_a, gain_kv, gain_b, w_kv, w_in_b):
    s, d = x.shape
    a = w_kv.shape[1] // 2
    tm = min(BIG_ROW_TILE, s)
    scale = HEAD_DIM ** -0.5

    def body(u_ref, x_ref, conv_ref, woa_ref, gkv_ref, gb_ref, wkv_ref, wib_ref,
             x1_ref, kv_ref, q_ref, gate_ref, halo_ref):
        _zero_at_first(pl.program_id(0), halo_ref)
        b, c, xin, g = (u_ref[:, k * d:(k + 1) * d] for k in range(4))
        p = c * xin
        p1, p2 = _shift_rows_down(p, halo_ref[...], tm)
        halo_ref[...] = p[tm - 8:tm, :]
        w = conv_ref[...]
        cv = w[0:1, :] * p2 + w[1:2, :] * p1 + w[2:3, :] * p
        ya = (b * cv) * (g * _sigmoid(g))
        x1 = x_ref[...] + _dot(_mx(ya), woa_ref[...])
        x1_ref[...] = x1
        _, xh = _rms_stats(x1)
        kv_ref[...] = _dot(_mx(xh * gkv_ref[...]), wkv_ref[...]).astype(kv_ref.dtype)
        qg = _dot(_mx(xh * gb_ref[...]), wib_ref[...])
        q_ref[...] = (qg[:, :a] * scale).astype(q_ref.dtype)
        gate_ref[...] = qg[:, a:]

    return pl.pallas_call(
        body, name="layer_a_out", grid=(s // tm,),
        in_specs=[_row_spec(tm, 4 * d), _row_spec(tm, d), _full_spec((8, d)), _full_spec((d, d)),
                  _full_spec((1, d)), _full_spec((1, d)), _full_spec((d, 2 * a)), _full_spec((d, 2 * a))],
        out_specs=[_row_spec(tm, d), _row_spec(tm, 2 * a), _row_spec(tm, a), _row_spec(tm, a)],
        out_shape=[jax.ShapeDtypeStruct((s, d), _F32), jax.ShapeDtypeStruct((s, 2 * a), _MXU_DTYPE),
                   jax.ShapeDtypeStruct((s, a), _MXU_DTYPE), jax.ShapeDtypeStruct((s, a), _F32)],
        scratch_shapes=[pltpu.VMEM((8, d), _F32)],
        compiler_params=_params(1),
    )(u, x, conv, w_out_a, gain_kv, gain_b, w_kv, w_in_b)


def _neg_softplus(z):
    return -(jnp.maximum(z, 0.0) + jnp.log(1.0 + jnp.exp(-jnp.abs(z))))


def _scan_dot(val, tri):
    return _dot(_mx(val), tri)


def _head_masks():
    lane = lax.broadcasted_iota(jnp.int32, (1, LANES), 1)
    return [lane < HEAD_DIM, lane >= HEAD_DIM]


def _stacked_causal(tq):
    row = lax.broadcasted_iota(jnp.int32, (2 * tq, tq), 0)
    col = lax.broadcasted_iota(jnp.int32, (2 * tq, tq), 1)
    return col < jnp.where(row >= tq, row - tq, row)


def _attn_fwd(q, kv, tri_suffix):
    s, a = q.shape
    n_hp = a // LANES
    tq = min(ATT_TILE, s)
    nq = s // tq

    def body(q_ref, k_ref, v_ref, tri_ref, o_ref, c_ref, stop_ref):
        hp, i = pl.program_id(0), pl.program_id(1)
        qv = q_ref[...]
        tri = tri_ref[...]
        causal = _stacked_causal(tq)
        qs = jnp.concatenate([jnp.where(hmask, qv, jnp.zeros_like(qv)) for hmask in _head_masks()], axis=0)

        def block(j, c, acc, diag=False, live=None):
            off = pl.multiple_of(j * tq, tq)
            kb = k_ref[pl.ds(off, tq), :]
            vb = v_ref[pl.ds(off, tq), :]
            z = _dot_nt(qs, kb)
            lg = _neg_softplus(z)
            if diag:
                lg = jnp.where(causal, lg, 0.0)
            tot = jnp.sum(lg, axis=1, keepdims=True)
            c_in = c
            if live is not None:
                c_in = c + (live - 1.0) * 1e30
                tot = tot * live
            w = jnp.exp(z + _scan_dot(lg, tri) + c_in)
            if diag:
                w = jnp.where(causal, w, 0.0)
            return c + tot, acc + _dot(_mx(w), vb)

        c, acc = block(i, jnp.zeros((2 * tq, 1), _F32), jnp.zeros((2 * tq, LANES), _F32), diag=True)
        c, acc = block(jnp.maximum(i - 1, 0), c, acc, live=jnp.where(i >= 1, 1.0, 0.0))

        def cond(carry):
            return jnp.logical_and(carry[0] >= 0, jnp.max(carry[1]) > SKIP_LOG)

        def step(carry):
            c, acc = block(carry[0], carry[1], carry[2])
            return carry[0] - 1, c, acc

        j, c, acc = lax.while_loop(cond, step, (i - 2, c, acc))
        stop_ref[hp, i] = jnp.maximum(jnp.minimum(j + 1, i - 1), 0)
        first = _head_masks()[0]
        o_ref[...] = jnp.where(first, acc[:tq], acc[tq:])
        c_ref[...] = jnp.where(first, c[:tq], c[tq:])

    q_spec = pl.BlockSpec((tq, LANES), lambda h, i: (i, h))
    return pl.pallas_call(
        body, name="attn_fwd", grid=(n_hp, nq),
        in_specs=[q_spec, pl.BlockSpec((s, LANES), lambda h, i: (0, h)),
                  pl.BlockSpec((s, LANES), lambda h, i: (0, n_hp + h)), _full_spec((tq, tq))],
        out_specs=[q_spec, q_spec, pl.BlockSpec(memory_space=pltpu.SMEM)],
        out_shape=[jax.ShapeDtypeStruct((s, a), _F32), jax.ShapeDtypeStruct((s, a), _F32),
                   jax.ShapeDtypeStruct((n_hp, nq), jnp.int32)],
        compiler_params=_params(2),
    )(q, kv, kv, tri_suffix)


def _silu_and_grad(g):
    sg = _sigmoid(g)
    return g * sg, sg * (1.0 + g * (1.0 - sg))


def _layer_b_out_loss_bwd(o, gate, x1, w_out_b, gain_f, target):
    s, d = x1.shape
    a = o.shape[1]
    tm = min(BIG_ROW_TILE, s)
    last = s // tm - 1

    def body(o_ref, gate_ref, x1_ref, wob_ref, gf_ref, tgt_ref,
             dx2_ref, do_ref, dgate_ref, gw_ref, gnf_ref, loss_ref, gw_acc):
        step = pl.program_id(0)
        _zero_at_first(step, gw_acc, gnf_ref, loss_ref)
        sl, dsl = _silu_and_grad(gate_ref[...])
        ov = o_ref[...]
        ob = _mx(ov * sl)
        wob = wob_ref[...]
        x2 = x1_ref[...] + _dot(ob, wob)
        r, xh = _rms_stats(x2)
        gf = gf_ref[...]
        err = xh * gf - tgt_ref[...]
        part = jnp.sum(jnp.sum(err * err, axis=1, keepdims=True), axis=0, keepdims=True)
        loss_ref[...] += part * (0.5 / d)
        dy = err * (1.0 / d)
        gnf_ref[...] += jnp.sum(dy * xh, axis=0, keepdims=True)
        dx2 = _rms_bwd(r, xh, dy * gf)
        dx2_ref[...] = dx2
        dxb = _mx(dx2)
        d_ob = _dot_nt(dxb, wob)
        gw_acc[...] += _dot_tn(ob, dxb)
        do_ref[...] = (d_ob * sl).astype(do_ref.dtype)
        dgate_ref[...] = (d_ob * ov * dsl).astype(dgate_ref.dtype)
        _emit_at_last(step, last, [(gw_acc, gw_ref)])

    return pl.pallas_call(
        body, name="layer_b_out_loss_bwd", grid=(s // tm,),
        in_specs=[_row_spec(tm, a), _row_spec(tm, a), _row_spec(tm, d), _full_spec((a, d)), _full_spec((1, d)),
                  _row_spec(tm, d)],
        out_specs=[_row_spec(tm, d), _row_spec(tm, a), _row_spec(tm, a), _full_spec((a, d)), _full_spec((1, d)),
                   _full_spec((1, LANES))],
        out_shape=[jax.ShapeDtypeStruct((s, d), _F32), jax.ShapeDtypeStruct((s, a), _MXU_DTYPE),
                   jax.ShapeDtypeStruct((s, a), _MXU_DTYPE), jax.ShapeDtypeStruct((a, d), _MXU_DTYPE),
                   jax.ShapeDtypeStruct((1, d), _F32), jax.ShapeDtypeStruct((1, LANES), _F32)],
        scratch_shapes=[pltpu.VMEM((a, d), _F32)],
        compiler_params=_params(1),
    )(o, gate, x1, w_out_b, gain_f, target)


def _attn_bwd(q, kv, do, c_tot, stop, tri_suffix, tri_prefix):
    s, a = q.shape
    n_hp = a // LANES
    tq = min(ATT_TILE, s)
    nq = s // tq
    scale = HEAD_DIM ** -0.5
    chunk = min(1024, s)

    def body(stop_ref, q_ref, do_ref, c_ref, k_ref, v_ref, ts_ref, tp_ref, dq_ref, dk_hbm, dv_hbm,
             dk_acc, dv_acc, stage, sem):
        hp, i = pl.program_id(0), pl.program_id(1)
        _zero_at_first(i, dk_acc, dv_acc)
        qv, dov, cv = q_ref[...], do_ref[...], c_ref[...]
        ts, tp = ts_ref[...], tp_ref[...]
        causal = _stacked_causal(tq)
        masks = _head_masks()
        qs = jnp.concatenate([jnp.where(hmask, qv, jnp.zeros_like(qv)) for hmask in masks], axis=0)
        dos = jnp.concatenate([jnp.where(hmask, dov, jnp.zeros_like(dov)) for hmask in masks], axis=0)
        cs = jnp.concatenate([cv[:, 0:1], cv[:, HEAD_DIM:HEAD_DIM + 1]], axis=0)

        def block(j, st, diag=False, live=None):
            asc, pre, dq = st
            rows = pl.ds(pl.multiple_of(j * tq, tq), tq)
            kb = k_ref[rows, :]
            vb = v_ref[rows, :]
            z = _dot_nt(qs, kb)
            lg = _neg_softplus(z)
            sig = jnp.exp(z + lg)
            if diag:
                lg = jnp.where(causal, lg, 0.0)
            tot = jnp.sum(lg, axis=1, keepdims=True)
            newer = jnp.zeros_like(tot) if diag else cs - asc - tot
            if live is not None:
                newer = newer + (live - 1.0) * 1e30
                tot = tot * live
            wgt = jnp.exp(z + _scan_dot(lg, ts) + newer)
            if diag:
                wgt = jnp.where(causal, wgt, 0.0)
            g = wgt * _dot_nt(dos, vb)
            dz = g - sig * (_scan_dot(g, tp) + pre)
            if diag:
                dz = jnp.where(causal, dz, 0.0)
            dzb = _mx(dz)
            dk_acc[rows, :] += _dot_tn(dzb, qs)
            dv_acc[rows, :] += _dot_tn(_mx(wgt), dos)
            return asc + tot, pre + jnp.sum(g, axis=1, keepdims=True), dq + _dot(dzb, kb)

        first = jnp.clip(stop_ref[hp, i], 0, jnp.maximum(i - 1, 0))
        st = (jnp.zeros((2 * tq, 1), _F32), jnp.zeros((2 * tq, 1), _F32), jnp.zeros((2 * tq, LANES), _F32))
        st = lax.fori_loop(first, i - 1, block, st)
        st = block(jnp.maximum(i - 1, 0), st, live=jnp.where(i >= 1, 1.0, 0.0))
        dq = block(i, st, diag=True)[2]
        dq_ref[...] = (jnp.where(masks[0], dq[:tq], dq[tq:]) * scale).astype(dq_ref.dtype)

        @pl.when(i == nq - 1)
        def _():
            cols = pl.ds(pl.multiple_of(hp * LANES, LANES), LANES)
            for acc, out in ((dk_acc, dk_hbm), (dv_acc, dv_hbm)):
                def cast(n, carry, acc=acc):
                    rows = pl.ds(pl.multiple_of(n * chunk, chunk), chunk)
                    stage[rows, :] = acc[rows, :].astype(stage.dtype)
                    return carry
                lax.fori_loop(0, s // chunk, cast, 0)
                cp = pltpu.make_async_copy(stage, out.at[:, cols], sem)
                cp.start()
                cp.wait()

    q_spec = pl.BlockSpec((tq, LANES), lambda h, i, *_: (i, h))
    grid_spec = pltpu.PrefetchScalarGridSpec(
        num_scalar_prefetch=1, grid=(n_hp, nq),
        in_specs=[q_spec, q_spec, q_spec, pl.BlockSpec((s, LANES), lambda h, i, *_: (0, h)),
                  pl.BlockSpec((s, LANES), lambda h, i, *_: (0, n_hp + h)),
                  pl.BlockSpec((tq, tq), lambda h, i, *_: (0, 0)), pl.BlockSpec((tq, tq), lambda h, i, *_: (0, 0))],
        out_specs=[q_spec, pl.BlockSpec(memory_space=pl.ANY), pl.BlockSpec(memory_space=pl.ANY)],
        scratch_shapes=[pltpu.VMEM((s, LANES), _F32), pltpu.VMEM((s, LANES), _F32), pltpu.VMEM((s, LANES), _MXU_DTYPE),
                        pltpu.SemaphoreType.DMA],
    )
    return pl.pallas_call(
        body, name="attn_bwd", grid_spec=grid_spec,
        out_shape=[jax.ShapeDtypeStruct((s, a), _MXU_DTYPE)] * 3,
        compiler_params=_params(2),
    )(stop, q, do, c_tot, kv, kv, tri_suffix, tri_prefix)


def _proj_bwd(x1, dx2, dq, dgate, dk, dv, w_in_b, w_kv, gain_b, gain_kv):
    s, d = x1.shape
    a = dq.shape[1]
    tm = min(BIG_ROW_TILE, s)

    def body(x1_ref, dx2_ref, dq_ref, dgate_ref, dk_ref, dv_ref, wib_ref, wkv_ref, gb_ref, gkv_ref,
             dx1_ref, gwib_out, gwkv_out, gnb_ref, gnkv_ref, gwib_ref, gwkv_ref):
        step = pl.program_id(0)
        _zero_at_first(step, gwib_ref, gwkv_ref, gnb_ref, gnkv_ref)
        r, xh = _rms_stats(x1_ref[...])
        gb, gkv = gb_ref[...], gkv_ref[...]
        hb, hk = _mx(xh * gb), _mx(xh * gkv)
        dq, dgate, dk, dv = dq_ref[...], dgate_ref[...], dk_ref[...], dv_ref[...]
        gwib_ref[:, :a] += _dot_tn(hb, dq)
        gwib_ref[:, a:] += _dot_tn(hb, dgate)
        gwkv_ref[:, :a] += _dot_tn(hk, dk)
        gwkv_ref[:, a:] += _dot_tn(hk, dv)
        d_hb = _dot_nt(dq, wib_ref[:, :a]) + _dot_nt(dgate, wib_ref[:, a:])
        d_hk = _dot_nt(dk, wkv_ref[:, :a]) + _dot_nt(dv, wkv_ref[:, a:])
        gnb_ref[...] += jnp.sum(d_hb * xh, axis=0, keepdims=True)
        gnkv_ref[...] += jnp.sum(d_hk * xh, axis=0, keepdims=True)
        dx1_ref[...] = dx2_ref[...] + _rms_bwd(r, xh, d_hb * gb + d_hk * gkv)
        _emit_at_last(step, s // tm - 1, [(gwib_ref, gwib_out), (gwkv_ref, gwkv_out)])

    return pl.pallas_call(
        body, name="proj_bwd", grid=(s // tm,),
        in_specs=[_row_spec(tm, d), _row_spec(tm, d)] + [_row_spec(tm, a)] * 4
        + [_full_spec((d, 2 * a)), _full_spec((d, 2 * a)), _full_spec((1, d)), _full_spec((1, d))],
        out_specs=[_row_spec(tm, d), _full_spec((d, 2 * a)), _full_spec((d, 2 * a)), _full_spec((1, d)),
                   _full_spec((1, d))],
        out_shape=[jax.ShapeDtypeStruct((s, d), _F32), jax.ShapeDtypeStruct((d, 2 * a), _MXU_DTYPE),
                   jax.ShapeDtypeStruct((d, 2 * a), _MXU_DTYPE), jax.ShapeDtypeStruct((1, d), _F32),
                   jax.ShapeDtypeStruct((1, d), _F32)],
        scratch_shapes=[pltpu.VMEM((d, 2 * a), _F32), pltpu.VMEM((d, 2 * a), _F32)],
        compiler_params=_params(1),
    )(x1, dx2, dq, dgate, dk, dv, w_in_b, w_kv, gain_b, gain_kv)


def _layer_a_bwd(u, dx1, conv, w_out_a):
    s, d = dx1.shape
    tm = min(BIG_ROW_TILE, s)
    n = s // tm
    per8 = tm // 8

    def body(u_ref, uprev_ref, dx1_ref, conv_ref, woa_ref, du_ref, gwoa_out, gconv_ref, halo_ref, gwoa_ref):
        step = pl.program_id(0)
        _zero_at_first(step, gwoa_ref, gconv_ref, halo_ref)
        b, c, xin, g = (u_ref[:, k * d:(k + 1) * d] for k in range(4))
        p = c * xin
        before = uprev_ref[:, d:2 * d] * uprev_ref[:, 2 * d:3 * d]
        before = jnp.where(step == n - 1, jnp.zeros_like(before), before)
        p1, p2 = _shift_rows_down(p, before, tm)
        w = conv_ref[...]
        cv = w[0:1, :] * p2 + w[1:2, :] * p1 + w[2:3, :] * p
        sl, dsl = _silu_and_grad(g)
        y = b * cv
        dxb = _mx(dx1_ref[...])
        gwoa_ref[...] += _dot_tn(_mx(y * sl), dxb)
        d_ya = _dot_nt(dxb, woa_ref[...])
        d_y = d_ya * sl
        d_cv = d_y * b
        n1, n2 = _shift_rows_up(d_cv, halo_ref[...], tm)
        halo_ref[...] = d_cv[0:8, :]
        d_p = w[2:3, :] * d_cv + w[1:2, :] * n1 + w[0:1, :] * n2
        gconv_ref[0:1, :] += jnp.sum(d_cv * p2, axis=0, keepdims=True)
        gconv_ref[1:2, :] += jnp.sum(d_cv * p1, axis=0, keepdims=True)
        gconv_ref[2:3, :] += jnp.sum(d_cv * p, axis=0, keepdims=True)
        du_ref[:, 0:d] = (d_y * cv).astype(du_ref.dtype)
        du_ref[:, d:2 * d] = (d_p * xin).astype(du_ref.dtype)
        du_ref[:, 2 * d:3 * d] = (d_p * c).astype(du_ref.dtype)
        du_ref[:, 3 * d:4 * d] = (d_ya * y * dsl).astype(du_ref.dtype)
        _emit_at_last(step, n - 1, [(gwoa_ref, gwoa_out)])

    def rev(i):
        return (n - 1 - i, 0)

    return pl.pallas_call(
        body, name="layer_a_bwd", grid=(n,),
        in_specs=[pl.BlockSpec((tm, 4 * d), rev),
                  pl.BlockSpec((8, 4 * d), lambda i: (jnp.maximum((n - 1 - i) * per8 - 1, 0), 0)),
                  pl.BlockSpec((tm, d), rev), _full_spec((8, d)), _full_spec((d, d))],
        out_specs=[pl.BlockSpec((tm, 4 * d), rev), _full_spec((d, d)), _full_spec((8, d))],
        out_shape=[jax.ShapeDtypeStruct((s, 4 * d), _MXU_DTYPE), jax.ShapeDtypeStruct((d, d), _MXU_DTYPE),
                   jax.ShapeDtypeStruct((8, d), _F32)],
        scratch_shapes=[pltpu.VMEM((8, d), _F32), pltpu.VMEM((d, d), _F32)],
        compiler_params=_params(1),
    )(u, u, dx1, conv, w_out_a)


def _grad_w_in_a(h, du, nb, grads, axes):
    s, d = h.shape
    bn = du.shape[1] // nb
    tm = min(4 * BIG_ROW_TILE, s)
    half = nb // 2
    n = len(grads)
    steps = s // tm

    def body(h_ref, du_ref, *refs):
        gw_out, gw_ref = refs[n], refs[-1]
        start, wait = _exchange_ops(refs[:n], refs[n + 1:2 * n + 1], *refs[2 * n + 1:-1], True, axes)
        jh, i = pl.program_id(0), pl.program_id(1)
        pl.when(jnp.logical_and(jh == 0, i == 0))(start)
        _zero_at_first(i, gw_ref)
        hv = h_ref[...]
        for j in range(half):
            gw_ref[j] += _dot_tn(hv, du_ref[:, j * bn:(j + 1) * bn])
        _emit_at_last(i, steps - 1, [(gw_ref, gw_out)])
        pl.when(jnp.logical_and(jh == 1, i == steps - 1))(wait)

    outs = pl.pallas_call(
        body, name="grad_w_in_a", grid=(2, steps),
        in_specs=[pl.BlockSpec((tm, d), lambda jh, i: (i, 0)), pl.BlockSpec((tm, half * bn), lambda jh, i: (i, jh))]
        + [_ANY] * n,
        out_specs=[pl.BlockSpec((half, d, bn), lambda jh, i: (jh, 0, 0))] + [_ANY] * n,
        out_shape=[jax.ShapeDtypeStruct((nb, d, bn), _MXU_DTYPE)] + _exchange_shapes(grads, True, axes),
        scratch_shapes=_exchange_sems(n) + [pltpu.VMEM((half, d, bn), _F32)],
        compiler_params=_params(2),
    )(h, du, *grads)
    return outs[0], outs[1:]


def _input_grad(x, dx1, du, w_in, gain):
    s, d = x.shape
    f = w_in.shape[1]
    tm = min(BIG_ROW_TILE, s)

    def body(x_ref, dx1_ref, du_ref, w_ref, g_ref, dx_ref, gn_ref):
        _zero_at_first(pl.program_id(0), gn_ref)
        r, xh = _rms_stats(x_ref[...])
        d_h = _dot_nt(du_ref[...], w_ref[...])
        gn_ref[...] += jnp.sum(d_h * xh, axis=0, keepdims=True)
        dx_ref[...] = dx1_ref[...] + _rms_bwd(r, xh, d_h * g_ref[...])

    return pl.pallas_call(
        body, name="input_grad", grid=(s // tm,),
        in_specs=[_row_spec(tm, d), _row_spec(tm, d), _row_spec(tm, f), _full_spec((d, f)), _full_spec((1, d))],
        out_specs=[_row_spec(tm, d), _full_spec((1, d))],
        out_shape=[jax.ShapeDtypeStruct((s, d), _F32), jax.ShapeDtypeStruct((1, d), _F32)],
        compiler_params=_params(1),
    )(x, dx1, du, w_in, gain)


def _reduce_adamw(name, parts, w, m, v):
    rows, cols = w.shape
    n_parts = parts.shape[0]
    tr = min(256, rows)
    c1 = 1.0 - ADAM_B1 ** ADAM_STEP
    c2 = 1.0 - ADAM_B2 ** ADAM_STEP

    def body(p_ref, w_ref, m_ref, v_ref, g_ref, d_ref, nm_ref, nv_ref):
        g = p_ref[0].astype(_F32)
        for k in range(1, n_parts):
            g = g + p_ref[k].astype(_F32)
        nm = ADAM_B1 * m_ref[...] + (1.0 - ADAM_B1) * g
        nv = ADAM_B2 * v_ref[...] + (1.0 - ADAM_B2) * (g * g)
        g_ref[...] = g
        nm_ref[...] = nm
        nv_ref[...] = nv
        d_ref[...] = -ADAM_LR * ((nm / c1) / (jnp.sqrt(nv / c2) + ADAM_EPS) + ADAM_WD * w_ref[...])

    tile = _row_spec(tr, cols)
    return pl.pallas_call(
        body, name=name, grid=(rows // tr,),
        in_specs=[pl.BlockSpec((n_parts, tr, cols), lambda i: (0, i, 0)), tile, tile, tile],
        out_specs=[tile] * 4,
        out_shape=[jax.ShapeDtypeStruct((rows, cols), _F32)] * 4,
        compiler_params=_params(1),
    )(parts, w, m, v)


def _pad_rows(a, rows=8):
    return jnp.pad(a, ((0, rows - a.shape[0]), (0, 0)))


def kernel(x, norm_a, w_in_a, conv_a, w_out_a, norm_kv, w_kv, norm_b, w_in_b, w_out_b, norm_f, loss_target, m_norm_a, m_w_in_a, m_conv_a, m_w_out_a, m_norm_kv, m_w_kv, m_norm_b, m_w_in_b, m_w_out_b, m_norm_f, v_norm_a, v_w_in_a, v_conv_a, v_w_out_a, v_norm_kv, v_w_kv, v_norm_b, v_w_in_b, v_w_out_b, v_norm_f):
    x0 = x[0]
    s, d = x0.shape
    a = d // 2
    sh = d // N_DEV
    me = 4 * lax.axis_index("x") + 2 * lax.axis_index("y") + lax.axis_index("c")

    small_a = _pad_rows(jnp.concatenate([norm_a, conv_a[0]], axis=0))
    wia_g, small_g = _gather_two_level("exchange_gather", [_mx(w_in_a[0]), small_a], axes=[1, 1])
    small_f = small_g.reshape(8, d)
    gain_a = small_f[0:1]
    conv_f = _pad_rows(small_f[1:4])
    gain_kv, gain_b, gain_f = norm_kv.reshape(1, d), norm_b.reshape(1, d), norm_f.reshape(1, d)

    wia_f = wia_g.reshape(d, 4 * d)
    u, h, (woa_g, wkv_g, wib_g, wob_g) = _ln_matmul_in_a(
        x0, gain_a, wia_f, [_mx(w_out_a[0]), _mx(w_kv), _mx(w_in_b[0]), _mx(w_out_b[0])], axes=[0, 0, 0, 1])
    woa_f = woa_g.reshape(d, d)
    wkv_f = wkv_g.reshape(d, 2 * a)
    wib_f = wib_g.reshape(d, 2 * a)
    wob_f = wob_g.reshape(a, d)
    x1, kv, q, gate = _layer_a_out(u, x0, conv_f, woa_f, gain_kv, gain_b, wkv_f, wib_f)
    tq = min(ATT_TILE, s)
    idx = jnp.arange(tq)
    tri_suffix = _mx(idx[:, None] >= idx[None, :])
    tri_prefix = _mx(idx[:, None] <= idx[None, :])
    o, c_tot, stop = _attn_fwd(q, kv, tri_suffix)
    dx2, do, dgate, g_wob, g_norm_f, loss_part = _layer_b_out_loss_bwd(o, gate, x1, wob_f, gain_f, loss_target[0])

    dq, dk, dv = _attn_bwd(q, kv, do, c_tot, stop, tri_suffix, tri_prefix)
    dx1, g_wib, g_wkv, g_norm_b, g_norm_kv = _proj_bwd(x1, dx2, dq, dgate, dk, dv, wib_f, wkv_f, gain_b, gain_kv)
    du, g_woa, g_conv = _layer_a_bwd(u, dx1, conv_f, woa_f)
    dx0, g_norm_a = _input_grad(x0, dx1, du, wia_f, gain_a)

    g_wia, (p_woa, p_wkv, p_wib, p_wob) = _grad_w_in_a(
        h, du, N_DEV,
        [g_woa.reshape(N_DEV, sh, d), g_wkv.reshape(N_DEV, sh, 2 * a), g_wib.reshape(N_DEV, sh, 2 * a),
         g_wob.reshape(a, N_DEV, sh).transpose(1, 0, 2)], axes=[0, 0, 0, 0])
    p_wia = _reduce_scatter_two_level("exchange_scatter", g_wia)
    small_grads = jnp.concatenate(
        [g_norm_a, g_conv[0:3], g_norm_kv, g_norm_b, g_norm_f, jnp.pad(loss_part, ((0, 0), (0, d - LANES)))], axis=0)
    (p_small,) = _exchange("exchange_small", [small_grads], scatter=False, axes=[0])

    upd_wia = _reduce_adamw("adamw_w_in_a", p_wia, w_in_a[0], m_w_in_a[0], v_w_in_a[0])
    upd_woa = _reduce_adamw("adamw_w_out_a", p_woa, w_out_a[0], m_w_out_a[0], v_w_out_a[0])
    upd_wkv = _reduce_adamw("adamw_w_kv", p_wkv, w_kv, m_w_kv, v_w_kv)
    upd_wib = _reduce_adamw("adamw_w_in_b", p_wib, w_in_b[0], m_w_in_b[0], v_w_in_b[0])
    upd_wob = _reduce_adamw("adamw_w_out_b", p_wob, w_out_b[0], m_w_out_b[0], v_w_out_b[0])

    def rep(a1, a2, a3):
        return jnp.concatenate([jnp.zeros((4, d), _F32), a1.reshape(1, d), a2.reshape(1, d), a3.reshape(1, d),
                                jnp.zeros((1, d), _F32)], axis=0)

    upd_rep = _reduce_adamw("adamw_replicated", p_small, rep(norm_kv, norm_b, norm_f),
                            rep(m_norm_kv, m_norm_b, m_norm_f), rep(v_norm_kv, v_norm_b, v_norm_f))

    def mine(n1, cv1):
        return _pad_rows(jnp.concatenate([n1, cv1[0]], axis=0))

    p_mine = lax.dynamic_slice(p_small, (0, 0, me * sh), (N_DEV, 8, sh))
    upd_mine = _reduce_adamw("adamw_sharded_small", p_mine, mine(norm_a, conv_a), mine(m_norm_a, m_conv_a),
                             mine(v_norm_a, v_conv_a))

    loss = upd_rep[0][7, 0]
    groups = []
    for k in range(4):
        groups.append([
            upd_mine[k][0:1], upd_wia[k][None], upd_mine[k][1:4][None], upd_woa[k][None], upd_rep[k][4],
            upd_wkv[k], upd_rep[k][5:6], upd_wib[k][None], upd_wob[k][None], upd_rep[k][6]])
    return (loss, dx0[None], *groups[0], *groups[1], *groups[2], *groups[3])
```

```python
import jax
import jax.numpy as jnp
from jax import lax
from jax.experimental import pallas as pl
from jax.experimental.pallas import tpu as pltpu

_MXU_DTYPE = jnp.bfloat16
_F32 = jnp.float32

RMS_EPS = 1e-6
HEAD_DIM = 64
LANES = 128
N_DEV = 8
ATT_TILE = 256
ATT_TILES_PER_STEP = 4
ROW_TILE = 256
BIG_ROW_TILE = 512
VMEM_LIMIT = 56 * 1024 * 1024
SKIP_LOG = -110.0

ADAM_LR = 0.001
ADAM_B1 = 0.9
ADAM_B2 = 0.999
ADAM_EPS = 1e-08
ADAM_WD = 0.01
ADAM_STEP = 10

_NT = (((1,), (1,)), ((), ()))
_TN = (((0,), (0,)), ((), ()))


def _params(n_grid):
    return pltpu.CompilerParams(dimension_semantics=("arbitrary",) * n_grid, vmem_limit_bytes=VMEM_LIMIT)


def _dot(a, b):
    return jnp.dot(a, b, preferred_element_type=_F32)


def _dot_nt(a, b):
    return lax.dot_general(a, b, _NT, preferred_element_type=_F32)


def _dot_tn(a, b):
    return lax.dot_general(a, b, _TN, preferred_element_type=_F32)


def _mx(a):
    return a.astype(_MXU_DTYPE)


def _sigmoid(a):
    return 1.0 / (1.0 + jnp.exp(-a))


def _rms_stats(xv):
    r = lax.rsqrt(jnp.mean(xv * xv, axis=-1, keepdims=True) + RMS_EPS)
    return r, xv * r


def _rms_bwd(r, xh, dyg):
    return r * (dyg - xh * jnp.mean(dyg * xh, axis=-1, keepdims=True))


def _row_spec(tm, width):
    return pl.BlockSpec((tm, width), lambda i: (i, 0))


def _full_spec(shape):
    zeros = (0,) * len(shape)
    return pl.BlockSpec(shape, lambda *_: zeros)


def _zero_at_first(step, *refs):
    @pl.when(step == 0)
    def _():
        for ref in refs:
            ref[...] = jnp.zeros_like(ref)


def _emit_at_last(step, last, pairs):
    @pl.when(step == last)
    def _():
        for acc, out in pairs:
            out[...] = acc[...].astype(out.dtype)


def _slot(ref, idx, axis):
    if axis == 0:
        return ref.at[idx]
    width = ref.shape[1] // N_DEV
    return ref.at[:, pl.ds(pl.multiple_of(idx * width, width), width)]


def _exchange_ops(ins, outs, send_sems, recv_sems, local_sems, scatter, axes):
    n = len(ins)
    x, y, c = lax.axis_index("x"), lax.axis_index("y"), lax.axis_index("c")
    me = 4 * x + 2 * y + c

    def remote(t, m, landed):
        px = 1 - x if m & 4 else x
        py = 1 - y if m & 2 else y
        pc = 1 - c if m & 1 else c
        idx = 4 * px + 2 * py + pc
        k = t * (N_DEV - 1) + m - 1
        return pltpu.make_async_remote_copy(
            src_ref=_slot(ins[t], idx, axes[t]) if scatter else ins[t],
            dst_ref=_slot(outs[t], idx if landed else me, axes[t]),
            send_sem=send_sems.at[k], recv_sem=recv_sems.at[k],
            device_id=(px, py, pc), device_id_type=pl.DeviceIdType.MESH)

    def local(t):
        return pltpu.make_async_copy(_slot(ins[t], me, axes[t]) if scatter else ins[t], _slot(outs[t], me, axes[t]),
                                     local_sems.at[t])

    def start():
        for t in range(n):
            local(t).start()
        for m in range(1, N_DEV):
            for t in range(n):
                remote(t, m, False).start()

    def wait():
        for m in range(1, N_DEV):
            for t in range(n):
                remote(t, m, True).wait_recv()
        for m in range(1, N_DEV):
            for t in range(n):
                remote(t, m, False).wait_send()
        for t in range(n):
            local(t).wait()

    return start, wait


def _exchange_shapes(arrays, scatter, axes):
    if scatter:
        return [jax.ShapeDtypeStruct(a.shape, a.dtype) for a in arrays]
    return [jax.ShapeDtypeStruct((N_DEV,) + a.shape if ax == 0 else (a.shape[0], N_DEV * a.shape[1]), a.dtype)
            for a, ax in zip(arrays, axes)]


def _exchange_sems(n):
    return [pltpu.SemaphoreType.DMA((n * (N_DEV - 1),)), pltpu.SemaphoreType.DMA((n * (N_DEV - 1),)),
            pltpu.SemaphoreType.DMA((n,))]


_ANY = pl.BlockSpec(memory_space=pl.ANY)


def _exchange(name, arrays, scatter, axes):
    n = len(arrays)

    def body(*refs):
        start, wait = _exchange_ops(refs[:n], refs[n:2 * n], *refs[2 * n:], scatter, axes)
        start()
        wait()

    return pl.pallas_call(
        body, name=name, in_specs=[_ANY] * n, out_specs=[_ANY] * n,
        out_shape=_exchange_shapes(arrays, scatter, axes), scratch_shapes=_exchange_sems(n),
    )(*arrays)


def _gather_two_level(name, arrays, axes):
    n = len(arrays)

    def body(*refs):
        ins, outs = refs[:n], refs[n:2 * n]
        send_sems, recv_sems, local_sems = refs[2 * n:]
        x, y, c = lax.axis_index("x"), lax.axis_index("y"), lax.axis_index("c")
        me, sibling = (x, y, c), (x, y, 1 - c)
        chips = [(1 - x, y), (x, 1 - y), (1 - x, 1 - y)]

        def rows(t, dev):
            return _slot(outs[t], 4 * dev[0] + 2 * dev[1] + dev[2], axes[t])

        def copy(t, k, block, to, src=None):
            return pltpu.make_async_remote_copy(
                src_ref=rows(t, block) if src is None else src, dst_ref=rows(t, block),
                send_sem=send_sems.at[t * (N_DEV - 1) + k], recv_sem=recv_sems.at[t * (N_DEV - 1) + k],
                device_id=to, device_id_type=pl.DeviceIdType.MESH)

        sent = []
        for t in range(n):
            pltpu.make_async_copy(ins[t], rows(t, me), local_sems.at[t]).start()
            sent.append(copy(t, 0, me, sibling, src=ins[t]))
            sent += [copy(t, 1 + j, me, (*chip, c), src=ins[t]) for j, chip in enumerate(chips)]
        for cp in sent:
            cp.start()
        for j, chip in enumerate(chips):
            for t in range(n):
                copy(t, 1 + j, (*chip, c), me).wait_recv()
                passed = copy(t, 4 + j, (*chip, c), sibling)
                passed.start()
                sent.append(passed)
        for t in range(n):
            copy(t, 0, sibling, me).wait_recv()
            for j, chip in enumerate(chips):
                copy(t, 4 + j, (*chip, 1 - c), me).wait_recv()
        for cp in sent:
            cp.wait_send()
        for t in range(n):
            pltpu.make_async_copy(ins[t], rows(t, me), local_sems.at[t]).wait()

    return pl.pallas_call(
        body, name=name, in_specs=[_ANY] * n, out_specs=[_ANY] * n,
        out_shape=_exchange_shapes(arrays, False, axes), scratch_shapes=_exchange_sems(n),
    )(*arrays)


def _reduce_scatter_two_level(name, g):
    n_chip = N_DEV // 2
    _, r, cc = g.shape
    chunk = min(256, r)

    def body(g_ref, out_ref, mine, theirs, total, load_sems, pair_send, pair_recv, chip_send, chip_recv, keep_sem):
        x, y, c = lax.axis_index("x"), lax.axis_index("y"), lax.axis_index("c")
        my_chip = 2 * x + y

        def load(q):
            return pltpu.make_async_copy(g_ref.at[2 * q + c], mine.at[q], load_sems.at[q])

        def swap(q):
            return pltpu.make_async_remote_copy(
                src_ref=g_ref.at[2 * q + 1 - c], dst_ref=theirs.at[q], send_sem=pair_send.at[q],
                recv_sem=pair_recv.at[q], device_id=(x, y, 1 - c), device_id_type=pl.DeviceIdType.MESH)

        for q in range(n_chip):
            load(q).start()
            swap(q).start()
        for q in range(n_chip):
            load(q).wait()
            swap(q).wait_recv()

            def add(i, carry, q=q):
                rows = pl.ds(pl.multiple_of(i * chunk, chunk), chunk)
                total[q, rows, :] = (mine[q, rows, :].astype(_F32) + theirs[q, rows, :].astype(_F32)).astype(total.dtype)
                return carry
            lax.fori_loop(0, r // chunk, add, 0)

        def send(k, landed):
            px = 1 - x if k in (0, 2) else x
            py = 1 - y if k in (1, 2) else y
            peer_chip = 2 * px + py
            return pltpu.make_async_remote_copy(
                src_ref=total.at[peer_chip], dst_ref=out_ref.at[peer_chip if landed else my_chip],
                send_sem=chip_send.at[k], recv_sem=chip_recv.at[k],
                device_id=(px, py, c), device_id_type=pl.DeviceIdType.MESH)

        keep = pltpu.make_async_copy(total.at[my_chip], out_ref.at[my_chip], keep_sem)
        keep.start()
        for k in range(n_chip - 1):
            send(k, False).start()
        for k in range(n_chip - 1):
            send(k, True).wait_recv()
        for k in range(n_chip - 1):
            send(k, False).wait_send()
        for q in range(n_chip):
            swap(q).wait_send()
        keep.wait()

    slab = pltpu.VMEM((n_chip, r, cc), g.dtype)
    return pl.pallas_call(
        body, name=name, in_specs=[_ANY], out_specs=_ANY,
        out_shape=jax.ShapeDtypeStruct((n_chip, r, cc), g.dtype),
        scratch_shapes=[slab, slab, slab, pltpu.SemaphoreType.DMA((n_chip,)), pltpu.SemaphoreType.DMA((n_chip,)),
                        pltpu.SemaphoreType.DMA((n_chip,)), pltpu.SemaphoreType.DMA((n_chip - 1,)),
                        pltpu.SemaphoreType.DMA((n_chip - 1,)), pltpu.SemaphoreType.DMA],
        compiler_params=pltpu.CompilerParams(vmem_limit_bytes=VMEM_LIMIT),
    )(g)


def _ln_matmul_in_a(x, gain, w_in, shards, axes):
    s, d = x.shape
    f = w_in.shape[1]
    tm = min(BIG_ROW_TILE, s)
    n = len(shards)
    last = s // tm - 1

    def body(x_ref, g_ref, w_ref, *refs):
        u_ref, h_ref = refs[n:n + 2]
        start, wait = _exchange_ops(refs[:n], refs[n + 2:2 * n + 2], *refs[2 * n + 2:], False, axes)
        pl.when(pl.program_id(0) == 0)(start)
        _, xh = _rms_stats(x_ref[...])
        h = _mx(xh * g_ref[...])
        h_ref[...] = h
        u_ref[...] = _dot(h, w_ref[...])
        pl.when(pl.program_id(0) == last)(wait)

    outs = pl.pallas_call(
        body, name="ln_matmul_in_a", grid=(s // tm,),
        in_specs=[_row_spec(tm, d), _full_spec((1, d)), _full_spec((d, f))] + [_ANY] * n,
        out_specs=[_row_spec(tm, f), _row_spec(tm, d)] + [_ANY] * n,
        out_shape=[jax.ShapeDtypeStruct((s, f), _F32), jax.ShapeDtypeStruct((s, d), _MXU_DTYPE)]
        + _exchange_shapes(shards, False, axes),
        scratch_shapes=_exchange_sems(n),
        compiler_params=_params(1),
    )(x, gain, w_in, *shards)
    return outs[0], outs[1], outs[2:]


def _shift_rows_down(p, before, tm):
    row = lax.broadcasted_iota(jnp.int32, (8, p.shape[1]), 0)
    r1, r2 = pltpu.roll(p, 1, 0), pltpu.roll(p, 2, 0)
    top1 = jnp.where(row == 0, before[7:8, :], r1[0:8, :])
    top2 = jnp.where(row == 0, before[6:7, :], jnp.where(row == 1, before[7:8, :], r2[0:8, :]))
    return jnp.concatenate([top1, r1[8:, :]], axis=0), jnp.concatenate([top2, r2[8:, :]], axis=0)


def _shift_rows_up(p, after, tm):
    row = lax.broadcasted_iota(jnp.int32, (8, p.shape[1]), 0)
    r1, r2 = pltpu.roll(p, tm - 1, 0), pltpu.roll(p, tm - 2, 0)
    end1 = jnp.where(row == 7, after[0:1, :], r1[tm - 8:, :])
    end2 = jnp.where(row == 6, after[0:1, :], jnp.where(row == 7, after[1:2, :], r2[tm - 8:, :]))
    return jnp.concatenate([r1[:tm - 8, :], end1], axis=0), jnp.concatenate([r2[:tm - 8, :], end2], axis=0)


def _layer_a_out(u, x, conv, w_out_a, gain_kv, gain_b, w_kv, w_in_b):
    s, d = x.shape
    a = w_kv.shape[1] // 2
    tm = min(BIG_ROW_TILE, s)
    scale = HEAD_DIM ** -0.5

    def body(u_ref, x_ref, conv_ref, woa_ref, gkv_ref, gb_ref, wkv_ref, wib_ref,
             x1_ref, kv_ref, q_ref, gate_ref, halo_ref):
        _zero_at_first(pl.program_id(0), halo_ref)
        b, c, xin, g = (u_ref[:, k * d:(k + 1) * d] for k in range(4))
        p = c * xin
        p1, p2 = _shift_rows_down(p, halo_ref[...], tm)
        halo_ref[...] = p[tm - 8:tm, :]
        w = conv_ref[...]
        cv = w[0:1, :] * p2 + w[1:2, :] * p1 + w[2:3, :] * p
        ya = (b * cv) * (g * _sigmoid(g))
        x1 = x_ref[...] + _dot(_mx(ya), woa_ref[...])
        x1_ref[...] = x1
        _, xh = _rms_stats(x1)
        kv_ref[...] = _dot(_mx(xh * gkv_ref[...]), wkv_ref[...]).astype(kv_ref.dtype)
        qg = _dot(_mx(xh * gb_ref[...]), wib_ref[...])
        q_ref[...] = (qg[:, :a] * scale).astype(q_ref.dtype)
        gate_ref[...] = qg[:, a:]

    return pl.pallas_call(
        body, name="layer_a_out", grid=(s // tm,),
        in_specs=[_row_spec(tm, 4 * d), _row_spec(tm, d), _full_spec((8, d)), _full_spec((d, d)),
                  _full_spec((1, d)), _full_spec((1, d)), _full_spec((d, 2 * a)), _full_spec((d, 2 * a))],
        out_specs=[_row_spec(tm, d), _row_spec(tm, 2 * a), _row_spec(tm, a), _row_spec(tm, a)],
        out_shape=[jax.ShapeDtypeStruct((s, d), _F32), jax.ShapeDtypeStruct((s, 2 * a), _MXU_DTYPE),
                   jax.ShapeDtypeStruct((s, a), _MXU_DTYPE), jax.ShapeDtypeStruct((s, a), _F32)],
        scratch_shapes=[pltpu.VMEM((8, d), _F32)],
        compiler_params=_params(1),
    )(u, x, conv, w_out_a, gain_kv, gain_b, w_kv, w_in_b)


def _neg_softplus(z):
    return -(jnp.maximum(z, 0.0) + jnp.log(1.0 + jnp.exp(-jnp.abs(z))))


def _scan_dot(val, tri):
    return _dot(_mx(val), tri)


def _head_masks():
    lane = lax.broadcasted_iota(jnp.int32, (1, LANES), 1)
    return [lane < HEAD_DIM, lane >= HEAD_DIM]


def _stacked_causal(tq):
    row = lax.broadcasted_iota(jnp.int32, (2 * tq, tq), 0)
    col = lax.broadcasted_iota(jnp.int32, (2 * tq, tq), 1)
    return col < jnp.where(row >= tq, row - tq, row)


def _attn_fwd(q, kv, tri_suffix):
    s, a = q.shape
    n_hp = a // LANES
    tq = min(ATT_TILE, s)
    nq = s // tq
    per = min(ATT_TILES_PER_STEP, nq)

    def body(q_ref, k_ref, v_ref, tri_ref, o_ref, c_ref, stop_ref):
        hp = pl.program_id(0)
        tri = tri_ref[...]
        causal = _stacked_causal(tq)
        first = _head_masks()[0]

        def block(qs, j, c, acc, diag=False, live=None):
            off = pl.multiple_of(j * tq, tq)
            kb = k_ref[pl.ds(off, tq), :]
            vb = v_ref[pl.ds(off, tq), :]
            z = _dot_nt(qs, kb)
            lg = _neg_softplus(z)
            if diag:
                lg = jnp.where(causal, lg, 0.0)
            tot = jnp.sum(lg, axis=1, keepdims=True)
            c_in = c
            if live is not None:
                c_in = c + (live - 1.0) * 1e30
                tot = tot * live
            w = jnp.exp(z + _scan_dot(lg, tri) + c_in)
            if diag:
                w = jnp.where(causal, w, 0.0)
            return c + tot, acc + _dot(_mx(w), vb)

        tiles = []
        for t in range(per):
            i = pl.program_id(1) * per + t
            qv = q_ref[t * tq:(t + 1) * tq, :]
            qs = jnp.concatenate([jnp.where(hmask, qv, jnp.zeros_like(qv)) for hmask in _head_masks()], axis=0)
            c, acc = block(qs, i, jnp.zeros((2 * tq, 1), _F32), jnp.zeros((2 * tq, LANES), _F32), diag=True)
            c, acc = block(qs, jnp.maximum(i - 1, 0), c, acc, live=jnp.where(i >= 1, 1.0, 0.0))
            tiles.append((i, qs, c, acc))
        for t, (i, qs, c, acc) in enumerate(tiles):
            def cond(carry):
                return jnp.logical_and(carry[0] >= 0, jnp.max(carry[1]) > SKIP_LOG)

            def step(carry, qs=qs):
                c, acc = block(qs, carry[0], carry[1], carry[2])
                return carry[0] - 1, c, acc

            j, c, acc = lax.while_loop(cond, step, (i - 2, c, acc))
            stop_ref[hp, i] = jnp.maximum(jnp.minimum(j + 1, i - 1), 0)
            o_ref[t * tq:(t + 1) * tq, :] = jnp.where(first, acc[:tq], acc[tq:])
            c_ref[t * tq:(t + 1) * tq, :] = jnp.where(first, c[:tq], c[tq:])

    q_spec = pl.BlockSpec((per * tq, LANES), lambda h, i: (i, h))
    return pl.pallas_call(
        body, name="attn_fwd", grid=(n_hp, nq // per),
        in_specs=[q_spec, pl.BlockSpec((s, LANES), lambda h, i: (0, h)),
                  pl.BlockSpec((s, LANES), lambda h, i: (0, n_hp + h)), _full_spec((tq, tq))],
        out_specs=[q_spec, q_spec, pl.BlockSpec(memory_space=pltpu.SMEM)],
        out_shape=[jax.ShapeDtypeStruct((s, a), _F32), jax.ShapeDtypeStruct((s, a), _F32),
                   jax.ShapeDtypeStruct((n_hp, nq), jnp.int32)],
        compiler_params=_params(2),
    )(q, kv, kv, tri_suffix)


def _silu_and_grad(g):
    sg = _sigmoid(g)
    return g * sg, sg * (1.0 + g * (1.0 - sg))


def _layer_b_out_loss_bwd(o, gate, x1, w_out_b, gain_f, target):
    s, d = x1.shape
    a = o.shape[1]
    tm = min(BIG_ROW_TILE, s)
    last = s // tm - 1

    def body(o_ref, gate_ref, x1_ref, wob_ref, gf_ref, tgt_ref,
             dx2_ref, do_ref, dgate_ref, gw_ref, gnf_ref, loss_ref, gw_acc):
        step = pl.program_id(0)
        _zero_at_first(step, gw_acc, gnf_ref, loss_ref)
        sl, dsl = _silu_and_grad(gate_ref[...])
        ov = o_ref[...]
        ob = _mx(ov * sl)
        wob = wob_ref[...]
        x2 = x1_ref[...] + _dot(ob, wob)
        r, xh = _rms_stats(x2)
        gf = gf_ref[...]
        err = xh * gf - tgt_ref[...]
        part = jnp.sum(jnp.sum(err * err, axis=1, keepdims=True), axis=0, keepdims=True)
        loss_ref[...] += part * (0.5 / d)
        dy = err * (1.0 / d)
        gnf_ref[...] += jnp.sum(dy * xh, axis=0, keepdims=True)
        dx2 = _rms_bwd(r, xh, dy * gf)
        dx2_ref[...] = dx2
        dxb = _mx(dx2)
        d_ob = _dot_nt(dxb, wob)
        gw_acc[...] += _dot_tn(ob, dxb)
        do_ref[...] = (d_ob * sl).astype(do_ref.dtype)
        dgate_ref[...] = (d_ob * ov * dsl).astype(dgate_ref.dtype)
        _emit_at_last(step, last, [(gw_acc, gw_ref)])

    return pl.pallas_call(
        body, name="layer_b_out_loss_bwd", grid=(s // tm,),
        in_specs=[_row_spec(tm, a), _row_spec(tm, a), _row_spec(tm, d), _full_spec((a, d)), _full_spec((1, d)),
                  _row_spec(tm, d)],
        out_specs=[_row_spec(tm, d), _row_spec(tm, a), _row_spec(tm, a), _full_spec((a, d)), _full_spec((1, d)),
                   _full_spec((1, LANES))],
        out_shape=[jax.ShapeDtypeStruct((s, d), _F32), jax.ShapeDtypeStruct((s, a), _MXU_DTYPE),
                   jax.ShapeDtypeStruct((s, a), _MXU_DTYPE), jax.ShapeDtypeStruct((a, d), _MXU_DTYPE),
                   jax.ShapeDtypeStruct((1, d), _F32), jax.ShapeDtypeStruct((1, LANES), _F32)],
        scratch_shapes=[pltpu.VMEM((a, d), _F32)],
        compiler_params=_params(1),
    )(o, gate, x1, w_out_b, gain_f, target)


def _attn_bwd(q, kv, do, c_tot, stop, tri_suffix, tri_prefix):
    s, a = q.shape
    n_hp = a // LANES
    tq = min(ATT_TILE, s)
    nq = s // tq
    per = min(ATT_TILES_PER_STEP, nq)
    scale = HEAD_DIM ** -0.5
    chunk = min(1024, s)

    def body(stop_ref, q_ref, do_ref, c_ref, k_ref, v_ref, ts_ref, tp_ref, dq_ref, dk_hbm, dv_hbm,
             dk_acc, dv_acc, stage, sem):
        hp, step = pl.program_id(0), pl.program_id(1)
        _zero_at_first(step, dk_acc, dv_acc)
        ts, tp = ts_ref[...], tp_ref[...]
        causal = _stacked_causal(tq)
        masks = _head_masks()

        def block(tile, j, st, diag=False, live=None, parts=None):
            qs, dos, cs = tile
            asc, pre, dq = st
            rows = pl.ds(pl.multiple_of(j * tq, tq), tq)
            kb = k_ref[rows, :]
            vb = v_ref[rows, :]
            z = _dot_nt(qs, kb)
            lg = _neg_softplus(z)
            sig = jnp.exp(z + lg)
            if diag:
                lg = jnp.where(causal, lg, 0.0)
            tot = jnp.sum(lg, axis=1, keepdims=True)
            newer = jnp.zeros_like(tot) if diag else cs - asc - tot
            if live is not None:
                newer = newer + (live - 1.0) * 1e30
                tot = tot * live
            wgt = jnp.exp(z + _scan_dot(lg, ts) + newer)
            if diag:
                wgt = jnp.where(causal, wgt, 0.0)
            g = wgt * _dot_nt(dos, vb)
            dz = g - sig * (_scan_dot(g, tp) + pre)
            if diag:
                dz = jnp.where(causal, dz, 0.0)
            dzb = _mx(dz)
            dk_part, dv_part = _dot_tn(dzb, qs), _dot_tn(_mx(wgt), dos)
            if parts is None:
                dk_acc[rows, :] += dk_part
                dv_acc[rows, :] += dv_part
            else:
                parts.append((dk_part, dv_part))
            return asc + tot, pre + jnp.sum(g, axis=1, keepdims=True), dq + _dot(dzb, kb)

        tiles = []
        for t in range(per):
            i = step * per + t
            sl = slice(t * tq, (t + 1) * tq)
            qv, dov, cv = q_ref[sl, :], do_ref[sl, :], c_ref[sl, :]
            tile = (jnp.concatenate([jnp.where(hmask, qv, jnp.zeros_like(qv)) for hmask in masks], axis=0),
                    jnp.concatenate([jnp.where(hmask, dov, jnp.zeros_like(dov)) for hmask in masks], axis=0),
                    jnp.concatenate([cv[:, 0:1], cv[:, HEAD_DIM:HEAD_DIM + 1]], axis=0))
            first = jnp.clip(stop_ref[hp, i], 0, jnp.maximum(i - 1, 0))
            st = (jnp.zeros((2 * tq, 1), _F32), jnp.zeros((2 * tq, 1), _F32), jnp.zeros((2 * tq, LANES), _F32))
            st = lax.fori_loop(first, i - 1, lambda j, st, tile=tile: block(tile, j, st), st)
            tiles.append((i, sl, tile, st))
        parts = []
        for i, sl, tile, st in tiles:
            st = block(tile, jnp.maximum(i - 1, 0), st, live=jnp.where(i >= 1, 1.0, 0.0), parts=parts)
            dq = block(tile, i, st, diag=True, parts=parts)[2]
            dq_ref[sl, :] = (jnp.where(masks[0], dq[:tq], dq[tq:]) * scale).astype(dq_ref.dtype)
        for u in range(per + 1):
            terms = ([parts[2 * u - 1]] if u >= 1 else []) + ([parts[2 * u]] if u < per else [])
            dk_sum, dv_sum = terms[0] if len(terms) == 1 else (terms[0][0] + terms[1][0], terms[0][1] + terms[1][1])
            rows = pl.ds(pl.multiple_of(jnp.maximum(step * per + u - 1, 0) * tq, tq), tq)
            dk_acc[rows, :] += dk_sum
            dv_acc[rows, :] += dv_sum

        @pl.when(step == nq // per - 1)
        def _():
            cols = pl.ds(pl.multiple_of(hp * LANES, LANES), LANES)
            for acc, out in ((dk_acc, dk_hbm), (dv_acc, dv_hbm)):
                def cast(n, carry, acc=acc):
                    rows = pl.ds(pl.multiple_of(n * chunk, chunk), chunk)
                    stage[rows, :] = acc[rows, :].astype(stage.dtype)
                    return carry
                lax.fori_loop(0, s // chunk, cast, 0)
                cp = pltpu.make_async_copy(stage, out.at[:, cols], sem)
                cp.start()
                cp.wait()

    q_spec = pl.BlockSpec((per * tq, LANES), lambda h, i, *_: (i, h))
    grid_spec = pltpu.PrefetchScalarGridSpec(
        num_scalar_prefetch=1, grid=(n_hp, nq // per),
        in_specs=[q_spec, q_spec, q_spec, pl.BlockSpec((s, LANES), lambda h, i, *_: (0, h)),
                  pl.BlockSpec((s, LANES), lambda h, i, *_: (0, n_hp + h)),
                  pl.BlockSpec((tq, tq), lambda h, i, *_: (0, 0)), pl.BlockSpec((tq, tq), lambda h, i, *_: (0, 0))],
        out_specs=[q_spec, pl.BlockSpec(memory_space=pl.ANY), pl.BlockSpec(memory_space=pl.ANY)],
        scratch_shapes=[pltpu.VMEM((s, LANES), _F32), pltpu.VMEM((s, LANES), _F32), pltpu.VMEM((s, LANES), _MXU_DTYPE),
                        pltpu.SemaphoreType.DMA],
    )
    return pl.pallas_call(
        body, name="attn_bwd", grid_spec=grid_spec,
        out_shape=[jax.ShapeDtypeStruct((s, a), _MXU_DTYPE)] * 3,
        compiler_params=_params(2),
    )(stop, q, do, c_tot, kv, kv, tri_suffix, tri_prefix)


def _proj_bwd(x1, dx2, dq, dgate, dk, dv, w_in_b, w_kv, gain_b, gain_kv):
    s, d = x1.shape
    a = dq.shape[1]
    tm = min(BIG_ROW_TILE, s)

    def body(x1_ref, dx2_ref, dq_ref, dgate_ref, dk_ref, dv_ref, wib_ref, wkv_ref, gb_ref, gkv_ref,
             dx1_ref, gwib_out, gwkv_out, gnb_ref, gnkv_ref, gwib_ref, gwkv_ref):
        step = pl.program_id(0)
        _zero_at_first(step, gwib_ref, gwkv_ref, gnb_ref, gnkv_ref)
        r, xh = _rms_stats(x1_ref[...])
        gb, gkv = gb_ref[...], gkv_ref[...]
        hb, hk = _mx(xh * gb), _mx(xh * gkv)
        dq, dgate, dk, dv = dq_ref[...], dgate_ref[...], dk_ref[...], dv_ref[...]
        gwib_ref[:, :a] += _dot_tn(hb, dq)
        gwib_ref[:, a:] += _dot_tn(hb, dgate)
        gwkv_ref[:, :a] += _dot_tn(hk, dk)
        gwkv_ref[:, a:] += _dot_tn(hk, dv)
        d_hb = _dot_nt(dq, wib_ref[:, :a]) + _dot_nt(dgate, wib_ref[:, a:])
        d_hk = _dot_nt(dk, wkv_ref[:, :a]) + _dot_nt(dv, wkv_ref[:, a:])
        gnb_ref[...] += jnp.sum(d_hb * xh, axis=0, keepdims=True)
        gnkv_ref[...] += jnp.sum(d_hk * xh, axis=0, keepdims=True)
        dx1_ref[...] = dx2_ref[...] + _rms_bwd(r, xh, d_hb * gb + d_hk * gkv)
        _emit_at_last(step, s // tm - 1, [(gwib_ref, gwib_out), (gwkv_ref, gwkv_out)])

    return pl.pallas_call(
        body, name="proj_bwd", grid=(s // tm,),
        in_specs=[_row_spec(tm, d), _row_spec(tm, d)] + [_row_spec(tm, a)] * 4
        + [_full_spec((d, 2 * a)), _full_spec((d, 2 * a)), _full_spec((1, d)), _full_spec((1, d))],
        out_specs=[_row_spec(tm, d), _full_spec((d, 2 * a)), _full_spec((d, 2 * a)), _full_spec((1, d)),
                   _full_spec((1, d))],
        out_shape=[jax.ShapeDtypeStruct((s, d), _F32), jax.ShapeDtypeStruct((d, 2 * a), _MXU_DTYPE),
                   jax.ShapeDtypeStruct((d, 2 * a), _MXU_DTYPE), jax.ShapeDtypeStruct((1, d), _F32),
                   jax.ShapeDtypeStruct((1, d), _F32)],
        scratch_shapes=[pltpu.VMEM((d, 2 * a), _F32), pltpu.VMEM((d, 2 * a), _F32)],
        compiler_params=_params(1),
    )(x1, dx2, dq, dgate, dk, dv, w_in_b, w_kv, gain_b, gain_kv)


def _layer_a_bwd(u, dx1, conv, w_out_a):
    s, d = dx1.shape
    tm = min(BIG_ROW_TILE, s)
    n = s // tm
    per8 = tm // 8

    def body(u_ref, uprev_ref, dx1_ref, conv_ref, woa_ref, du_ref, gwoa_out, gconv_ref, halo_ref, gwoa_ref):
        step = pl.program_id(0)
        _zero_at_first(step, gwoa_ref, gconv_ref, halo_ref)
        b, c, xin, g = (u_ref[:, k * d:(k + 1) * d] for k in range(4))
        p = c * xin
        before = uprev_ref[:, d:2 * d] * uprev_ref[:, 2 * d:3 * d]
        before = jnp.where(step == n - 1, jnp.zeros_like(before), before)
        p1, p2 = _shift_rows_down(p, before, tm)
        w = conv_ref[...]
        cv = w[0:1, :] * p2 + w[1:2, :] * p1 + w[2:3, :] * p
        sl, dsl = _silu_and_grad(g)
        y = b * cv
        dxb = _mx(dx1_ref[...])
        gwoa_ref[...] += _dot_tn(_mx(y * sl), dxb)
        d_ya = _dot_nt(dxb, woa_ref[...])
        d_y = d_ya * sl
        d_cv = d_y * b
        n1, n2 = _shift_rows_up(d_cv, halo_ref[...], tm)
        halo_ref[...] = d_cv[0:8, :]
        d_p = w[2:3, :] * d_cv + w[1:2, :] * n1 + w[0:1, :] * n2
        gconv_ref[0:1, :] += jnp.sum(d_cv * p2, axis=0, keepdims=True)
        gconv_ref[1:2, :] += jnp.sum(d_cv * p1, axis=0, keepdims=True)
        gconv_ref[2:3, :] += jnp.sum(d_cv * p, axis=0, keepdims=True)
        du_ref[:, 0:d] = (d_y * cv).astype(du_ref.dtype)
        du_ref[:, d:2 * d] = (d_p * xin).astype(du_ref.dtype)
        du_ref[:, 2 * d:3 * d] = (d_p * c).astype(du_ref.dtype)
        du_ref[:, 3 * d:4 * d] = (d_ya * y * dsl).astype(du_ref.dtype)
        _emit_at_last(step, n - 1, [(gwoa_ref, gwoa_out)])

    def rev(i):
        return (n - 1 - i, 0)

    return pl.pallas_call(
        body, name="layer_a_bwd", grid=(n,),
        in_specs=[pl.BlockSpec((tm, 4 * d), rev),
                  pl.BlockSpec((8, 4 * d), lambda i: (jnp.maximum((n - 1 - i) * per8 - 1, 0), 0)),
                  pl.BlockSpec((tm, d), rev), _full_spec((8, d)), _full_spec((d, d))],
        out_specs=[pl.BlockSpec((tm, 4 * d), rev), _full_spec((d, d)), _full_spec((8, d))],
        out_shape=[jax.ShapeDtypeStruct((s, 4 * d), _MXU_DTYPE), jax.ShapeDtypeStruct((d, d), _MXU_DTYPE),
                   jax.ShapeDtypeStruct((8, d), _F32)],
        scratch_shapes=[pltpu.VMEM((8, d), _F32), pltpu.VMEM((d, d), _F32)],
        compiler_params=_params(1),
    )(u, u, dx1, conv, w_out_a)


def _grad_w_in_a(h, du, nb, grads, axes):
    s, d = h.shape
    bn = du.shape[1] // nb
    tm = min(4 * BIG_ROW_TILE, s)
    half = nb // 2
    n = len(grads)
    steps = s // tm

    def body(h_ref, du_ref, *refs):
        gw_out, gw_ref = refs[n], refs[-1]
        start, wait = _exchange_ops(refs[:n], refs[n + 1:2 * n + 1], *refs[2 * n + 1:-1], True, axes)
        jh, i = pl.program_id(0), pl.program_id(1)
        pl.when(jnp.logical_and(jh == 0, i == 0))(start)
        _zero_at_first(i, gw_ref)
        hv = h_ref[...]
        for j in range(half):
            gw_ref[j] += _dot_tn(hv, du_ref[:, j * bn:(j + 1) * bn])
        _emit_at_last(i, steps - 1, [(gw_ref, gw_out)])
        pl.when(jnp.logical_and(jh == 1, i == steps - 1))(wait)

    outs = pl.pallas_call(
        body, name="grad_w_in_a", grid=(2, steps),
        in_specs=[pl.BlockSpec((tm, d), lambda jh, i: (i, 0)), pl.BlockSpec((tm, half * bn), lambda jh, i: (i, jh))]
        + [_ANY] * n,
        out_specs=[pl.BlockSpec((half, d, bn), lambda jh, i: (jh, 0, 0))] + [_ANY] * n,
        out_shape=[jax.ShapeDtypeStruct((nb, d, bn), _MXU_DTYPE)] + _exchange_shapes(grads, True, axes),
        scratch_shapes=_exchange_sems(n) + [pltpu.VMEM((half, d, bn), _F32)],
        compiler_params=_params(2),
    )(h, du, *grads)
    return outs[0], outs[1:]


def _input_grad(x, dx1, du, w_in, gain):
    s, d = x.shape
    f = w_in.shape[1]
    tm = min(BIG_ROW_TILE, s)

    def body(x_ref, dx1_ref, du_ref, w_ref, g_ref, dx_ref, gn_ref):
        _zero_at_first(pl.program_id(0), gn_ref)
        r, xh = _rms_stats(x_ref[...])
        d_h = _dot_nt(du_ref[...], w_ref[...])
        gn_ref[...] += jnp.sum(d_h * xh, axis=0, keepdims=True)
        dx_ref[...] = dx1_ref[...] + _rms_bwd(r, xh, d_h * g_ref[...])

    return pl.pallas_call(
        body, name="input_grad", grid=(s // tm,),
        in_specs=[_row_spec(tm, d), _row_spec(tm, d), _row_spec(tm, f), _full_spec((d, f)), _full_spec((1, d))],
        out_specs=[_row_spec(tm, d), _full_spec((1, d))],
        out_shape=[jax.ShapeDtypeStruct((s, d), _F32), jax.ShapeDtypeStruct((1, d), _F32)],
        compiler_params=_params(1),
    )(x, dx1, du, w_in, gain)


def _reduce_adamw(name, parts, w, m, v):
    rows, cols = w.shape
    n_parts = parts.shape[0]
    tr = min(256, rows)
    c1 = 1.0 - ADAM_B1 ** ADAM_STEP
    c2 = 1.0 - ADAM_B2 ** ADAM_STEP

    def body(p_ref, w_ref, m_ref, v_ref, g_ref, d_ref, nm_ref, nv_ref):
        g = p_ref[0].astype(_F32)
        for k in range(1, n_parts):
            g = g + p_ref[k].astype(_F32)
        nm = ADAM_B1 * m_ref[...] + (1.0 - ADAM_B1) * g
        nv = ADAM_B2 * v_ref[...] + (1.0 - ADAM_B2) * (g * g)
        g_ref[...] = g
        nm_ref[...] = nm
        nv_ref[...] = nv
        d_ref[...] = -ADAM_LR * ((nm / c1) / (jnp.sqrt(nv / c2) + ADAM_EPS) + ADAM_WD * w_ref[...])

    tile = _row_spec(tr, cols)
    return pl.pallas_call(
        body, name=name, grid=(rows // tr,),
        in_specs=[pl.BlockSpec((n_parts, tr, cols), lambda i: (0, i, 0)), tile, tile, tile],
        out_specs=[tile] * 4,
        out_shape=[jax.ShapeDtypeStruct((rows, cols), _F32)] * 4,
        compiler_params=_params(1),
    )(parts, w, m, v)


def _pad_rows(a, rows=8):
    return jnp.pad(a, ((0, rows - a.shape[0]), (0, 0)))


def kernel(x, norm_a, w_in_a, conv_a, w_out_a, norm_kv, w_kv, norm_b, w_in_b, w_out_b, norm_f, loss_target, m_norm_a, m_w_in_a, m_conv_a, m_w_out_a, m_norm_kv, m_w_kv, m_norm_b, m_w_in_b, m_w_out_b, m_norm_f, v_norm_a, v_w_in_a, v_conv_a, v_w_out_a, v_norm_kv, v_w_kv, v_norm_b, v_w_in_b, v_w_out_b, v_norm_f):
    x0 = x[0]
    s, d = x0.shape
    a = d // 2
    sh = d // N_DEV
    me = 4 * lax.axis_index("x") + 2 * lax.axis_index("y") + lax.axis_index("c")

    small_a = _pad_rows(jnp.concatenate([norm_a, conv_a[0]], axis=0))
    wia_g, small_g = _gather_two_level("exchange_gather", [_mx(w_in_a[0]), small_a], axes=[1, 1])
    small_f = small_g.reshape(8, d)
    gain_a = small_f[0:1]
    conv_f = _pad_rows(small_f[1:4])
    gain_kv, gain_b, gain_f = norm_kv.reshape(1, d), norm_b.reshape(1, d), norm_f.reshape(1, d)

    wia_f = wia_g.reshape(d, 4 * d)
    u, h, (woa_g, wkv_g, wib_g, wob_g) = _ln_matmul_in_a(
        x0, gain_a, wia_f, [_mx(w_out_a[0]), _mx(w_kv), _mx(w_in_b[0]), _mx(w_out_b[0])], axes=[0, 0, 0, 1])
    woa_f = woa_g.reshape(d, d)
    wkv_f = wkv_g.reshape(d, 2 * a)
    wib_f = wib_g.reshape(d, 2 * a)
    wob_f = wob_g.reshape(a, d)
    x1, kv, q, gate = _layer_a_out(u, x0, conv_f, woa_f, gain_kv, gain_b, wkv_f, wib_f)
    tq = min(ATT_TILE, s)
    idx = jnp.arange(tq)
    tri_suffix = _mx(idx[:, None] >= idx[None, :])
    tri_prefix = _mx(idx[:, None] <= idx[None, :])
    o, c_tot, stop = _attn_fwd(q, kv, tri_suffix)
    dx2, do, dgate, g_wob, g_norm_f, loss_part = _layer_b_out_loss_bwd(o, gate, x1, wob_f, gain_f, loss_target[0])

    dq, dk, dv = _attn_bwd(q, kv, do, c_tot, stop, tri_suffix, tri_prefix)
    dx1, g_wib, g_wkv, g_norm_b, g_norm_kv = _proj_bwd(x1, dx2, dq, dgate, dk, dv, wib_f, wkv_f, gain_b, gain_kv)
    du, g_woa, g_conv = _layer_a_bwd(u, dx1, conv_f, woa_f)
    dx0, g_norm_a = _input_grad(x0, dx1, du, wia_f, gain_a)

    g_wia, (p_woa, p_wkv, p_wib, p_wob) = _grad_w_in_a(
        h, du, N_DEV,
        [g_woa.reshape(N_DEV, sh, d), g_wkv.reshape(N_DEV, sh, 2 * a), g_wib.reshape(N_DEV, sh, 2 * a),
         g_wob.reshape(a, N_DEV, sh).transpose(1, 0, 2)], axes=[0, 0, 0, 0])
    p_wia = _reduce_scatter_two_level("exchange_scatter", g_wia)
    small_grads = jnp.concatenate(
        [g_norm_a, g_conv[0:3], g_norm_kv, g_norm_b, g_norm_f, jnp.pad(loss_part, ((0, 0), (0, d - LANES)))], axis=0)
    (p_small,) = _exchange("exchange_small", [small_grads], scatter=False, axes=[0])

    upd_wia = _reduce_adamw("adamw_w_in_a", p_wia, w_in_a[0], m_w_in_a[0], v_w_in_a[0])
    upd_woa = _reduce_adamw("adamw_w_out_a", p_woa, w_out_a[0], m_w_out_a[0], v_w_out_a[0])
    upd_wkv = _reduce_adamw("adamw_w_kv", p_wkv, w_kv, m_w_kv, v_w_kv)
    upd_wib = _reduce_adamw("adamw_w_in_b", p_wib, w_in_b[0], m_w_in_b[0], v_w_in_b[0])
    upd_wob = _reduce_adamw("adamw_w_out_b", p_wob, w_out_b[0], m_w_out_b[0], v_w_out_b[0])

    def rep(a1, a2, a3):
        return jnp.concatenate([jnp.zeros((4, d), _F32), a1.reshape(1, d), a2.reshape(1, d), a3.reshape(1, d),
                                jnp.zeros((1, d), _F32)], axis=0)

    upd_rep = _reduce_adamw("adamw_replicated", p_small, rep(norm_kv, norm_b, norm_f),
                            rep(m_norm_kv, m_norm_b, m_norm_f), rep(v_norm_kv, v_norm_b, v_norm_f))

    def mine(n1, cv1):
        return _pad_rows(jnp.concatenate([n1, cv1[0]], axis=0))

    p_mine = lax.dynamic_slice(p_small, (0, 0, me * sh), (N_DEV, 8, sh))
    upd_mine = _reduce_adamw("adamw_sharded_small", p_mine, mine(norm_a, conv_a), mine(m_norm_a, m_conv_a),
                             mine(v_norm_a, v_conv_a))

    loss = upd_rep[0][7, 0]
    groups = []
    for k in range(4):
        groups.append([
            upd_mine[k][0:1], upd_wia[k][None], upd_mine[k][1:4][None], upd_woa[k][None], upd_rep[k][4],
            upd_wkv[k], upd_rep[k][5:6], upd_wib[k][None], upd_wob[k][None], upd_rep[k][6]])
    return (loss, dx0[None], *groups[0], *groups[1], *groups[2], *groups[3])
```

```python
import jax
import jax.numpy as jnp
from jax import lax
from jax.experimental import pallas as pl
from jax.experimental.pallas import tpu as pltpu

_MXU_DTYPE = jnp.bfloat16
_F32 = jnp.float32

RMS_EPS = 1e-6
HEAD_DIM = 64
LANES = 128
N_DEV = 8
ATT_TILE = 256
ATT_TILES_PER_STEP = 8
ROW_TILE = 256
BIG_ROW_TILE = 512
VMEM_LIMIT = 56 * 1024 * 1024
SKIP_LOG = -110.0

ADAM_LR = 0.001
ADAM_B1 = 0.9
ADAM_B2 = 0.999
ADAM_EPS = 1e-08
ADAM_WD = 0.01
ADAM_STEP = 10

_NT = (((1,), (1,)), ((), ()))
_TN = (((0,), (0,)), ((), ()))


def _params(n_grid):
    return pltpu.CompilerParams(dimension_semantics=("arbitrary",) * n_grid, vmem_limit_bytes=VMEM_LIMIT)


def _dot(a, b):
    return jnp.dot(a, b, preferred_element_type=_F32)


def _dot_nt(a, b):
    return lax.dot_general(a, b, _NT, preferred_element_type=_F32)


def _dot_tn(a, b):
    return lax.dot_general(a, b, _TN, preferred_element_type=_F32)


def _mx(a):
    return a.astype(_MXU_DTYPE)


def _sigmoid(a):
    return 1.0 / (1.0 + jnp.exp(-a))


def _rms_stats(xv):
    r = lax.rsqrt(jnp.mean(xv * xv, axis=-1, keepdims=True) + RMS_EPS)
    return r, xv * r


def _rms_bwd(r, xh, dyg):
    return r * (dyg - xh * jnp.mean(dyg * xh, axis=-1, keepdims=True))


def _row_spec(tm, width):
    return pl.BlockSpec((tm, width), lambda i: (i, 0))


def _full_spec(shape):
    zeros = (0,) * len(shape)
    return pl.BlockSpec(shape, lambda *_: zeros)


def _zero_at_first(step, *refs):
    @pl.when(step == 0)
    def _():
        for ref in refs:
            ref[...] = jnp.zeros_like(ref)


def _emit_at_last(step, last, pairs):
    @pl.when(step == last)
    def _():
        for acc, out in pairs:
            out[...] = acc[...].astype(out.dtype)


def _slot(ref, idx, axis):
    if axis == 0:
        return ref.at[idx]
    width = ref.shape[1] // N_DEV
    return ref.at[:, pl.ds(pl.multiple_of(idx * width, width), width)]


def _exchange_ops(ins, outs, send_sems, recv_sems, local_sems, scatter, axes):
    n = len(ins)
    x, y, c = lax.axis_index("x"), lax.axis_index("y"), lax.axis_index("c")
    me = 4 * x + 2 * y + c

    def remote(t, m, landed):
        px = 1 - x if m & 4 else x
        py = 1 - y if m & 2 else y
        pc = 1 - c if m & 1 else c
        idx = 4 * px + 2 * py + pc
        k = t * (N_DEV - 1) + m - 1
        return pltpu.make_async_remote_copy(
            src_ref=_slot(ins[t], idx, axes[t]) if scatter else ins[t],
            dst_ref=_slot(outs[t], idx if landed else me, axes[t]),
            send_sem=send_sems.at[k], recv_sem=recv_sems.at[k],
            device_id=(px, py, pc), device_id_type=pl.DeviceIdType.MESH)

    def local(t):
        return pltpu.make_async_copy(_slot(ins[t], me, axes[t]) if scatter else ins[t], _slot(outs[t], me, axes[t]),
                                     local_sems.at[t])

    def start():
        for t in range(n):
            local(t).start()
        for m in range(1, N_DEV):
            for t in range(n):
                remote(t, m, False).start()

    def wait():
        for m in range(1, N_DEV):
            for t in range(n):
                remote(t, m, True).wait_recv()
        for m in range(1, N_DEV):
            for t in range(n):
                remote(t, m, False).wait_send()
        for t in range(n):
            local(t).wait()

    return start, wait


def _exchange_shapes(arrays, scatter, axes):
    if scatter:
        return [jax.ShapeDtypeStruct(a.shape, a.dtype) for a in arrays]
    return [jax.ShapeDtypeStruct((N_DEV,) + a.shape if ax == 0 else (a.shape[0], N_DEV * a.shape[1]), a.dtype)
            for a, ax in zip(arrays, axes)]


def _exchange_sems(n):
    return [pltpu.SemaphoreType.DMA((n * (N_DEV - 1),)), pltpu.SemaphoreType.DMA((n * (N_DEV - 1),)),
            pltpu.SemaphoreType.DMA((n,))]


_ANY = pl.BlockSpec(memory_space=pl.ANY)


def _exchange(name, arrays, scatter, axes):
    n = len(arrays)

    def body(*refs):
        start, wait = _exchange_ops(refs[:n], refs[n:2 * n], *refs[2 * n:], scatter, axes)
        start()
        wait()

    return pl.pallas_call(
        body, name=name, in_specs=[_ANY] * n, out_specs=[_ANY] * n,
        out_shape=_exchange_shapes(arrays, scatter, axes), scratch_shapes=_exchange_sems(n),
    )(*arrays)


def _gather_two_level(name, arrays, axes):
    n = len(arrays)

    def body(*refs):
        ins, outs = refs[:n], refs[n:2 * n]
        send_sems, recv_sems, local_sems = refs[2 * n:]
        x, y, c = lax.axis_index("x"), lax.axis_index("y"), lax.axis_index("c")
        me, sibling = (x, y, c), (x, y, 1 - c)
        chips = [(1 - x, y), (x, 1 - y), (1 - x, 1 - y)]

        def rows(t, dev):
            return _slot(outs[t], 4 * dev[0] + 2 * dev[1] + dev[2], axes[t])

        def copy(t, k, block, to, src=None):
            return pltpu.make_async_remote_copy(
                src_ref=rows(t, block) if src is None else src, dst_ref=rows(t, block),
                send_sem=send_sems.at[t * (N_DEV - 1) + k], recv_sem=recv_sems.at[t * (N_DEV - 1) + k],
                device_id=to, device_id_type=pl.DeviceIdType.MESH)

        sent = []
        for t in range(n):
            pltpu.make_async_copy(ins[t], rows(t, me), local_sems.at[t]).start()
            sent.append(copy(t, 0, me, sibling, src=ins[t]))
            sent += [copy(t, 1 + j, me, (*chip, c), src=ins[t]) for j, chip in enumerate(chips)]
        for cp in sent:
            cp.start()
        for j, chip in enumerate(chips):
            for t in range(n):
                copy(t, 1 + j, (*chip, c), me).wait_recv()
                passed = copy(t, 4 + j, (*chip, c), sibling)
                passed.start()
                sent.append(passed)
        for t in range(n):
            copy(t, 0, sibling, me).wait_recv()
            for j, chip in enumerate(chips):
                copy(t, 4 + j, (*chip, 1 - c), me).wait_recv()
        for cp in sent:
            cp.wait_send()
        for t in range(n):
            pltpu.make_async_copy(ins[t], rows(t, me), local_sems.at[t]).wait()

    return pl.pallas_call(
        body, name=name, in_specs=[_ANY] * n, out_specs=[_ANY] * n,
        out_shape=_exchange_shapes(arrays, False, axes), scratch_shapes=_exchange_sems(n),
    )(*arrays)


def _reduce_scatter_two_level(name, g):
    n_chip = N_DEV // 2
    _, r, cc = g.shape
    chunk = min(256, r)

    def body(g_ref, out_ref, mine, theirs, total, load_sems, pair_send, pair_recv, chip_send, chip_recv, keep_sem):
        x, y, c = lax.axis_index("x"), lax.axis_index("y"), lax.axis_index("c")
        my_chip = 2 * x + y

        def load(q):
            return pltpu.make_async_copy(g_ref.at[2 * q + c], mine.at[q], load_sems.at[q])

        def swap(q):
            return pltpu.make_async_remote_copy(
                src_ref=g_ref.at[2 * q + 1 - c], dst_ref=theirs.at[q], send_sem=pair_send.at[q],
                recv_sem=pair_recv.at[q], device_id=(x, y, 1 - c), device_id_type=pl.DeviceIdType.MESH)

        for q in range(n_chip):
            load(q).start()
            swap(q).start()
        for q in range(n_chip):
            load(q).wait()
            swap(q).wait_recv()

            def add(i, carry, q=q):
                rows = pl.ds(pl.multiple_of(i * chunk, chunk), chunk)
                total[q, rows, :] = (mine[q, rows, :].astype(_F32) + theirs[q, rows, :].astype(_F32)).astype(total.dtype)
                return carry
            lax.fori_loop(0, r // chunk, add, 0)

        def send(k, landed):
            px = 1 - x if k in (0, 2) else x
            py = 1 - y if k in (1, 2) else y
            peer_chip = 2 * px + py
            return pltpu.make_async_remote_copy(
                src_ref=total.at[peer_chip], dst_ref=out_ref.at[peer_chip if landed else my_chip],
                send_sem=chip_send.at[k], recv_sem=chip_recv.at[k],
                device_id=(px, py, c), device_id_type=pl.DeviceIdType.MESH)

        keep = pltpu.make_async_copy(total.at[my_chip], out_ref.at[my_chip], keep_sem)
        keep.start()
        for k in range(n_chip - 1):
            send(k, False).start()
        for k in range(n_chip - 1):
            send(k, True).wait_recv()
        for k in range(n_chip - 1):
            send(k, False).wait_send()
        for q in range(n_chip):
            swap(q).wait_send()
        keep.wait()

    slab = pltpu.VMEM((n_chip, r, cc), g.dtype)
    return pl.pallas_call(
        body, name=name, in_specs=[_ANY], out_specs=_ANY,
        out_shape=jax.ShapeDtypeStruct((n_chip, r, cc), g.dtype),
        scratch_shapes=[slab, slab, slab, pltpu.SemaphoreType.DMA((n_chip,)), pltpu.SemaphoreType.DMA((n_chip,)),
                        pltpu.SemaphoreType.DMA((n_chip,)), pltpu.SemaphoreType.DMA((n_chip - 1,)),
                        pltpu.SemaphoreType.DMA((n_chip - 1,)), pltpu.SemaphoreType.DMA],
        compiler_params=pltpu.CompilerParams(vmem_limit_bytes=VMEM_LIMIT),
    )(g)


def _ln_matmul_in_a(x, gain, w_in, shards, axes):
    s, d = x.shape
    f = w_in.shape[1]
    tm = min(BIG_ROW_TILE, s)
    n = len(shards)
    last = s // tm - 1

    def body(x_ref, g_ref, w_ref, *refs):
        u_ref, h_ref = refs[n:n + 2]
        start, wait = _exchange_ops(refs[:n], refs[n + 2:2 * n + 2], *refs[2 * n + 2:], False, axes)
        pl.when(pl.program_id(0) == 0)(start)
        _, xh = _rms_stats(x_ref[...])
        h = _mx(xh * g_ref[...])
        h_ref[...] = h
        u_ref[...] = _dot(h, w_ref[...])
        pl.when(pl.program_id(0) == last)(wait)

    outs = pl.pallas_call(
        body, name="ln_matmul_in_a", grid=(s // tm,),
        in_specs=[_row_spec(tm, d), _full_spec((1, d)), _full_spec((d, f))] + [_ANY] * n,
        out_specs=[_row_spec(tm, f), _row_spec(tm, d)] + [_ANY] * n,
        out_shape=[jax.ShapeDtypeStruct((s, f), _F32), jax.ShapeDtypeStruct((s, d), _MXU_DTYPE)]
        + _exchange_shapes(shards, False, axes),
        scratch_shapes=_exchange_sems(n),
        compiler_params=_params(1),
    )(x, gain, w_in, *shards)
    return outs[0], outs[1], outs[2:]


def _shift_rows_down(p, before, tm):
    row = lax.broadcasted_iota(jnp.int32, (8, p.shape[1]), 0)
    r1, r2 = pltpu.roll(p, 1, 0), pltpu.roll(p, 2, 0)
    top1 = jnp.where(row == 0, before[7:8, :], r1[0:8, :])
    top2 = jnp.where(row == 0, before[6:7, :], jnp.where(row == 1, before[7:8, :], r2[0:8, :]))
    return jnp.concatenate([top1, r1[8:, :]], axis=0), jnp.concatenate([top2, r2[8:, :]], axis=0)


def _shift_rows_up(p, after, tm):
    row = lax.broadcasted_iota(jnp.int32, (8, p.shape[1]), 0)
    r1, r2 = pltpu.roll(p, tm - 1, 0), pltpu.roll(p, tm - 2, 0)
    end1 = jnp.where(row == 7, after[0:1, :], r1[tm - 8:, :])
    end2 = jnp.where(row == 6, after[0:1, :], jnp.where(row == 7, after[1:2, :], r2[tm - 8:, :]))
    return jnp.concatenate([r1[:tm - 8, :], end1], axis=0), jnp.concatenate([r2[:tm - 8, :], end2], axis=0)


def _layer_a_out(u, x, conv, w_out_a, gain_kv, gain_b, w_kv, w_in_b):
    s, d = x.shape
    a = w_kv.shape[1] // 2
    tm = min(BIG_ROW_TILE, s)
    scale = HEAD_DIM ** -0.5

    def body(u_ref, x_ref, conv_ref, woa_ref, gkv_ref, gb_ref, wkv_ref, wib_ref,
             x1_ref, kv_ref, q_ref, gate_ref, halo_ref):
        _zero_at_first(pl.program_id(0), halo_ref)
        b, c, xin, g = (u_ref[:, k * d:(k + 1) * d] for k in range(4))
        p = c * xin
        p1, p2 = _shift_rows_down(p, halo_ref[...], tm)
        halo_ref[...] = p[tm - 8:tm, :]
        w = conv_ref[...]
        cv = w[0:1, :] * p2 + w[1:2, :] * p1 + w[2:3, :] * p
        ya = (b * cv) * (g * _sigmoid(g))
        x1 = x_ref[...] + _dot(_mx(ya), woa_ref[...])
        x1_ref[...] = x1
        _, xh = _rms_stats(x1)
        kv_ref[...] = _dot(_mx(xh * gkv_ref[...]), wkv_ref[...]).astype(kv_ref.dtype)
        qg = _dot(_mx(xh * gb_ref[...]), wib_ref[...])
        q_ref[...] = (qg[:, :a] * scale).astype(q_ref.dtype)
        gate_ref[...] = qg[:, a:]

    return pl.pallas_call(
        body, name="layer_a_out", grid=(s // tm,),
        in_specs=[_row_spec(tm, 4 * d), _row_spec(tm, d), _full_spec((8, d)), _full_spec((d, d)),
                  _full_spec((1, d)), _full_spec((1, d)), _full_spec((d, 2 * a)), _full_spec((d, 2 * a))],
        out_specs=[_row_spec(tm, d), _row_spec(tm, 2 * a), _row_spec(tm, a), _row_spec(tm, a)],
        out_shape=[jax.ShapeDtypeStruct((s, d), _F32), jax.ShapeDtypeStruct((s, 2 * a), _MXU_DTYPE),
                   jax.ShapeDtypeStruct((s, a), _MXU_DTYPE), jax.ShapeDtypeStruct((s, a), _F32)],
        scratch_shapes=[pltpu.VMEM((8, d), _F32)],
        compiler_params=_params(1),
    )(u, x, conv, w_out_a, gain_kv, gain_b, w_kv, w_in_b)


def _neg_softplus(z):
    return -(jnp.maximum(z, 0.0) + jnp.log(1.0 + jnp.exp(-jnp.abs(z))))


def _scan_dot(val, tri):
    return _dot(_mx(val), tri)


def _head_masks():
    lane = lax.broadcasted_iota(jnp.int32, (1, LANES), 1)
    return [lane < HEAD_DIM, lane >= HEAD_DIM]


def _stacked_causal(tq):
    row = lax.broadcasted_iota(jnp.int32, (2 * tq, tq), 0)
    col = lax.broadcasted_iota(jnp.int32, (2 * tq, tq), 1)
    return col < jnp.where(row >= tq, row - tq, row)


def _attn_fwd(q, kv, tri_suffix):
    s, a = q.shape
    n_hp = a // LANES
    tq = min(ATT_TILE, s)
    nq = s // tq
    per = min(ATT_TILES_PER_STEP, nq)

    def body(q_ref, k_ref, v_ref, tri_ref, o_ref, c_ref, stop_ref):
        hp = pl.program_id(0)
        tri = tri_ref[...]
        causal = _stacked_causal(tq)
        first = _head_masks()[0]

        def block(qs, j, c, acc, diag=False, live=None):
            off = pl.multiple_of(j * tq, tq)
            kb = k_ref[pl.ds(off, tq), :]
            vb = v_ref[pl.ds(off, tq), :]
            z = _dot_nt(qs, kb)
            lg = _neg_softplus(z)
            if diag:
                lg = jnp.where(causal, lg, 0.0)
            tot = jnp.sum(lg, axis=1, keepdims=True)
            c_in = c
            if live is not None:
                c_in = c + (live - 1.0) * 1e30
                tot = tot * live
            w = jnp.exp(z + _scan_dot(lg, tri) + c_in)
            if diag:
                w = jnp.where(causal, w, 0.0)
            return c + tot, acc + _dot(_mx(w), vb)

        tiles = []
        for t in range(per):
            i = pl.program_id(1) * per + t
            qv = q_ref[t * tq:(t + 1) * tq, :]
            qs = jnp.concatenate([jnp.where(hmask, qv, jnp.zeros_like(qv)) for hmask in _head_masks()], axis=0)
            c, acc = block(qs, i, jnp.zeros((2 * tq, 1), _F32), jnp.zeros((2 * tq, LANES), _F32), diag=True)
            c, acc = block(qs, jnp.maximum(i - 1, 0), c, acc, live=jnp.where(i >= 1, 1.0, 0.0))
            tiles.append((i, qs, c, acc))
        for t, (i, qs, c, acc) in enumerate(tiles):
            def cond(carry):
                return jnp.logical_and(carry[0] >= 0, jnp.max(carry[1]) > SKIP_LOG)

            def step(carry, qs=qs):
                c, acc = block(qs, carry[0], carry[1], carry[2])
                return carry[0] - 1, c, acc

            j, c, acc = lax.while_loop(cond, step, (i - 2, c, acc))
            stop_ref[hp, i] = jnp.maximum(jnp.minimum(j + 1, i - 1), 0)
            o_ref[t * tq:(t + 1) * tq, :] = jnp.where(first, acc[:tq], acc[tq:])
            c_ref[t * tq:(t + 1) * tq, :] = jnp.where(first, c[:tq], c[tq:])

    q_spec = pl.BlockSpec((per * tq, LANES), lambda h, i: (i, h))
    return pl.pallas_call(
        body, name="attn_fwd", grid=(n_hp, nq // per),
        in_specs=[q_spec, pl.BlockSpec((s, LANES), lambda h, i: (0, h)),
                  pl.BlockSpec((s, LANES), lambda h, i: (0, n_hp + h)), _full_spec((tq, tq))],
        out_specs=[q_spec, q_spec, pl.BlockSpec(memory_space=pltpu.SMEM)],
        out_shape=[jax.ShapeDtypeStruct((s, a), _F32), jax.ShapeDtypeStruct((s, a), _F32),
                   jax.ShapeDtypeStruct((n_hp, nq), jnp.int32)],
        compiler_params=_params(2),
    )(q, kv, kv, tri_suffix)


def _silu_and_grad(g):
    sg = _sigmoid(g)
    return g * sg, sg * (1.0 + g * (1.0 - sg))


def _layer_b_out_loss_bwd(o, gate, x1, w_out_b, gain_f, target):
    s, d = x1.shape
    a = o.shape[1]
    tm = min(BIG_ROW_TILE, s)
    last = s // tm - 1

    def body(o_ref, gate_ref, x1_ref, wob_ref, gf_ref, tgt_ref,
             dx2_ref, do_ref, dgate_ref, gw_ref, gnf_ref, loss_ref, gw_acc):
        step = pl.program_id(0)
        _zero_at_first(step, gw_acc, gnf_ref, loss_ref)
        sl, dsl = _silu_and_grad(gate_ref[...])
        ov = o_ref[...]
        ob = _mx(ov * sl)
        wob = wob_ref[...]
        x2 = x1_ref[...] + _dot(ob, wob)
        r, xh = _rms_stats(x2)
        gf = gf_ref[...]
        err = xh * gf - tgt_ref[...]
        part = jnp.sum(jnp.sum(err * err, axis=1, keepdims=True), axis=0, keepdims=True)
        loss_ref[...] += part * (0.5 / d)
        dy = err * (1.0 / d)
        gnf_ref[...] += jnp.sum(dy * xh, axis=0, keepdims=True)
        dx2 = _rms_bwd(r, xh, dy * gf)
        dx2_ref[...] = dx2
        dxb = _mx(dx2)
        d_ob = _dot_nt(dxb, wob)
        gw_acc[...] += _dot_tn(ob, dxb)
        do_ref[...] = (d_ob * sl).astype(do_ref.dtype)
        dgate_ref[...] = (d_ob * ov * dsl).astype(dgate_ref.dtype)
        _emit_at_last(step, last, [(gw_acc, gw_ref)])

    return pl.pallas_call(
        body, name="layer_b_out_loss_bwd", grid=(s // tm,),
        in_specs=[_row_spec(tm, a), _row_spec(tm, a), _row_spec(tm, d), _full_spec((a, d)), _full_spec((1, d)),
                  _row_spec(tm, d)],
        out_specs=[_row_spec(tm, d), _row_spec(tm, a), _row_spec(tm, a), _full_spec((a, d)), _full_spec((1, d)),
                   _full_spec((1, LANES))],
        out_shape=[jax.ShapeDtypeStruct((s, d), _F32), jax.ShapeDtypeStruct((s, a), _MXU_DTYPE),
                   jax.ShapeDtypeStruct((s, a), _MXU_DTYPE), jax.ShapeDtypeStruct((a, d), _MXU_DTYPE),
                   jax.ShapeDtypeStruct((1, d), _F32), jax.ShapeDtypeStruct((1, LANES), _F32)],
        scratch_shapes=[pltpu.VMEM((a, d), _F32)],
        compiler_params=_params(1),
    )(o, gate, x1, w_out_b, gain_f, target)


def _attn_bwd(q, kv, do, c_tot, stop, tri_suffix, tri_prefix):
    s, a = q.shape
    n_hp = a // LANES
    tq = min(ATT_TILE, s)
    nq = s // tq
    per = min(ATT_TILES_PER_STEP, nq)
    scale = HEAD_DIM ** -0.5
    chunk = min(1024, s)

    def body(stop_ref, q_ref, do_ref, c_ref, k_ref, v_ref, ts_ref, tp_ref, dq_ref, dk_hbm, dv_hbm,
             dk_acc, dv_acc, stage, sem):
        hp, step = pl.program_id(0), pl.program_id(1)
        _zero_at_first(step, dk_acc, dv_acc)
        ts, tp = ts_ref[...], tp_ref[...]
        causal = _stacked_causal(tq)
        masks = _head_masks()

        def block(tile, j, st, diag=False, live=None, parts=None):
            qs, dos, cs = tile
            asc, pre, dq = st
            rows = pl.ds(pl.multiple_of(j * tq, tq), tq)
            kb = k_ref[rows, :]
            vb = v_ref[rows, :]
            z = _dot_nt(qs, kb)
            lg = _neg_softplus(z)
            sig = jnp.exp(z + lg)
            if diag:
                lg = jnp.where(causal, lg, 0.0)
            tot = jnp.sum(lg, axis=1, keepdims=True)
            newer = jnp.zeros_like(tot) if diag else cs - asc - tot
            if live is not None:
                newer = newer + (live - 1.0) * 1e30
                tot = tot * live
            wgt = jnp.exp(z + _scan_dot(lg, ts) + newer)
            if diag:
                wgt = jnp.where(causal, wgt, 0.0)
            g = wgt * _dot_nt(dos, vb)
            dz = g - sig * (_scan_dot(g, tp) + pre)
            if diag:
                dz = jnp.where(causal, dz, 0.0)
            dzb = _mx(dz)
            dk_part, dv_part = _dot_tn(dzb, qs), _dot_tn(_mx(wgt), dos)
            if parts is None:
                dk_acc[rows, :] += dk_part
                dv_acc[rows, :] += dv_part
            else:
                parts.append((dk_part, dv_part))
            return asc + tot, pre + jnp.sum(g, axis=1, keepdims=True), dq + _dot(dzb, kb)

        tiles = []
        for t in range(per):
            i = step * per + t
            sl = slice(t * tq, (t + 1) * tq)
            qv, dov, cv = q_ref[sl, :], do_ref[sl, :], c_ref[sl, :]
            tile = (jnp.concatenate([jnp.where(hmask, qv, jnp.zeros_like(qv)) for hmask in masks], axis=0),
                    jnp.concatenate([jnp.where(hmask, dov, jnp.zeros_like(dov)) for hmask in masks], axis=0),
                    jnp.concatenate([cv[:, 0:1], cv[:, HEAD_DIM:HEAD_DIM + 1]], axis=0))
            first = jnp.clip(stop_ref[hp, i], 0, jnp.maximum(i - 1, 0))
            st = (jnp.zeros((2 * tq, 1), _F32), jnp.zeros((2 * tq, 1), _F32), jnp.zeros((2 * tq, LANES), _F32))
            st = lax.fori_loop(first, i - 1, lambda j, st, tile=tile: block(tile, j, st), st)
            tiles.append((i, sl, tile, st))
        parts = []
        for i, sl, tile, st in tiles:
            st = block(tile, jnp.maximum(i - 1, 0), st, live=jnp.where(i >= 1, 1.0, 0.0), parts=parts)
            dq = block(tile, i, st, diag=True, parts=parts)[2]
            dq_ref[sl, :] = (jnp.where(masks[0], dq[:tq], dq[tq:]) * scale).astype(dq_ref.dtype)
        for u in range(per + 1):
            terms = ([parts[2 * u - 1]] if u >= 1 else []) + ([parts[2 * u]] if u < per else [])
            dk_sum, dv_sum = terms[0] if len(terms) == 1 else (terms[0][0] + terms[1][0], terms[0][1] + terms[1][1])
            rows = pl.ds(pl.multiple_of(jnp.maximum(step * per + u - 1, 0) * tq, tq), tq)
            dk_acc[rows, :] += dk_sum
            dv_acc[rows, :] += dv_sum

        @pl.when(step == nq // per - 1)
        def _():
            cols = pl.ds(pl.multiple_of(hp * LANES, LANES), LANES)
            for acc, out in ((dk_acc, dk_hbm), (dv_acc, dv_hbm)):
                def cast(n, carry, acc=acc):
                    rows = pl.ds(pl.multiple_of(n * chunk, chunk), chunk)
                    stage[rows, :] = acc[rows, :].astype(stage.dtype)
                    return carry
                lax.fori_loop(0, s // chunk, cast, 0)
                cp = pltpu.make_async_copy(stage, out.at[:, cols], sem)
                cp.start()
                cp.wait()

    q_spec = pl.BlockSpec((per * tq, LANES), lambda h, i, *_: (i, h))
    grid_spec = pltpu.PrefetchScalarGridSpec(
        num_scalar_prefetch=1, grid=(n_hp, nq // per),
        in_specs=[q_spec, q_spec, q_spec, pl.BlockSpec((s, LANES), lambda h, i, *_: (0, h)),
                  pl.BlockSpec((s, LANES), lambda h, i, *_: (0, n_hp + h)),
                  pl.BlockSpec((tq, tq), lambda h, i, *_: (0, 0)), pl.BlockSpec((tq, tq), lambda h, i, *_: (0, 0))],
        out_specs=[q_spec, pl.BlockSpec(memory_space=pl.ANY), pl.BlockSpec(memory_space=pl.ANY)],
        scratch_shapes=[pltpu.VMEM((s, LANES), _F32), pltpu.VMEM((s, LANES), _F32), pltpu.VMEM((s, LANES), _MXU_DTYPE),
                        pltpu.SemaphoreType.DMA],
    )
    return pl.pallas_call(
        body, name="attn_bwd", grid_spec=grid_spec,
        out_shape=[jax.ShapeDtypeStruct((s, a), _MXU_DTYPE)] * 3,
        compiler_params=_params(2),
    )(stop, q, do, c_tot, kv, kv, tri_suffix, tri_prefix)


def _proj_bwd(x1, dx2, dq, dgate, dk, dv, w_in_b, w_kv, gain_b, gain_kv):
    s, d = x1.shape
    a = dq.shape[1]
    tm = min(BIG_ROW_TILE, s)

    def body(x1_ref, dx2_ref, dq_ref, dgate_ref, dk_ref, dv_ref, wib_ref, wkv_ref, gb_ref, gkv_ref,
             dx1_ref, gwib_out, gwkv_out, gnb_ref, gnkv_ref, gwib_ref, gwkv_ref):
        step = pl.program_id(0)
        _zero_at_first(step, gwib_ref, gwkv_ref, gnb_ref, gnkv_ref)
        r, xh = _rms_stats(x1_ref[...])
        gb, gkv = gb_ref[...], gkv_ref[...]
        hb, hk = _mx(xh * gb), _mx(xh * gkv)
        d_qg = jnp.concatenate([dq_ref[...], dgate_ref[...]], axis=1)
        d_kv = jnp.concatenate([dk_ref[...], dv_ref[...]], axis=1)
        gwib_ref[...] += _dot_tn(hb, d_qg)
        gwkv_ref[...] += _dot_tn(hk, d_kv)
        d_hb = _dot_nt(d_qg, wib_ref[...])
        d_hk = _dot_nt(d_kv, wkv_ref[...])
        gnb_ref[...] += jnp.sum(d_hb * xh, axis=0, keepdims=True)
        gnkv_ref[...] += jnp.sum(d_hk * xh, axis=0, keepdims=True)
        dx1_ref[...] = dx2_ref[...] + _rms_bwd(r, xh, d_hb * gb + d_hk * gkv)
        _emit_at_last(step, s // tm - 1, [(gwib_ref, gwib_out), (gwkv_ref, gwkv_out)])

    return pl.pallas_call(
        body, name="proj_bwd", grid=(s // tm,),
        in_specs=[_row_spec(tm, d), _row_spec(tm, d)] + [_row_spec(tm, a)] * 4
        + [_full_spec((d, 2 * a)), _full_spec((d, 2 * a)), _full_spec((1, d)), _full_spec((1, d))],
        out_specs=[_row_spec(tm, d), _full_spec((d, 2 * a)), _full_spec((d, 2 * a)), _full_spec((1, d)),
                   _full_spec((1, d))],
        out_shape=[jax.ShapeDtypeStruct((s, d), _F32), jax.ShapeDtypeStruct((d, 2 * a), _MXU_DTYPE),
                   jax.ShapeDtypeStruct((d, 2 * a), _MXU_DTYPE), jax.ShapeDtypeStruct((1, d), _F32),
                   jax.ShapeDtypeStruct((1, d), _F32)],
        scratch_shapes=[pltpu.VMEM((d, 2 * a), _F32), pltpu.VMEM((d, 2 * a), _F32)],
        compiler_params=_params(1),
    )(x1, dx2, dq, dgate, dk, dv, w_in_b, w_kv, gain_b, gain_kv)


def _layer_a_bwd(u, dx1, conv, w_out_a):
    s, d = dx1.shape
    tm = min(BIG_ROW_TILE, s)
    n = s // tm
    per8 = tm // 8

    def body(u_ref, uprev_ref, dx1_ref, conv_ref, woa_ref, du_ref, gwoa_out, gconv_ref, halo_ref, gwoa_ref):
        step = pl.program_id(0)
        _zero_at_first(step, gwoa_ref, gconv_ref, halo_ref)
        b, c, xin, g = (u_ref[:, k * d:(k + 1) * d] for k in range(4))
        p = c * xin
        before = uprev_ref[:, d:2 * d] * uprev_ref[:, 2 * d:3 * d]
        before = jnp.where(step == n - 1, jnp.zeros_like(before), before)
        p1, p2 = _shift_rows_down(p, before, tm)
        w = conv_ref[...]
        cv = w[0:1, :] * p2 + w[1:2, :] * p1 + w[2:3, :] * p
        sl, dsl = _silu_and_grad(g)
        y = b * cv
        dxb = _mx(dx1_ref[...])
        gwoa_ref[...] += _dot_tn(_mx(y * sl), dxb)
        d_ya = _dot_nt(dxb, woa_ref[...])
        d_y = d_ya * sl
        d_cv = d_y * b
        n1, n2 = _shift_rows_up(d_cv, halo_ref[...], tm)
        halo_ref[...] = d_cv[0:8, :]
        d_p = w[2:3, :] * d_cv + w[1:2, :] * n1 + w[0:1, :] * n2
        gconv_ref[0:1, :] += jnp.sum(d_cv * p2, axis=0, keepdims=True)
        gconv_ref[1:2, :] += jnp.sum(d_cv * p1, axis=0, keepdims=True)
        gconv_ref[2:3, :] += jnp.sum(d_cv * p, axis=0, keepdims=True)
        du_ref[:, 0:d] = (d_y * cv).astype(du_ref.dtype)
        du_ref[:, d:2 * d] = (d_p * xin).astype(du_ref.dtype)
        du_ref[:, 2 * d:3 * d] = (d_p * c).astype(du_ref.dtype)
        du_ref[:, 3 * d:4 * d] = (d_ya * y * dsl).astype(du_ref.dtype)
        _emit_at_last(step, n - 1, [(gwoa_ref, gwoa_out)])

    def rev(i):
        return (n - 1 - i, 0)

    return pl.pallas_call(
        body, name="layer_a_bwd", grid=(n,),
        in_specs=[pl.BlockSpec((tm, 4 * d), rev),
                  pl.BlockSpec((8, 4 * d), lambda i: (jnp.maximum((n - 1 - i) * per8 - 1, 0), 0)),
                  pl.BlockSpec((tm, d), rev), _full_spec((8, d)), _full_spec((d, d))],
        out_specs=[pl.BlockSpec((tm, 4 * d), rev), _full_spec((d, d)), _full_spec((8, d))],
        out_shape=[jax.ShapeDtypeStruct((s, 4 * d), _MXU_DTYPE), jax.ShapeDtypeStruct((d, d), _MXU_DTYPE),
                   jax.ShapeDtypeStruct((8, d), _F32)],
        scratch_shapes=[pltpu.VMEM((8, d), _F32), pltpu.VMEM((d, d), _F32)],
        compiler_params=_params(1),
    )(u, u, dx1, conv, w_out_a)


def _grad_w_in_a(h, du, nb, grads, axes):
    s, d = h.shape
    bn = du.shape[1] // nb
    tm = min(4 * BIG_ROW_TILE, s)
    half = nb // 2
    n = len(grads)
    steps = s // tm

    def body(h_ref, du_ref, *refs):
        gw_out, gw_ref = refs[n], refs[-1]
        start, wait = _exchange_ops(refs[:n], refs[n + 1:2 * n + 1], *refs[2 * n + 1:-1], True, axes)
        jh, i = pl.program_id(0), pl.program_id(1)
        pl.when(jnp.logical_and(jh == 0, i == 0))(start)
        _zero_at_first(i, gw_ref)
        hv = h_ref[...]
        for j in range(half):
            gw_ref[j] += _dot_tn(hv, du_ref[:, j * bn:(j + 1) * bn])
        _emit_at_last(i, steps - 1, [(gw_ref, gw_out)])
        pl.when(jnp.logical_and(jh == 1, i == steps - 1))(wait)

    outs = pl.pallas_call(
        body, name="grad_w_in_a", grid=(2, steps),
        in_specs=[pl.BlockSpec((tm, d), lambda jh, i: (i, 0)), pl.BlockSpec((tm, half * bn), lambda jh, i: (i, jh))]
        + [_ANY] * n,
        out_specs=[pl.BlockSpec((half, d, bn), lambda jh, i: (jh, 0, 0))] + [_ANY] * n,
        out_shape=[jax.ShapeDtypeStruct((nb, d, bn), _MXU_DTYPE)] + _exchange_shapes(grads, True, axes),
        scratch_shapes=_exchange_sems(n) + [pltpu.VMEM((half, d, bn), _F32)],
        compiler_params=_params(2),
    )(h, du, *grads)
    return outs[0], outs[1:]


def _input_grad(x, dx1, du, w_in, gain):
    s, d = x.shape
    f = w_in.shape[1]
    tm = min(BIG_ROW_TILE, s)

    def body(x_ref, dx1_ref, du_ref, w_ref, g_ref, dx_ref, gn_ref):
        _zero_at_first(pl.program_id(0), gn_ref)
        r, xh = _rms_stats(x_ref[...])
        d_h = _dot_nt(du_ref[...], w_ref[...])
        gn_ref[...] += jnp.sum(d_h * xh, axis=0, keepdims=True)
        dx_ref[...] = dx1_ref[...] + _rms_bwd(r, xh, d_h * g_ref[...])

    return pl.pallas_call(
        body, name="input_grad", grid=(s // tm,),
        in_specs=[_row_spec(tm, d), _row_spec(tm, d), _row_spec(tm, f), _full_spec((d, f)), _full_spec((1, d))],
        out_specs=[_row_spec(tm, d), _full_spec((1, d))],
        out_shape=[jax.ShapeDtypeStruct((s, d), _F32), jax.ShapeDtypeStruct((1, d), _F32)],
        compiler_params=_params(1),
    )(x, dx1, du, w_in, gain)


def _reduce_adamw(name, parts, w, m, v):
    rows, cols = w.shape
    n_parts = parts.shape[0]
    tr = min(256, rows)
    c1 = 1.0 - ADAM_B1 ** ADAM_STEP
    c2 = 1.0 - ADAM_B2 ** ADAM_STEP

    def body(p_ref, w_ref, m_ref, v_ref, g_ref, d_ref, nm_ref, nv_ref):
        g = p_ref[0].astype(_F32)
        for k in range(1, n_parts):
            g = g + p_ref[k].astype(_F32)
        nm = ADAM_B1 * m_ref[...] + (1.0 - ADAM_B1) * g
        nv = ADAM_B2 * v_ref[...] + (1.0 - ADAM_B2) * (g * g)
        g_ref[...] = g
        nm_ref[...] = nm
        nv_ref[...] = nv
        d_ref[...] = -ADAM_LR * ((nm / c1) / (jnp.sqrt(nv / c2) + ADAM_EPS) + ADAM_WD * w_ref[...])

    tile = _row_spec(tr, cols)
    return pl.pallas_call(
        body, name=name, grid=(rows // tr,),
        in_specs=[pl.BlockSpec((n_parts, tr, cols), lambda i: (0, i, 0)), tile, tile, tile],
        out_specs=[tile] * 4,
        out_shape=[jax.ShapeDtypeStruct((rows, cols), _F32)] * 4,
        compiler_params=_params(1),
    )(parts, w, m, v)


def _pad_rows(a, rows=8):
    return jnp.pad(a, ((0, rows - a.shape[0]), (0, 0)))


def kernel(x, norm_a, w_in_a, conv_a, w_out_a, norm_kv, w_kv, norm_b, w_in_b, w_out_b, norm_f, loss_target, m_norm_a, m_w_in_a, m_conv_a, m_w_out_a, m_norm_kv, m_w_kv, m_norm_b, m_w_in_b, m_w_out_b, m_norm_f, v_norm_a, v_w_in_a, v_conv_a, v_w_out_a, v_norm_kv, v_w_kv, v_norm_b, v_w_in_b, v_w_out_b, v_norm_f):
    x0 = x[0]
    s, d = x0.shape
    a = d // 2
    sh = d // N_DEV
    me = 4 * lax.axis_index("x") + 2 * lax.axis_index("y") + lax.axis_index("c")

    small_a = _pad_rows(jnp.concatenate([norm_a, conv_a[0]], axis=0))
    wia_g, small_g = _gather_two_level("exchange_gather", [_mx(w_in_a[0]), small_a], axes=[1, 1])
    small_f = small_g.reshape(8, d)
    gain_a = small_f[0:1]
    conv_f = _pad_rows(small_f[1:4])
    gain_kv, gain_b, gain_f = norm_kv.reshape(1, d), norm_b.reshape(1, d), norm_f.reshape(1, d)

    wia_f = wia_g.reshape(d, 4 * d)
    u, h, (woa_g, wkv_g, wib_g, wob_g) = _ln_matmul_in_a(
        x0, gain_a, wia_f, [_mx(w_out_a[0]), _mx(w_kv), _mx(w_in_b[0]), _mx(w_out_b[0])], axes=[0, 0, 0, 1])
    woa_f = woa_g.reshape(d, d)
    wkv_f = wkv_g.reshape(d, 2 * a)
    wib_f = wib_g.reshape(d, 2 * a)
    wob_f = wob_g.reshape(a, d)
    x1, kv, q, gate = _layer_a_out(u, x0, conv_f, woa_f, gain_kv, gain_b, wkv_f, wib_f)
    tq = min(ATT_TILE, s)
    idx = jnp.arange(tq)
    tri_suffix = _mx(idx[:, None] >= idx[None, :])
    tri_prefix = _mx(idx[:, None] <= idx[None, :])
    o, c_tot, stop = _attn_fwd(q, kv, tri_suffix)
    dx2, do, dgate, g_wob, g_norm_f, loss_part = _layer_b_out_loss_bwd(o, gate, x1, wob_f, gain_f, loss_target[0])

    dq, dk, dv = _attn_bwd(q, kv, do, c_tot, stop, tri_suffix, tri_prefix)
    dx1, g_wib, g_wkv, g_norm_b, g_norm_kv = _proj_bwd(x1, dx2, dq, dgate, dk, dv, wib_f, wkv_f, gain_b, gain_kv)
    du, g_woa, g_conv = _layer_a_bwd(u, dx1, conv_f, woa_f)
    dx0, g_norm_a = _input_grad(x0, dx1, du, wia_f, gain_a)

    g_wia, (p_woa, p_wkv, p_wib, p_wob) = _grad_w_in_a(
        h, du, N_DEV,
        [g_woa.reshape(N_DEV, sh, d), g_wkv.reshape(N_DEV, sh, 2 * a), g_wib.reshape(N_DEV, sh, 2 * a),
         g_wob.reshape(a, N_DEV, sh).transpose(1, 0, 2)], axes=[0, 0, 0, 0])
    p_wia = _reduce_scatter_two_level("exchange_scatter", g_wia)
    small_grads = jnp.concatenate(
        [g_norm_a, g_conv[0:3], g_norm_kv, g_norm_b, g_norm_f, jnp.pad(loss_part, ((0, 0), (0, d - LANES)))], axis=0)
    (p_small,) = _exchange("exchange_small", [small_grads], scatter=False, axes=[0])

    upd_wia = _reduce_adamw("adamw_w_in_a", p_wia, w_in_a[0], m_w_in_a[0], v_w_in_a[0])
    upd_woa = _reduce_adamw("adamw_w_out_a", p_woa, w_out_a[0], m_w_out_a[0], v_w_out_a[0])
    upd_wkv = _reduce_adamw("adamw_w_kv", p_wkv, w_kv, m_w_kv, v_w_kv)
    upd_wib = _reduce_adamw("adamw_w_in_b", p_wib, w_in_b[0], m_w_in_b[0], v_w_in_b[0])
    upd_wob = _reduce_adamw("adamw_w_out_b", p_wob, w_out_b[0], m_w_out_b[0], v_w_out_b[0])

    def rep(a1, a2, a3):
        return jnp.concatenate([jnp.zeros((4, d), _F32), a1.reshape(1, d), a2.reshape(1, d), a3.reshape(1, d),
                                jnp.zeros((1, d), _F32)], axis=0)

    upd_rep = _reduce_adamw("adamw_replicated", p_small, rep(norm_kv, norm_b, norm_f),
                            rep(m_norm_kv, m_norm_b, m_norm_f), rep(v_norm_kv, v_norm_b, v_norm_f))

    def mine(n1, cv1):
        return _pad_rows(jnp.concatenate([n1, cv1[0]], axis=0))

    p_mine = lax.dynamic_slice(p_small, (0, 0, me * sh), (N_DEV, 8, sh))
    upd_mine = _reduce_adamw("adamw_sharded_small", p_mine, mine(norm_a, conv_a), mine(m_norm_a, m_conv_a),
                             mine(v_norm_a, v_conv_a))

    loss = upd_rep[0][7, 0]
    groups = []
    for k in range(4):
        groups.append([
            upd_mine[k][0:1], upd_wia[k][None], upd_mine[k][1:4][None], upd_woa[k][None], upd_rep[k][4],
            upd_wkv[k], upd_rep[k][5:6], upd_wib[k][None], upd_wob[k][None], upd_rep[k][6]])
    return (loss, dx0[None], *groups[0], *groups[1], *groups[2], *groups[3])
```

```python
import jax
import jax.numpy as jnp
from jax import lax
from jax.experimental import pallas as pl
from jax.experimental.pallas import tpu as pltpu

_MXU_DTYPE = jnp.bfloat16
_F32 = jnp.float32

RMS_EPS = 1e-6
HEAD_DIM = 64
LANES = 128
N_DEV = 8
ATT_TILE = 256
ATT_TILES_PER_STEP = 8
ROW_TILE = 256
BIG_ROW_TILE = 512
VMEM_LIMIT = 60 * 1024 * 1024
SKIP_LOG = -110.0

ADAM_LR = 0.001
ADAM_B1 = 0.9
ADAM_B2 = 0.999
ADAM_EPS = 1e-08
ADAM_WD = 0.01
ADAM_STEP = 10

_NT = (((1,), (1,)), ((), ()))
_TN = (((0,), (0,)), ((), ()))


def _params(n_grid):
    return pltpu.CompilerParams(dimension_semantics=("arbitrary",) * n_grid, vmem_limit_bytes=VMEM_LIMIT)


def _dot(a, b):
    return jnp.dot(a, b, preferred_element_type=_F32)


def _dot_nt(a, b):
    return lax.dot_general(a, b, _NT, preferred_element_type=_F32)


def _dot_tn(a, b):
    return lax.dot_general(a, b, _TN, preferred_element_type=_F32)


def _mx(a):
    return a.astype(_MXU_DTYPE)


def _sigmoid(a):
    return 1.0 / (1.0 + jnp.exp(-a))


def _rms_stats(xv):
    r = lax.rsqrt(jnp.mean(xv * xv, axis=-1, keepdims=True) + RMS_EPS)
    return r, xv * r


def _rms_bwd(r, xh, dyg):
    return r * (dyg - xh * jnp.mean(dyg * xh, axis=-1, keepdims=True))


def _row_spec(tm, width):
    return pl.BlockSpec((tm, width), lambda i: (i, 0))


def _full_spec(shape):
    zeros = (0,) * len(shape)
    return pl.BlockSpec(shape, lambda *_: zeros)


def _zero_at_first(step, *refs):
    @pl.when(step == 0)
    def _():
        for ref in refs:
            ref[...] = jnp.zeros_like(ref)


def _emit_at_last(step, last, pairs):
    @pl.when(step == last)
    def _():
        for acc, out in pairs:
            out[...] = acc[...].astype(out.dtype)


def _slot(ref, idx, axis):
    if axis == 0:
        return ref.at[idx]
    width = ref.shape[1] // N_DEV
    return ref.at[:, pl.ds(pl.multiple_of(idx * width, width), width)]


def _exchange_ops(ins, outs, send_sems, recv_sems, local_sems, scatter, axes):
    n = len(ins)
    x, y, c = lax.axis_index("x"), lax.axis_index("y"), lax.axis_index("c")
    me = 4 * x + 2 * y + c

    def remote(t, m, landed):
        px = 1 - x if m & 4 else x
        py = 1 - y if m & 2 else y
        pc = 1 - c if m & 1 else c
        idx = 4 * px + 2 * py + pc
        k = t * (N_DEV - 1) + m - 1
        return pltpu.make_async_remote_copy(
            src_ref=_slot(ins[t], idx, axes[t]) if scatter else ins[t],
            dst_ref=_slot(outs[t], idx if landed else me, axes[t]),
            send_sem=send_sems.at[k], recv_sem=recv_sems.at[k],
            device_id=(px, py, pc), device_id_type=pl.DeviceIdType.MESH)

    def local(t):
        return pltpu.make_async_copy(_slot(ins[t], me, axes[t]) if scatter else ins[t], _slot(outs[t], me, axes[t]),
                                     local_sems.at[t])

    def start():
        for t in range(n):
            local(t).start()
        for m in range(1, N_DEV):
            for t in range(n):
                remote(t, m, False).start()

    def wait():
        for m in range(1, N_DEV):
            for t in range(n):
                remote(t, m, True).wait_recv()
        for m in range(1, N_DEV):
            for t in range(n):
                remote(t, m, False).wait_send()
        for t in range(n):
            local(t).wait()

    return start, wait


def _exchange_shapes(arrays, scatter, axes):
    if scatter:
        return [jax.ShapeDtypeStruct(a.shape, a.dtype) for a in arrays]
    return [jax.ShapeDtypeStruct((N_DEV,) + a.shape if ax == 0 else (a.shape[0], N_DEV * a.shape[1]), a.dtype)
            for a, ax in zip(arrays, axes)]


def _exchange_sems(n):
    return [pltpu.SemaphoreType.DMA((n * (N_DEV - 1),)), pltpu.SemaphoreType.DMA((n * (N_DEV - 1),)),
            pltpu.SemaphoreType.DMA((n,))]


_ANY = pl.BlockSpec(memory_space=pl.ANY)


def _exchange(name, arrays, scatter, axes):
    n = len(arrays)

    def body(*refs):
        start, wait = _exchange_ops(refs[:n], refs[n:2 * n], *refs[2 * n:], scatter, axes)
        start()
        wait()

    return pl.pallas_call(
        body, name=name, in_specs=[_ANY] * n, out_specs=[_ANY] * n,
        out_shape=_exchange_shapes(arrays, scatter, axes), scratch_shapes=_exchange_sems(n),
    )(*arrays)


def _gather_two_level(name, arrays, axes):
    n = len(arrays)

    def body(*refs):
        ins, outs = refs[:n], refs[n:2 * n]
        send_sems, recv_sems, local_sems = refs[2 * n:]
        x, y, c = lax.axis_index("x"), lax.axis_index("y"), lax.axis_index("c")
        me, sibling = (x, y, c), (x, y, 1 - c)
        chips = [(1 - x, y), (x, 1 - y), (1 - x, 1 - y)]

        def rows(t, dev):
            return _slot(outs[t], 4 * dev[0] + 2 * dev[1] + dev[2], axes[t])

        def copy(t, k, block, to, src=None):
            return pltpu.make_async_remote_copy(
                src_ref=rows(t, block) if src is None else src, dst_ref=rows(t, block),
                send_sem=send_sems.at[t * (N_DEV - 1) + k], recv_sem=recv_sems.at[t * (N_DEV - 1) + k],
                device_id=to, device_id_type=pl.DeviceIdType.MESH)

        sent = []
        for t in range(n):
            pltpu.make_async_copy(ins[t], rows(t, me), local_sems.at[t]).start()
            sent.append(copy(t, 0, me, sibling, src=ins[t]))
            sent += [copy(t, 1 + j, me, (*chip, c), src=ins[t]) for j, chip in enumerate(chips)]
        for cp in sent:
            cp.start()
        for j, chip in enumerate(chips):
            for t in range(n):
                copy(t, 1 + j, (*chip, c), me).wait_recv()
                passed = copy(t, 4 + j, (*chip, c), sibling)
                passed.start()
                sent.append(passed)
        for t in range(n):
            copy(t, 0, sibling, me).wait_recv()
            for j, chip in enumerate(chips):
                copy(t, 4 + j, (*chip, 1 - c), me).wait_recv()
        for cp in sent:
            cp.wait_send()
        for t in range(n):
            pltpu.make_async_copy(ins[t], rows(t, me), local_sems.at[t]).wait()

    return pl.pallas_call(
        body, name=name, in_specs=[_ANY] * n, out_specs=[_ANY] * n,
        out_shape=_exchange_shapes(arrays, False, axes), scratch_shapes=_exchange_sems(n),
    )(*arrays)


def _reduce_scatter_two_level(name, g):
    n_chip = N_DEV // 2
    _, r, cc = g.shape
    chunk = min(256, r)

    def body(g_ref, out_ref, mine, theirs, total, load_sems, pair_send, pair_recv, chip_send, chip_recv, keep_sem):
        x, y, c = lax.axis_index("x"), lax.axis_index("y"), lax.axis_index("c")
        my_chip = 2 * x + y

        def load(q):
            return pltpu.make_async_copy(g_ref.at[2 * q + c], mine.at[q], load_sems.at[q])

        def swap(q):
            return pltpu.make_async_remote_copy(
                src_ref=g_ref.at[2 * q + 1 - c], dst_ref=theirs.at[q], send_sem=pair_send.at[q],
                recv_sem=pair_recv.at[q], device_id=(x, y, 1 - c), device_id_type=pl.DeviceIdType.MESH)

        for q in range(n_chip):
            load(q).start()
            swap(q).start()
        for q in range(n_chip):
            load(q).wait()
            swap(q).wait_recv()

            def add(i, carry, q=q):
                rows = pl.ds(pl.multiple_of(i * chunk, chunk), chunk)
                total[q, rows, :] = (mine[q, rows, :].astype(_F32) + theirs[q, rows, :].astype(_F32)).astype(total.dtype)
                return carry
            lax.fori_loop(0, r // chunk, add, 0)

        def send(k, landed):
            px = 1 - x if k in (0, 2) else x
            py = 1 - y if k in (1, 2) else y
            peer_chip = 2 * px + py
            return pltpu.make_async_remote_copy(
                src_ref=total.at[peer_chip], dst_ref=out_ref.at[peer_chip if landed else my_chip],
                send_sem=chip_send.at[k], recv_sem=chip_recv.at[k],
                device_id=(px, py, c), device_id_type=pl.DeviceIdType.MESH)

        keep = pltpu.make_async_copy(total.at[my_chip], out_ref.at[my_chip], keep_sem)
        keep.start()
        for k in range(n_chip - 1):
            send(k, False).start()
        for k in range(n_chip - 1):
            send(k, True).wait_recv()
        for k in range(n_chip - 1):
            send(k, False).wait_send()
        for q in range(n_chip):
            swap(q).wait_send()
        keep.wait()

    slab = pltpu.VMEM((n_chip, r, cc), g.dtype)
    return pl.pallas_call(
        body, name=name, in_specs=[_ANY], out_specs=_ANY,
        out_shape=jax.ShapeDtypeStruct((n_chip, r, cc), g.dtype),
        scratch_shapes=[slab, slab, slab, pltpu.SemaphoreType.DMA((n_chip,)), pltpu.SemaphoreType.DMA((n_chip,)),
                        pltpu.SemaphoreType.DMA((n_chip,)), pltpu.SemaphoreType.DMA((n_chip - 1,)),
                        pltpu.SemaphoreType.DMA((n_chip - 1,)), pltpu.SemaphoreType.DMA],
        compiler_params=pltpu.CompilerParams(vmem_limit_bytes=VMEM_LIMIT),
    )(g)


def _ln_matmul_in_a(x, gain, w_in, shards, axes):
    s, d = x.shape
    f = w_in.shape[1]
    tm = min(BIG_ROW_TILE, s)
    n = len(shards)
    last = s // tm - 1

    def body(x_ref, g_ref, w_ref, *refs):
        u_ref, h_ref = refs[n:n + 2]
        start, wait = _exchange_ops(refs[:n], refs[n + 2:2 * n + 2], *refs[2 * n + 2:], False, axes)
        pl.when(pl.program_id(0) == 0)(start)
        _, xh = _rms_stats(x_ref[...])
        h = _mx(xh * g_ref[...])
        h_ref[...] = h
        u_ref[...] = _dot(h, w_ref[...])
        pl.when(pl.program_id(0) == last)(wait)

    outs = pl.pallas_call(
        body, name="ln_matmul_in_a", grid=(s // tm,),
        in_specs=[_row_spec(tm, d), _full_spec((1, d)), _full_spec((d, f))] + [_ANY] * n,
        out_specs=[_row_spec(tm, f), _row_spec(tm, d)] + [_ANY] * n,
        out_shape=[jax.ShapeDtypeStruct((s, f), _F32), jax.ShapeDtypeStruct((s, d), _MXU_DTYPE)]
        + _exchange_shapes(shards, False, axes),
        scratch_shapes=_exchange_sems(n),
        compiler_params=_params(1),
    )(x, gain, w_in, *shards)
    return outs[0], outs[1], outs[2:]


def _shift_rows_down(p, before, tm):
    row = lax.broadcasted_iota(jnp.int32, (8, p.shape[1]), 0)
    r1, r2 = pltpu.roll(p, 1, 0), pltpu.roll(p, 2, 0)
    top1 = jnp.where(row == 0, before[7:8, :], r1[0:8, :])
    top2 = jnp.where(row == 0, before[6:7, :], jnp.where(row == 1, before[7:8, :], r2[0:8, :]))
    return jnp.concatenate([top1, r1[8:, :]], axis=0), jnp.concatenate([top2, r2[8:, :]], axis=0)


def _shift_rows_up(p, after, tm):
    row = lax.broadcasted_iota(jnp.int32, (8, p.shape[1]), 0)
    r1, r2 = pltpu.roll(p, tm - 1, 0), pltpu.roll(p, tm - 2, 0)
    end1 = jnp.where(row == 7, after[0:1, :], r1[tm - 8:, :])
    end2 = jnp.where(row == 6, after[0:1, :], jnp.where(row == 7, after[1:2, :], r2[tm - 8:, :]))
    return jnp.concatenate([r1[:tm - 8, :], end1], axis=0), jnp.concatenate([r2[:tm - 8, :], end2], axis=0)


def _layer_a_out(u, x, conv, w_out_a, gain_kv, gain_b, w_kv, w_in_b):
    s, d = x.shape
    a = w_kv.shape[1] // 2
    tm = min(BIG_ROW_TILE, s)
    scale = HEAD_DIM ** -0.5

    def body(u_ref, x_ref, conv_ref, woa_ref, gkv_ref, gb_ref, wkv_ref, wib_ref,
             x1_ref, kv_ref, q_ref, gate_ref, halo_ref):
        _zero_at_first(pl.program_id(0), halo_ref)
        b, c, xin, g = (u_ref[:, k * d:(k + 1) * d] for k in range(4))
        p = c * xin
        p1, p2 = _shift_rows_down(p, halo_ref[...], tm)
        halo_ref[...] = p[tm - 8:tm, :]
        w = conv_ref[...]
        cv = w[0:1, :] * p2 + w[1:2, :] * p1 + w[2:3, :] * p
        ya = (b * cv) * (g * _sigmoid(g))
        x1 = x_ref[...] + _dot(_mx(ya), woa_ref[...])
        x1_ref[...] = x1
        _, xh = _rms_stats(x1)
        kv_ref[...] = _dot(_mx(xh * gkv_ref[...]), wkv_ref[...]).astype(kv_ref.dtype)
        qg = _dot(_mx(xh * gb_ref[...]), wib_ref[...])
        q_ref[...] = (qg[:, :a] * scale).astype(q_ref.dtype)
        gate_ref[...] = qg[:, a:]

    return pl.pallas_call(
        body, name="layer_a_out", grid=(s // tm,),
        in_specs=[_row_spec(tm, 4 * d), _row_spec(tm, d), _full_spec((8, d)), _full_spec((d, d)),
                  _full_spec((1, d)), _full_spec((1, d)), _full_spec((d, 2 * a)), _full_spec((d, 2 * a))],
        out_specs=[_row_spec(tm, d), _row_spec(tm, 2 * a), _row_spec(tm, a), _row_spec(tm, a)],
        out_shape=[jax.ShapeDtypeStruct((s, d), _F32), jax.ShapeDtypeStruct((s, 2 * a), _MXU_DTYPE),
                   jax.ShapeDtypeStruct((s, a), _MXU_DTYPE), jax.ShapeDtypeStruct((s, a), _F32)],
        scratch_shapes=[pltpu.VMEM((8, d), _F32)],
        compiler_params=_params(1),
    )(u, x, conv, w_out_a, gain_kv, gain_b, w_kv, w_in_b)


def _neg_softplus(z):
    return -(jnp.maximum(z, 0.0) + jnp.log(1.0 + jnp.exp(-jnp.abs(z))))


def _scan_dot(val, tri):
    return _dot(_mx(val), tri)


def _head_masks():
    lane = lax.broadcasted_iota(jnp.int32, (1, LANES), 1)
    return [lane < HEAD_DIM, lane >= HEAD_DIM]


def _stacked_causal(tq):
    row = lax.broadcasted_iota(jnp.int32, (2 * tq, tq), 0)
    col = lax.broadcasted_iota(jnp.int32, (2 * tq, tq), 1)
    return col < jnp.where(row >= tq, row - tq, row)


def _attn_fwd(q, kv, tri_suffix):
    s, a = q.shape
    n_hp = a // LANES
    tq = min(ATT_TILE, s)
    nq = s // tq
    per = min(ATT_TILES_PER_STEP, nq)

    def body(q_ref, k_ref, v_ref, tri_ref, o_ref, c_ref, stop_ref):
        hp = pl.program_id(0)
        tri = tri_ref[...]
        causal = _stacked_causal(tq)
        first = _head_masks()[0]

        def block(qs, j, c, acc, diag=False, live=None):
            off = pl.multiple_of(j * tq, tq)
            kb = k_ref[pl.ds(off, tq), :]
            vb = v_ref[pl.ds(off, tq), :]
            z = _dot_nt(qs, kb)
            lg = _neg_softplus(z)
            if diag:
                lg = jnp.where(causal, lg, 0.0)
            tot = jnp.sum(lg, axis=1, keepdims=True)
            c_in = c
            if live is not None:
                c_in = c + (live - 1.0) * 1e30
                tot = tot * live
            w = jnp.exp(z + _scan_dot(lg, tri) + c_in)
            if diag:
                w = jnp.where(causal, w, 0.0)
            return c + tot, acc + _dot(_mx(w), vb)

        tiles = []
        for t in range(per):
            i = pl.program_id(1) * per + t
            qv = q_ref[t * tq:(t + 1) * tq, :]
            qs = jnp.concatenate([jnp.where(hmask, qv, jnp.zeros_like(qv)) for hmask in _head_masks()], axis=0)
            c, acc = block(qs, i, jnp.zeros((2 * tq, 1), _F32), jnp.zeros((2 * tq, LANES), _F32), diag=True)
            c, acc = block(qs, jnp.maximum(i - 1, 0), c, acc, live=jnp.where(i >= 1, 1.0, 0.0))
            tiles.append((i, qs, c, acc))
        for t, (i, qs, c, acc) in enumerate(tiles):
            def cond(carry):
                return jnp.logical_and(carry[0] >= 0, jnp.max(carry[1]) > SKIP_LOG)

            def step(carry, qs=qs):
                c, acc = block(qs, carry[0], carry[1], carry[2])
                return carry[0] - 1, c, acc

            j, c, acc = lax.while_loop(cond, step, (i - 2, c, acc))
            stop_ref[hp, i] = jnp.maximum(jnp.minimum(j + 1, i - 1), 0)
            o_ref[t * tq:(t + 1) * tq, :] = jnp.where(first, acc[:tq], acc[tq:])
            c_ref[t * tq:(t + 1) * tq, :] = jnp.where(first, c[:tq], c[tq:])

    q_spec = pl.BlockSpec((per * tq, LANES), lambda h, i: (i, h))
    return pl.pallas_call(
        body, name="attn_fwd", grid=(n_hp, nq // per),
        in_specs=[q_spec, pl.BlockSpec((s, LANES), lambda h, i: (0, h)),
                  pl.BlockSpec((s, LANES), lambda h, i: (0, n_hp + h)), _full_spec((tq, tq))],
        out_specs=[q_spec, q_spec, pl.BlockSpec(memory_space=pltpu.SMEM)],
        out_shape=[jax.ShapeDtypeStruct((s, a), _F32), jax.ShapeDtypeStruct((s, a), _F32),
                   jax.ShapeDtypeStruct((n_hp, nq), jnp.int32)],
        compiler_params=_params(2),
    )(q, kv, kv, tri_suffix)


def _silu_and_grad(g):
    sg = _sigmoid(g)
    return g * sg, sg * (1.0 + g * (1.0 - sg))


def _layer_b_out_loss_bwd(o, gate, x1, w_out_b, gain_f, target):
    s, d = x1.shape
    a = o.shape[1]
    tm = min(BIG_ROW_TILE, s)
    last = s // tm - 1

    def body(o_ref, gate_ref, x1_ref, wob_ref, gf_ref, tgt_ref,
             dx2_ref, do_ref, dgate_ref, gw_ref, gnf_ref, loss_ref, gw_acc):
        step = pl.program_id(0)
        _zero_at_first(step, gw_acc, gnf_ref, loss_ref)
        sl, dsl = _silu_and_grad(gate_ref[...])
        ov = o_ref[...]
        ob = _mx(ov * sl)
        wob = wob_ref[...]
        x2 = x1_ref[...] + _dot(ob, wob)
        r, xh = _rms_stats(x2)
        gf = gf_ref[...]
        err = xh * gf - tgt_ref[...]
        part = jnp.sum(jnp.sum(err * err, axis=1, keepdims=True), axis=0, keepdims=True)
        loss_ref[...] += part * (0.5 / d)
        dy = err * (1.0 / d)
        gnf_ref[...] += jnp.sum(dy * xh, axis=0, keepdims=True)
        dx2 = _rms_bwd(r, xh, dy * gf)
        dx2_ref[...] = dx2
        dxb = _mx(dx2)
        d_ob = _dot_nt(dxb, wob)
        gw_acc[...] += _dot_tn(ob, dxb)
        do_ref[...] = (d_ob * sl).astype(do_ref.dtype)
        dgate_ref[...] = (d_ob * ov * dsl).astype(dgate_ref.dtype)
        _emit_at_last(step, last, [(gw_acc, gw_ref)])

    return pl.pallas_call(
        body, name="layer_b_out_loss_bwd", grid=(s // tm,),
        in_specs=[_row_spec(tm, a), _row_spec(tm, a), _row_spec(tm, d), _full_spec((a, d)), _full_spec((1, d)),
                  _row_spec(tm, d)],
        out_specs=[_row_spec(tm, d), _row_spec(tm, a), _row_spec(tm, a), _full_spec((a, d)), _full_spec((1, d)),
                   _full_spec((1, LANES))],
        out_shape=[jax.ShapeDtypeStruct((s, d), _F32), jax.ShapeDtypeStruct((s, a), _MXU_DTYPE),
                   jax.ShapeDtypeStruct((s, a), _MXU_DTYPE), jax.ShapeDtypeStruct((a, d), _MXU_DTYPE),
                   jax.ShapeDtypeStruct((1, d), _F32), jax.ShapeDtypeStruct((1, LANES), _F32)],
        scratch_shapes=[pltpu.VMEM((a, d), _F32)],
        compiler_params=_params(1),
    )(o, gate, x1, w_out_b, gain_f, target)


def _attn_bwd(q, kv, do, c_tot, stop, tri_suffix, tri_prefix):
    s, a = q.shape
    n_hp = a // LANES
    tq = min(ATT_TILE, s)
    nq = s // tq
    per = min(ATT_TILES_PER_STEP, nq)
    scale = HEAD_DIM ** -0.5
    chunk = min(1024, s)

    def body(stop_ref, q_ref, do_ref, c_ref, k_ref, v_ref, ts_ref, tp_ref, dq_ref, dk_hbm, dv_hbm,
             dk_acc, dv_acc, stage, sem):
        hp, step = pl.program_id(0), pl.program_id(1)
        _zero_at_first(step, dk_acc, dv_acc)
        ts, tp = ts_ref[...], tp_ref[...]
        causal = _stacked_causal(tq)
        masks = _head_masks()

        def block(tile, j, st, diag=False, live=None, parts=None):
            qs, dos, cs = tile
            asc, pre, dq = st
            rows = pl.ds(pl.multiple_of(j * tq, tq), tq)
            kb = k_ref[rows, :]
            vb = v_ref[rows, :]
            z = _dot_nt(qs, kb)
            lg = _neg_softplus(z)
            sig = jnp.exp(z + lg)
            if diag:
                lg = jnp.where(causal, lg, 0.0)
            tot = jnp.sum(lg, axis=1, keepdims=True)
            newer = jnp.zeros_like(tot) if diag else cs - asc - tot
            if live is not None:
                newer = newer + (live - 1.0) * 1e30
                tot = tot * live
            wgt = jnp.exp(z + _scan_dot(lg, ts) + newer)
            if diag:
                wgt = jnp.where(causal, wgt, 0.0)
            g = wgt * _dot_nt(dos, vb)
            dz = g - sig * (_scan_dot(g, tp) + pre)
            if diag:
                dz = jnp.where(causal, dz, 0.0)
            dzb = _mx(dz)
            dk_part, dv_part = _dot_tn(dzb, qs), _dot_tn(_mx(wgt), dos)
            if parts is None:
                dk_acc[rows, :] += dk_part
                dv_acc[rows, :] += dv_part
            else:
                parts.append((dk_part, dv_part))
            return asc + tot, pre + jnp.sum(g, axis=1, keepdims=True), dq + _dot(dzb, kb)

        tiles = []
        for t in range(per):
            i = step * per + t
            sl = slice(t * tq, (t + 1) * tq)
            qv, dov, cv = q_ref[sl, :], do_ref[sl, :], c_ref[sl, :]
            tile = (jnp.concatenate([jnp.where(hmask, qv, jnp.zeros_like(qv)) for hmask in masks], axis=0),
                    jnp.concatenate([jnp.where(hmask, dov, jnp.zeros_like(dov)) for hmask in masks], axis=0),
                    jnp.concatenate([cv[:, 0:1], cv[:, HEAD_DIM:HEAD_DIM + 1]], axis=0))
            first = jnp.clip(stop_ref[hp, i], 0, jnp.maximum(i - 1, 0))
            st = (jnp.zeros((2 * tq, 1), _F32), jnp.zeros((2 * tq, 1), _F32), jnp.zeros((2 * tq, LANES), _F32))
            st = lax.fori_loop(first, i - 1, lambda j, st, tile=tile: block(tile, j, st), st)
            tiles.append((i, sl, tile, st))
        before, diagonal = [], []
        states = [block(tile, jnp.maximum(i - 1, 0), st, live=jnp.where(i >= 1, 1.0, 0.0), parts=before)
                  for i, sl, tile, st in tiles]
        for (i, sl, tile, _), st in zip(tiles, states):
            dq = block(tile, i, st, diag=True, parts=diagonal)[2]
            dq_ref[sl, :] = (jnp.where(masks[0], dq[:tq], dq[tq:]) * scale).astype(dq_ref.dtype)
        parts = [p for pair in zip(before, diagonal) for p in pair]
        for u in range(per + 1):
            terms = ([parts[2 * u - 1]] if u >= 1 else []) + ([parts[2 * u]] if u < per else [])
            dk_sum, dv_sum = terms[0] if len(terms) == 1 else (terms[0][0] + terms[1][0], terms[0][1] + terms[1][1])
            rows = pl.ds(pl.multiple_of(jnp.maximum(step * per + u - 1, 0) * tq, tq), tq)
            dk_acc[rows, :] += dk_sum
            dv_acc[rows, :] += dv_sum

        @pl.when(step == nq // per - 1)
        def _():
            cols = pl.ds(pl.multiple_of(hp * LANES, LANES), LANES)
            for acc, out in ((dk_acc, dk_hbm), (dv_acc, dv_hbm)):
                def cast(n, carry, acc=acc):
                    rows = pl.ds(pl.multiple_of(n * chunk, chunk), chunk)
                    stage[rows, :] = acc[rows, :].astype(stage.dtype)
                    return carry
                lax.fori_loop(0, s // chunk, cast, 0)
                cp = pltpu.make_async_copy(stage, out.at[:, cols], sem)
                cp.start()
                cp.wait()

    q_spec = pl.BlockSpec((per * tq, LANES), lambda h, i, *_: (i, h))
    grid_spec = pltpu.PrefetchScalarGridSpec(
        num_scalar_prefetch=1, grid=(n_hp, nq // per),
        in_specs=[q_spec, q_spec, q_spec, pl.BlockSpec((s, LANES), lambda h, i, *_: (0, h)),
                  pl.BlockSpec((s, LANES), lambda h, i, *_: (0, n_hp + h)),
                  pl.BlockSpec((tq, tq), lambda h, i, *_: (0, 0)), pl.BlockSpec((tq, tq), lambda h, i, *_: (0, 0))],
        out_specs=[q_spec, pl.BlockSpec(memory_space=pl.ANY), pl.BlockSpec(memory_space=pl.ANY)],
        scratch_shapes=[pltpu.VMEM((s, LANES), _F32), pltpu.VMEM((s, LANES), _F32), pltpu.VMEM((s, LANES), _MXU_DTYPE),
                        pltpu.SemaphoreType.DMA],
    )
    return pl.pallas_call(
        body, name="attn_bwd", grid_spec=grid_spec,
        out_shape=[jax.ShapeDtypeStruct((s, a), _MXU_DTYPE)] * 3,
        compiler_params=_params(2),
    )(stop, q, do, c_tot, kv, kv, tri_suffix, tri_prefix)


def _proj_bwd(x1, dx2, dq, dgate, dk, dv, w_in_b, w_kv, gain_b, gain_kv):
    s, d = x1.shape
    a = dq.shape[1]
    tm = min(BIG_ROW_TILE, s)

    def body(x1_ref, dx2_ref, dq_ref, dgate_ref, dk_ref, dv_ref, wib_ref, wkv_ref, gb_ref, gkv_ref,
             dx1_ref, gwib_out, gwkv_out, gnb_ref, gnkv_ref, gwib_ref, gwkv_ref):
        step = pl.program_id(0)
        _zero_at_first(step, gwib_ref, gwkv_ref, gnb_ref, gnkv_ref)
        r, xh = _rms_stats(x1_ref[...])
        gb, gkv = gb_ref[...], gkv_ref[...]
        hb, hk = _mx(xh * gb), _mx(xh * gkv)
        d_qg = jnp.concatenate([dq_ref[...], dgate_ref[...]], axis=1)
        d_kv = jnp.concatenate([dk_ref[...], dv_ref[...]], axis=1)
        gwib_ref[...] += _dot_tn(hb, d_qg)
        gwkv_ref[...] += _dot_tn(hk, d_kv)
        d_hb = _dot_nt(d_qg, wib_ref[...])
        d_hk = _dot_nt(d_kv, wkv_ref[...])
        gnb_ref[...] += jnp.sum(d_hb * xh, axis=0, keepdims=True)
        gnkv_ref[...] += jnp.sum(d_hk * xh, axis=0, keepdims=True)
        dx1_ref[...] = dx2_ref[...] + _rms_bwd(r, xh, d_hb * gb + d_hk * gkv)
        _emit_at_last(step, s // tm - 1, [(gwib_ref, gwib_out), (gwkv_ref, gwkv_out)])

    return pl.pallas_call(
        body, name="proj_bwd", grid=(s // tm,),
        in_specs=[_row_spec(tm, d), _row_spec(tm, d)] + [_row_spec(tm, a)] * 4
        + [_full_spec((d, 2 * a)), _full_spec((d, 2 * a)), _full_spec((1, d)), _full_spec((1, d))],
        out_specs=[_row_spec(tm, d), _full_spec((d, 2 * a)), _full_spec((d, 2 * a)), _full_spec((1, d)),
                   _full_spec((1, d))],
        out_shape=[jax.ShapeDtypeStruct((s, d), _F32), jax.ShapeDtypeStruct((d, 2 * a), _MXU_DTYPE),
                   jax.ShapeDtypeStruct((d, 2 * a), _MXU_DTYPE), jax.ShapeDtypeStruct((1, d), _F32),
                   jax.ShapeDtypeStruct((1, d), _F32)],
        scratch_shapes=[pltpu.VMEM((d, 2 * a), _F32), pltpu.VMEM((d, 2 * a), _F32)],
        compiler_params=_params(1),
    )(x1, dx2, dq, dgate, dk, dv, w_in_b, w_kv, gain_b, gain_kv)


def _layer_a_bwd(u, dx1, conv, w_out_a):
    s, d = dx1.shape
    tm = min(BIG_ROW_TILE, s)
    n = s // tm
    per8 = tm // 8

    def body(u_ref, uprev_ref, dx1_ref, conv_ref, woa_ref, du_ref, gwoa_out, gconv_ref, halo_ref, gwoa_ref):
        step = pl.program_id(0)
        _zero_at_first(step, gwoa_ref, gconv_ref, halo_ref)
        b, c, xin, g = (u_ref[:, k * d:(k + 1) * d] for k in range(4))
        p = c * xin
        before = uprev_ref[:, d:2 * d] * uprev_ref[:, 2 * d:3 * d]
        before = jnp.where(step == n - 1, jnp.zeros_like(before), before)
        p1, p2 = _shift_rows_down(p, before, tm)
        w = conv_ref[...]
        cv = w[0:1, :] * p2 + w[1:2, :] * p1 + w[2:3, :] * p
        sl, dsl = _silu_and_grad(g)
        y = b * cv
        dxb = _mx(dx1_ref[...])
        gwoa_ref[...] += _dot_tn(_mx(y * sl), dxb)
        d_ya = _dot_nt(dxb, woa_ref[...])
        d_y = d_ya * sl
        d_cv = d_y * b
        n1, n2 = _shift_rows_up(d_cv, halo_ref[...], tm)
        halo_ref[...] = d_cv[0:8, :]
        d_p = w[2:3, :] * d_cv + w[1:2, :] * n1 + w[0:1, :] * n2
        gconv_ref[0:1, :] += jnp.sum(d_cv * p2, axis=0, keepdims=True)
        gconv_ref[1:2, :] += jnp.sum(d_cv * p1, axis=0, keepdims=True)
        gconv_ref[2:3, :] += jnp.sum(d_cv * p, axis=0, keepdims=True)
        du_ref[:, 0:d] = (d_y * cv).astype(du_ref.dtype)
        du_ref[:, d:2 * d] = (d_p * xin).astype(du_ref.dtype)
        du_ref[:, 2 * d:3 * d] = (d_p * c).astype(du_ref.dtype)
        du_ref[:, 3 * d:4 * d] = (d_ya * y * dsl).astype(du_ref.dtype)
        _emit_at_last(step, n - 1, [(gwoa_ref, gwoa_out)])

    def rev(i):
        return (n - 1 - i, 0)

    return pl.pallas_call(
        body, name="layer_a_bwd", grid=(n,),
        in_specs=[pl.BlockSpec((tm, 4 * d), rev),
                  pl.BlockSpec((8, 4 * d), lambda i: (jnp.maximum((n - 1 - i) * per8 - 1, 0), 0)),
                  pl.BlockSpec((tm, d), rev), _full_spec((8, d)), _full_spec((d, d))],
        out_specs=[pl.BlockSpec((tm, 4 * d), rev), _full_spec((d, d)), _full_spec((8, d))],
        out_shape=[jax.ShapeDtypeStruct((s, 4 * d), _MXU_DTYPE), jax.ShapeDtypeStruct((d, d), _MXU_DTYPE),
                   jax.ShapeDtypeStruct((8, d), _F32)],
        scratch_shapes=[pltpu.VMEM((8, d), _F32), pltpu.VMEM((d, d), _F32)],
        compiler_params=_params(1),
    )(u, u, dx1, conv, w_out_a)


def _grad_w_in_a(h, du, nb, grads, axes):
    s, d = h.shape
    bn = du.shape[1] // nb
    tm = min(4 * BIG_ROW_TILE, s)
    half = nb // 2
    n = len(grads)
    steps = s // tm

    def body(h_ref, du_ref, *refs):
        gw_out, gw_ref = refs[n], refs[-1]
        start, wait = _exchange_ops(refs[:n], refs[n + 1:2 * n + 1], *refs[2 * n + 1:-1], True, axes)
        jh, i = pl.program_id(0), pl.program_id(1)
        pl.when(jnp.logical_and(jh == 0, i == 0))(start)
        _zero_at_first(i, gw_ref)
        hv = h_ref[...]
        for j in range(half):
            gw_ref[j] += _dot_tn(hv, du_ref[:, j * bn:(j + 1) * bn])
        _emit_at_last(i, steps - 1, [(gw_ref, gw_out)])
        pl.when(jnp.logical_and(jh == 1, i == steps - 1))(wait)

    outs = pl.pallas_call(
        body, name="grad_w_in_a", grid=(2, steps),
        in_specs=[pl.BlockSpec((tm, d), lambda jh, i: (i, 0)), pl.BlockSpec((tm, half * bn), lambda jh, i: (i, jh))]
        + [_ANY] * n,
        out_specs=[pl.BlockSpec((half, d, bn), lambda jh, i: (jh, 0, 0))] + [_ANY] * n,
        out_shape=[jax.ShapeDtypeStruct((nb, d, bn), _MXU_DTYPE)] + _exchange_shapes(grads, True, axes),
        scratch_shapes=_exchange_sems(n) + [pltpu.VMEM((half, d, bn), _F32)],
        compiler_params=_params(2),
    )(h, du, *grads)
    return outs[0], outs[1:]


def _input_grad(x, dx1, du, w_in, gain):
    s, d = x.shape
    f = w_in.shape[1]
    tm = min(BIG_ROW_TILE, s)

    def body(x_ref, dx1_ref, du_ref, w_ref, g_ref, dx_ref, gn_ref):
        _zero_at_first(pl.program_id(0), gn_ref)
        r, xh = _rms_stats(x_ref[...])
        d_h = _dot_nt(du_ref[...], w_ref[...])
        gn_ref[...] += jnp.sum(d_h * xh, axis=0, keepdims=True)
        dx_ref[...] = dx1_ref[...] + _rms_bwd(r, xh, d_h * g_ref[...])

    return pl.pallas_call(
        body, name="input_grad", grid=(s // tm,),
        in_specs=[_row_spec(tm, d), _row_spec(tm, d), _row_spec(tm, f), _full_spec((d, f)), _full_spec((1, d))],
        out_specs=[_row_spec(tm, d), _full_spec((1, d))],
        out_shape=[jax.ShapeDtypeStruct((s, d), _F32), jax.ShapeDtypeStruct((1, d), _F32)],
        compiler_params=_params(1),
    )(x, dx1, du, w_in, gain)


def _reduce_adamw(name, parts, w, m, v):
    rows, cols = w.shape
    n_parts = parts.shape[0]
    tr = min(256, rows)
    c1 = 1.0 - ADAM_B1 ** ADAM_STEP
    c2 = 1.0 - ADAM_B2 ** ADAM_STEP

    def body(p_ref, w_ref, m_ref, v_ref, g_ref, d_ref, nm_ref, nv_ref):
        g = p_ref[0].astype(_F32)
        for k in range(1, n_parts):
            g = g + p_ref[k].astype(_F32)
        nm = ADAM_B1 * m_ref[...] + (1.0 - ADAM_B1) * g
        nv = ADAM_B2 * v_ref[...] + (1.0 - ADAM_B2) * (g * g)
        g_ref[...] = g
        nm_ref[...] = nm
        nv_ref[...] = nv
        d_ref[...] = -ADAM_LR * ((nm / c1) / (jnp.sqrt(nv / c2) + ADAM_EPS) + ADAM_WD * w_ref[...])

    tile = _row_spec(tr, cols)
    return pl.pallas_call(
        body, name=name, grid=(rows // tr,),
        in_specs=[pl.BlockSpec((n_parts, tr, cols), lambda i: (0, i, 0)), tile, tile, tile],
        out_specs=[tile] * 4,
        out_shape=[jax.ShapeDtypeStruct((rows, cols), _F32)] * 4,
        compiler_params=_params(1),
    )(parts, w, m, v)


def _pad_rows(a, rows=8):
    return jnp.pad(a, ((0, rows - a.shape[0]), (0, 0)))


def kernel(x, norm_a, w_in_a, conv_a, w_out_a, norm_kv, w_kv, norm_b, w_in_b, w_out_b, norm_f, loss_target, m_norm_a, m_w_in_a, m_conv_a, m_w_out_a, m_norm_kv, m_w_kv, m_norm_b, m_w_in_b, m_w_out_b, m_norm_f, v_norm_a, v_w_in_a, v_conv_a, v_w_out_a, v_norm_kv, v_w_kv, v_norm_b, v_w_in_b, v_w_out_b, v_norm_f):
    x0 = x[0]
    s, d = x0.shape
    a = d // 2
    sh = d // N_DEV
    me = 4 * lax.axis_index("x") + 2 * lax.axis_index("y") + lax.axis_index("c")

    small_a = _pad_rows(jnp.concatenate([norm_a, conv_a[0]], axis=0))
    wia_g, small_g = _gather_two_level("exchange_gather", [_mx(w_in_a[0]), small_a], axes=[1, 1])
    small_f = small_g.reshape(8, d)
    gain_a = small_f[0:1]
    conv_f = _pad_rows(small_f[1:4])
    gain_kv, gain_b, gain_f = norm_kv.reshape(1, d), norm_b.reshape(1, d), norm_f.reshape(1, d)

    wia_f = wia_g.reshape(d, 4 * d)
    u, h, (woa_g, wkv_g, wib_g, wob_g) = _ln_matmul_in_a(
        x0, gain_a, wia_f, [_mx(w_out_a[0]), _mx(w_kv), _mx(w_in_b[0]), _mx(w_out_b[0])], axes=[0, 0, 0, 1])
    woa_f = woa_g.reshape(d, d)
    wkv_f = wkv_g.reshape(d, 2 * a)
    wib_f = wib_g.reshape(d, 2 * a)
    wob_f = wob_g.reshape(a, d)
    x1, kv, q, gate = _layer_a_out(u, x0, conv_f, woa_f, gain_kv, gain_b, wkv_f, wib_f)
    tq = min(ATT_TILE, s)
    idx = jnp.arange(tq)
    tri_suffix = _mx(idx[:, None] >= idx[None, :])
    tri_prefix = _mx(idx[:, None] <= idx[None, :])
    o, c_tot, stop = _attn_fwd(q, kv, tri_suffix)
    dx2, do, dgate, g_wob, g_norm_f, loss_part = _layer_b_out_loss_bwd(o, gate, x1, wob_f, gain_f, loss_target[0])

    dq, dk, dv = _attn_bwd(q, kv, do, c_tot, stop, tri_suffix, tri_prefix)
    dx1, g_wib, g_wkv, g_norm_b, g_norm_kv = _proj_bwd(x1, dx2, dq, dgate, dk, dv, wib_f, wkv_f, gain_b, gain_kv)
    du, g_woa, g_conv = _layer_a_bwd(u, dx1, conv_f, woa_f)
    dx0, g_norm_a = _input_grad(x0, dx1, du, wia_f, gain_a)

    g_wia, (p_woa, p_wkv, p_wib, p_wob) = _grad_w_in_a(
        h, du, N_DEV,
        [g_woa.reshape(N_DEV, sh, d), g_wkv.reshape(N_DEV, sh, 2 * a), g_wib.reshape(N_DEV, sh, 2 * a),
         g_wob.reshape(a, N_DEV, sh).transpose(1, 0, 2)], axes=[0, 0, 0, 0])
    p_wia = _reduce_scatter_two_level("exchange_scatter", g_wia)
    small_grads = jnp.concatenate(
        [g_norm_a, g_conv[0:3], g_norm_kv, g_norm_b, g_norm_f, jnp.pad(loss_part, ((0, 0), (0, d - LANES)))], axis=0)
    (p_small,) = _exchange("exchange_small", [small_grads], scatter=False, axes=[0])

    upd_wia = _reduce_adamw("adamw_w_in_a", p_wia, w_in_a[0], m_w_in_a[0], v_w_in_a[0])
    upd_woa = _reduce_adamw("adamw_w_out_a", p_woa, w_out_a[0], m_w_out_a[0], v_w_out_a[0])
    upd_wkv = _reduce_adamw("adamw_w_kv", p_wkv, w_kv, m_w_kv, v_w_kv)
    upd_wib = _reduce_adamw("adamw_w_in_b", p_wib, w_in_b[0], m_w_in_b[0], v_w_in_b[0])
    upd_wob = _reduce_adamw("adamw_w_out_b", p_wob, w_out_b[0], m_w_out_b[0], v_w_out_b[0])

    def rep(a1, a2, a3):
        return jnp.concatenate([jnp.zeros((4, d), _F32), a1.reshape(1, d), a2.reshape(1, d), a3.reshape(1, d),
                                jnp.zeros((1, d), _F32)], axis=0)

    upd_rep = _reduce_adamw("adamw_replicated", p_small, rep(norm_kv, norm_b, norm_f),
                            rep(m_norm_kv, m_norm_b, m_norm_f), rep(v_norm_kv, v_norm_b, v_norm_f))

    def mine(n1, cv1):
        return _pad_rows(jnp.concatenate([n1, cv1[0]], axis=0))

    p_mine = lax.dynamic_slice(p_small, (0, 0, me * sh), (N_DEV, 8, sh))
    upd_mine = _reduce_adamw("adamw_sharded_small", p_mine, mine(norm_a, conv_a), mine(m_norm_a, m_conv_a),
                             mine(v_norm_a, v_conv_a))

    loss = upd_rep[0][7, 0]
    groups = []
    for k in range(4):
        groups.append([
            upd_mine[k][0:1], upd_wia[k][None], upd_mine[k][1:4][None], upd_woa[k][None], upd_rep[k][4],
            upd_wkv[k], upd_rep[k][5:6], upd_wib[k][None], upd_wob[k][None], upd_rep[k][6]])
    return (loss, dx0[None], *groups[0], *groups[1], *groups[2], *groups[3])
```

```python
import jax
import jax.numpy as jnp
from jax import lax
from jax.experimental import pallas as pl
from jax.experimental.pallas import tpu as pltpu

_MXU_DTYPE = jnp.bfloat16
_F32 = jnp.float32

RMS_EPS = 1e-6
HEAD_DIM = 64
LANES = 128
N_DEV = 8
ATT_TILE = 256
ATT_TILES_PER_STEP = 8
ROW_TILE = 256
BIG_ROW_TILE = 512
VMEM_LIMIT = 60 * 1024 * 1024
SKIP_LOG = -110.0

ADAM_LR = 0.001
ADAM_B1 = 0.9
ADAM_B2 = 0.999
ADAM_EPS = 1e-08
ADAM_WD = 0.01
ADAM_STEP = 10

_NT = (((1,), (1,)), ((), ()))
_TN = (((0,), (0,)), ((), ()))


def _params(n_grid):
    return pltpu.CompilerParams(dimension_semantics=("arbitrary",) * n_grid, vmem_limit_bytes=VMEM_LIMIT)


def _dot(a, b):
    return jnp.dot(a, b, preferred_element_type=_F32)


def _dot_nt(a, b):
    return lax.dot_general(a, b, _NT, preferred_element_type=_F32)


def _dot_tn(a, b):
    return lax.dot_general(a, b, _TN, preferred_element_type=_F32)


def _mx(a):
    return a.astype(_MXU_DTYPE)


def _sigmoid(a):
    return 1.0 / (1.0 + jnp.exp(-a))


def _rms_stats(xv):
    r = lax.rsqrt(jnp.mean(xv * xv, axis=-1, keepdims=True) + RMS_EPS)
    return r, xv * r


def _rms_bwd(r, xh, dyg):
    return r * (dyg - xh * jnp.mean(dyg * xh, axis=-1, keepdims=True))


def _row_spec(tm, width):
    return pl.BlockSpec((tm, width), lambda i: (i, 0))


def _full_spec(shape):
    zeros = (0,) * len(shape)
    return pl.BlockSpec(shape, lambda *_: zeros)


def _zero_at_first(step, *refs):
    @pl.when(step == 0)
    def _():
        for ref in refs:
            ref[...] = jnp.zeros_like(ref)


def _emit_at_last(step, last, pairs):
    @pl.when(step == last)
    def _():
        for acc, out in pairs:
            out[...] = acc[...].astype(out.dtype)


def _slot(ref, idx, axis):
    if axis == 0:
        return ref.at[idx]
    width = ref.shape[1] // N_DEV
    return ref.at[:, pl.ds(pl.multiple_of(idx * width, width), width)]


def _exchange_ops(ins, outs, send_sems, recv_sems, local_sems, scatter, axes):
    n = len(ins)
    x, y, c = lax.axis_index("x"), lax.axis_index("y"), lax.axis_index("c")
    me = 4 * x + 2 * y + c

    def remote(t, m, landed):
        px = 1 - x if m & 4 else x
        py = 1 - y if m & 2 else y
        pc = 1 - c if m & 1 else c
        idx = 4 * px + 2 * py + pc
        k = t * (N_DEV - 1) + m - 1
        return pltpu.make_async_remote_copy(
            src_ref=_slot(ins[t], idx, axes[t]) if scatter else ins[t],
            dst_ref=_slot(outs[t], idx if landed else me, axes[t]),
            send_sem=send_sems.at[k], recv_sem=recv_sems.at[k],
            device_id=(px, py, pc), device_id_type=pl.DeviceIdType.MESH)

    def local(t):
        return pltpu.make_async_copy(_slot(ins[t], me, axes[t]) if scatter else ins[t], _slot(outs[t], me, axes[t]),
                                     local_sems.at[t])

    def start():
        for t in range(n):
            local(t).start()
        for m in range(1, N_DEV):
            for t in range(n):
                remote(t, m, False).start()

    def wait():
        for m in range(1, N_DEV):
            for t in range(n):
                remote(t, m, True).wait_recv()
        for m in range(1, N_DEV):
            for t in range(n):
                remote(t, m, False).wait_send()
        for t in range(n):
            local(t).wait()

    return start, wait


def _exchange_shapes(arrays, scatter, axes):
    if scatter:
        return [jax.ShapeDtypeStruct(a.shape, a.dtype) for a in arrays]
    return [jax.ShapeDtypeStruct((N_DEV,) + a.shape if ax == 0 else (a.shape[0], N_DEV * a.shape[1]), a.dtype)
            for a, ax in zip(arrays, axes)]


def _exchange_sems(n):
    return [pltpu.SemaphoreType.DMA((n * (N_DEV - 1),)), pltpu.SemaphoreType.DMA((n * (N_DEV - 1),)),
            pltpu.SemaphoreType.DMA((n,))]


_ANY = pl.BlockSpec(memory_space=pl.ANY)


def _exchange(name, arrays, scatter, axes):
    n = len(arrays)

    def body(*refs):
        start, wait = _exchange_ops(refs[:n], refs[n:2 * n], *refs[2 * n:], scatter, axes)
        start()
        wait()

    return pl.pallas_call(
        body, name=name, in_specs=[_ANY] * n, out_specs=[_ANY] * n,
        out_shape=_exchange_shapes(arrays, scatter, axes), scratch_shapes=_exchange_sems(n),
    )(*arrays)


def _gather_two_level(name, arrays, axes):
    n = len(arrays)

    def body(*refs):
        ins, outs = refs[:n], refs[n:2 * n]
        send_sems, recv_sems, local_sems = refs[2 * n:]
        x, y, c = lax.axis_index("x"), lax.axis_index("y"), lax.axis_index("c")
        me, sibling = (x, y, c), (x, y, 1 - c)
        chips = [(1 - x, y), (x, 1 - y), (1 - x, 1 - y)]

        def rows(t, dev):
            return _slot(outs[t], 4 * dev[0] + 2 * dev[1] + dev[2], axes[t])

        def copy(t, k, block, to, src=None):
            return pltpu.make_async_remote_copy(
                src_ref=rows(t, block) if src is None else src, dst_ref=rows(t, block),
                send_sem=send_sems.at[t * (N_DEV - 1) + k], recv_sem=recv_sems.at[t * (N_DEV - 1) + k],
                device_id=to, device_id_type=pl.DeviceIdType.MESH)

        sent = []
        for t in range(n):
            pltpu.make_async_copy(ins[t], rows(t, me), local_sems.at[t]).start()
            sent.append(copy(t, 0, me, sibling, src=ins[t]))
            sent += [copy(t, 1 + j, me, (*chip, c), src=ins[t]) for j, chip in enumerate(chips)]
        for cp in sent:
            cp.start()
        for j, chip in enumerate(chips):
            for t in range(n):
                copy(t, 1 + j, (*chip, c), me).wait_recv()
                passed = copy(t, 4 + j, (*chip, c), sibling)
                passed.start()
                sent.append(passed)
        for t in range(n):
            copy(t, 0, sibling, me).wait_recv()
            for j, chip in enumerate(chips):
                copy(t, 4 + j, (*chip, 1 - c), me).wait_recv()
        for cp in sent:
            cp.wait_send()
        for t in range(n):
            pltpu.make_async_copy(ins[t], rows(t, me), local_sems.at[t]).wait()

    return pl.pallas_call(
        body, name=name, in_specs=[_ANY] * n, out_specs=[_ANY] * n,
        out_shape=_exchange_shapes(arrays, False, axes), scratch_shapes=_exchange_sems(n),
    )(*arrays)


def _reduce_scatter_two_level(name, g):
    n_chip = N_DEV // 2
    _, r, cc = g.shape
    chunk = min(256, r)

    def body(g_ref, out_ref, mine, theirs, total, load_sems, pair_send, pair_recv, chip_send, chip_recv, keep_sem):
        x, y, c = lax.axis_index("x"), lax.axis_index("y"), lax.axis_index("c")
        my_chip = 2 * x + y

        def load(q):
            return pltpu.make_async_copy(g_ref.at[2 * q + c], mine.at[q], load_sems.at[q])

        def swap(q):
            return pltpu.make_async_remote_copy(
                src_ref=g_ref.at[2 * q + 1 - c], dst_ref=theirs.at[q], send_sem=pair_send.at[q],
                recv_sem=pair_recv.at[q], device_id=(x, y, 1 - c), device_id_type=pl.DeviceIdType.MESH)

        for q in range(n_chip):
            load(q).start()
            swap(q).start()
        for q in range(n_chip):
            load(q).wait()
            swap(q).wait_recv()

            def add(i, carry, q=q):
                rows = pl.ds(pl.multiple_of(i * chunk, chunk), chunk)
                total[q, rows, :] = (mine[q, rows, :].astype(_F32) + theirs[q, rows, :].astype(_F32)).astype(total.dtype)
                return carry
            lax.fori_loop(0, r // chunk, add, 0)

        def send(k, landed):
            px = 1 - x if k in (0, 2) else x
            py = 1 - y if k in (1, 2) else y
            peer_chip = 2 * px + py
            return pltpu.make_async_remote_copy(
                src_ref=total.at[peer_chip], dst_ref=out_ref.at[peer_chip if landed else my_chip],
                send_sem=chip_send.at[k], recv_sem=chip_recv.at[k],
                device_id=(px, py, c), device_id_type=pl.DeviceIdType.MESH)

        keep = pltpu.make_async_copy(total.at[my_chip], out_ref.at[my_chip], keep_sem)
        keep.start()
        for k in range(n_chip - 1):
            send(k, False).start()
        for k in range(n_chip - 1):
            send(k, True).wait_recv()
        for k in range(n_chip - 1):
            send(k, False).wait_send()
        for q in range(n_chip):
            swap(q).wait_send()
        keep.wait()

    slab = pltpu.VMEM((n_chip, r, cc), g.dtype)
    return pl.pallas_call(
        body, name=name, in_specs=[_ANY], out_specs=_ANY,
        out_shape=jax.ShapeDtypeStruct((n_chip, r, cc), g.dtype),
        scratch_shapes=[slab, slab, slab, pltpu.SemaphoreType.DMA((n_chip,)), pltpu.SemaphoreType.DMA((n_chip,)),
                        pltpu.SemaphoreType.DMA((n_chip,)), pltpu.SemaphoreType.DMA((n_chip - 1,)),
                        pltpu.SemaphoreType.DMA((n_chip - 1,)), pltpu.SemaphoreType.DMA],
        compiler_params=pltpu.CompilerParams(vmem_limit_bytes=VMEM_LIMIT),
    )(g)


def _ln_matmul_in_a(x, gain, w_in, shards, axes):
    s, d = x.shape
    f = w_in.shape[1]
    tm = min(BIG_ROW_TILE, s)
    n = len(shards)
    last = s // tm - 1

    def body(x_ref, g_ref, w_ref, *refs):
        u_ref, h_ref = refs[n:n + 2]
        start, wait = _exchange_ops(refs[:n], refs[n + 2:2 * n + 2], *refs[2 * n + 2:], False, axes)
        pl.when(pl.program_id(0) == 0)(start)
        _, xh = _rms_stats(x_ref[...])
        h = _mx(xh * g_ref[...])
        h_ref[...] = h
        u_ref[...] = _dot(h, w_ref[...])
        pl.when(pl.program_id(0) == last)(wait)

    outs = pl.pallas_call(
        body, name="ln_matmul_in_a", grid=(s // tm,),
        in_specs=[_row_spec(tm, d), _full_spec((1, d)), _full_spec((d, f))] + [_ANY] * n,
        out_specs=[_row_spec(tm, f), _row_spec(tm, d)] + [_ANY] * n,
        out_shape=[jax.ShapeDtypeStruct((s, f), _F32), jax.ShapeDtypeStruct((s, d), _MXU_DTYPE)]
        + _exchange_shapes(shards, False, axes),
        scratch_shapes=_exchange_sems(n),
        compiler_params=_params(1),
    )(x, gain, w_in, *shards)
    return outs[0], outs[1], outs[2:]


def _shift_rows_down(p, before, tm):
    row = lax.broadcasted_iota(jnp.int32, (8, p.shape[1]), 0)
    r1, r2 = pltpu.roll(p, 1, 0), pltpu.roll(p, 2, 0)
    top1 = jnp.where(row == 0, before[7:8, :], r1[0:8, :])
    top2 = jnp.where(row == 0, before[6:7, :], jnp.where(row == 1, before[7:8, :], r2[0:8, :]))
    return jnp.concatenate([top1, r1[8:, :]], axis=0), jnp.concatenate([top2, r2[8:, :]], axis=0)


def _shift_rows_up(p, after, tm):
    row = lax.broadcasted_iota(jnp.int32, (8, p.shape[1]), 0)
    r1, r2 = pltpu.roll(p, tm - 1, 0), pltpu.roll(p, tm - 2, 0)
    end1 = jnp.where(row == 7, after[0:1, :], r1[tm - 8:, :])
    end2 = jnp.where(row == 6, after[0:1, :], jnp.where(row == 7, after[1:2, :], r2[tm - 8:, :]))
    return jnp.concatenate([r1[:tm - 8, :], end1], axis=0), jnp.concatenate([r2[:tm - 8, :], end2], axis=0)


def _layer_a_out(u, x, conv, w_out_a, gain_kv, gain_b, w_kv, w_in_b):
    s, d = x.shape
    a = w_kv.shape[1] // 2
    tm = min(BIG_ROW_TILE, s)
    scale = HEAD_DIM ** -0.5

    def body(u_ref, x_ref, conv_ref, woa_ref, gkv_ref, gb_ref, wkv_ref, wib_ref,
             x1_ref, kv_ref, q_ref, gate_ref, halo_ref):
        _zero_at_first(pl.program_id(0), halo_ref)
        b, c, xin, g = (u_ref[:, k * d:(k + 1) * d] for k in range(4))
        p = c * xin
        p1, p2 = _shift_rows_down(p, halo_ref[...], tm)
        halo_ref[...] = p[tm - 8:tm, :]
        w = conv_ref[...]
        cv = w[0:1, :] * p2 + w[1:2, :] * p1 + w[2:3, :] * p
        ya = (b * cv) * (g * _sigmoid(g))
        x1 = x_ref[...] + _dot(_mx(ya), woa_ref[...])
        x1_ref[...] = x1
        _, xh = _rms_stats(x1)
        kv_ref[...] = _dot(_mx(xh * gkv_ref[...]), wkv_ref[...]).astype(kv_ref.dtype)
        qg = _dot(_mx(xh * gb_ref[...]), wib_ref[...])
        q_ref[...] = (qg[:, :a] * scale).astype(q_ref.dtype)
        gate_ref[...] = qg[:, a:]

    return pl.pallas_call(
        body, name="layer_a_out", grid=(s // tm,),
        in_specs=[_row_spec(tm, 4 * d), _row_spec(tm, d), _full_spec((8, d)), _full_spec((d, d)),
                  _full_spec((1, d)), _full_spec((1, d)), _full_spec((d, 2 * a)), _full_spec((d, 2 * a))],
        out_specs=[_row_spec(tm, d), _row_spec(tm, 2 * a), _row_spec(tm, a), _row_spec(tm, a)],
        out_shape=[jax.ShapeDtypeStruct((s, d), _F32), jax.ShapeDtypeStruct((s, 2 * a), _MXU_DTYPE),
                   jax.ShapeDtypeStruct((s, a), _MXU_DTYPE), jax.ShapeDtypeStruct((s, a), _F32)],
        scratch_shapes=[pltpu.VMEM((8, d), _F32)],
        compiler_params=_params(1),
    )(u, x, conv, w_out_a, gain_kv, gain_b, w_kv, w_in_b)


def _neg_softplus(z):
    return -(jnp.maximum(z, 0.0) + jnp.log(1.0 + jnp.exp(-jnp.abs(z))))


def _scan_dot(val, tri):
    return _dot(_mx(val), tri)


def _head_masks():
    lane = lax.broadcasted_iota(jnp.int32, (1, LANES), 1)
    return [lane < HEAD_DIM, lane >= HEAD_DIM]


def _stacked_causal(tq):
    row = lax.broadcasted_iota(jnp.int32, (2 * tq, tq), 0)
    col = lax.broadcasted_iota(jnp.int32, (2 * tq, tq), 1)
    return col < jnp.where(row >= tq, row - tq, row)


def _attn_fwd(q, kv, tri_suffix):
    s, a = q.shape
    n_hp = a // LANES
    tq = min(ATT_TILE, s)
    nq = s // tq
    per = min(ATT_TILES_PER_STEP, nq)

    def body(q_ref, k_ref, v_ref, tri_ref, o_ref, c_ref, stop_ref):
        hp = pl.program_id(0)
        tri = tri_ref[...]
        causal = _stacked_causal(tq)
        first = _head_masks()[0]

        def block(qs, j, c, acc, diag=False, live=None):
            off = pl.multiple_of(j * tq, tq)
            kb = k_ref[pl.ds(off, tq), :]
            vb = v_ref[pl.ds(off, tq), :]
            z = _dot_nt(qs, kb)
            lg = _neg_softplus(z)
            if diag:
                lg = jnp.where(causal, lg, 0.0)
            tot = jnp.sum(lg, axis=1, keepdims=True)
            c_in = c
            if live is not None:
                c_in = c + (live - 1.0) * 1e30
                tot = tot * live
            w = jnp.exp(z + _scan_dot(lg, tri) + c_in)
            if diag:
                w = jnp.where(causal, w, 0.0)
            return c + tot, acc + _dot(_mx(w), vb)

        tiles = []
        for t in range(per):
            i = pl.program_id(1) * per + t
            qv = q_ref[t * tq:(t + 1) * tq, :]
            qs = jnp.concatenate([jnp.where(hmask, qv, jnp.zeros_like(qv)) for hmask in _head_masks()], axis=0)
            c, acc = block(qs, i, jnp.zeros((2 * tq, 1), _F32), jnp.zeros((2 * tq, LANES), _F32), diag=True)
            tiles.append((i, qs, c, acc))
        tiles = [(i, qs, *block(qs, jnp.maximum(i - 1, 0), c, acc, live=jnp.where(i >= 1, 1.0, 0.0)))
                 for i, qs, c, acc in tiles]
        for t, (i, qs, c, acc) in enumerate(tiles):
            def cond(carry):
                return jnp.logical_and(carry[0] >= 0, jnp.max(carry[1]) > SKIP_LOG)

            def step(carry, qs=qs):
                c, acc = block(qs, carry[0], carry[1], carry[2])
                return carry[0] - 1, c, acc

            j, c, acc = lax.while_loop(cond, step, (i - 2, c, acc))
            stop_ref[hp, i] = jnp.maximum(jnp.minimum(j + 1, i - 1), 0)
            o_ref[t * tq:(t + 1) * tq, :] = jnp.where(first, acc[:tq], acc[tq:])
            c_ref[t * tq:(t + 1) * tq, :] = jnp.where(first, c[:tq], c[tq:])

    q_spec = pl.BlockSpec((per * tq, LANES), lambda h, i: (i, h))
    return pl.pallas_call(
        body, name="attn_fwd", grid=(n_hp, nq // per),
        in_specs=[q_spec, pl.BlockSpec((s, LANES), lambda h, i: (0, h)),
                  pl.BlockSpec((s, LANES), lambda h, i: (0, n_hp + h)), _full_spec((tq, tq))],
        out_specs=[q_spec, q_spec, pl.BlockSpec(memory_space=pltpu.SMEM)],
        out_shape=[jax.ShapeDtypeStruct((s, a), _F32), jax.ShapeDtypeStruct((s, a), _F32),
                   jax.ShapeDtypeStruct((n_hp, nq), jnp.int32)],
        compiler_params=_params(2),
    )(q, kv, kv, tri_suffix)


def _silu_and_grad(g):
    sg = _sigmoid(g)
    return g * sg, sg * (1.0 + g * (1.0 - sg))


def _layer_b_out_loss_bwd(o, gate, x1, w_out_b, gain_f, target):
    s, d = x1.shape
    a = o.shape[1]
    tm = min(BIG_ROW_TILE, s)
    last = s // tm - 1

    def body(o_ref, gate_ref, x1_ref, wob_ref, gf_ref, tgt_ref,
             dx2_ref, do_ref, dgate_ref, gw_ref, gnf_ref, loss_ref, gw_acc):
        step = pl.program_id(0)
        _zero_at_first(step, gw_acc, gnf_ref, loss_ref)
        sl, dsl = _silu_and_grad(gate_ref[...])
        ov = o_ref[...]
        ob = _mx(ov * sl)
        wob = wob_ref[...]
        x2 = x1_ref[...] + _dot(ob, wob)
        r, xh = _rms_stats(x2)
        gf = gf_ref[...]
        err = xh * gf - tgt_ref[...]
        part = jnp.sum(jnp.sum(err * err, axis=1, keepdims=True), axis=0, keepdims=True)
        loss_ref[...] += part * (0.5 / d)
        dy = err * (1.0 / d)
        gnf_ref[...] += jnp.sum(dy * xh, axis=0, keepdims=True)
        dx2 = _rms_bwd(r, xh, dy * gf)
        dx2_ref[...] = dx2
        dxb = _mx(dx2)
        d_ob = _dot_nt(dxb, wob)
        gw_acc[...] += _dot_tn(ob, dxb)
        do_ref[...] = (d_ob * sl).astype(do_ref.dtype)
        dgate_ref[...] = (d_ob * ov * dsl).astype(dgate_ref.dtype)
        _emit_at_last(step, last, [(gw_acc, gw_ref)])

    return pl.pallas_call(
        body, name="layer_b_out_loss_bwd", grid=(s // tm,),
        in_specs=[_row_spec(tm, a), _row_spec(tm, a), _row_spec(tm, d), _full_spec((a, d)), _full_spec((1, d)),
                  _row_spec(tm, d)],
        out_specs=[_row_spec(tm, d), _row_spec(tm, a), _row_spec(tm, a), _full_spec((a, d)), _full_spec((1, d)),
                   _full_spec((1, LANES))],
        out_shape=[jax.ShapeDtypeStruct((s, d), _F32), jax.ShapeDtypeStruct((s, a), _MXU_DTYPE),
                   jax.ShapeDtypeStruct((s, a), _MXU_DTYPE), jax.ShapeDtypeStruct((a, d), _MXU_DTYPE),
                   jax.ShapeDtypeStruct((1, d), _F32), jax.ShapeDtypeStruct((1, LANES), _F32)],
        scratch_shapes=[pltpu.VMEM((a, d), _F32)],
        compiler_params=_params(1),
    )(o, gate, x1, w_out_b, gain_f, target)


def _attn_bwd(q, kv, do, c_tot, stop, tri_suffix, tri_prefix):
    s, a = q.shape
    n_hp = a // LANES
    tq = min(ATT_TILE, s)
    nq = s // tq
    per = min(ATT_TILES_PER_STEP, nq)
    scale = HEAD_DIM ** -0.5
    chunk = min(1024, s)

    def body(stop_ref, q_ref, do_ref, c_ref, k_ref, v_ref, ts_ref, tp_ref, dq_ref, dk_hbm, dv_hbm,
             dk_acc, dv_acc, stage, sem):
        hp, step = pl.program_id(0), pl.program_id(1)
        _zero_at_first(step, dk_acc, dv_acc)
        ts, tp = ts_ref[...], tp_ref[...]
        causal = _stacked_causal(tq)
        masks = _head_masks()

        def block(tile, j, st, diag=False, live=None, parts=None):
            qs, dos, cs = tile
            asc, pre, dq = st
            rows = pl.ds(pl.multiple_of(j * tq, tq), tq)
            kb = k_ref[rows, :]
            vb = v_ref[rows, :]
            z = _dot_nt(qs, kb)
            lg = _neg_softplus(z)
            sig = jnp.exp(z + lg)
            if diag:
                lg = jnp.where(causal, lg, 0.0)
            tot = jnp.sum(lg, axis=1, keepdims=True)
            newer = jnp.zeros_like(tot) if diag else cs - asc - tot
            if live is not None:
                newer = newer + (live - 1.0) * 1e30
                tot = tot * live
            wgt = jnp.exp(z + _scan_dot(lg, ts) + newer)
            if diag:
                wgt = jnp.where(causal, wgt, 0.0)
            g = wgt * _dot_nt(dos, vb)
            dz = g - sig * (_scan_dot(g, tp) + pre)
            if diag:
                dz = jnp.where(causal, dz, 0.0)
            dzb = _mx(dz)
            dk_part, dv_part = _dot_tn(dzb, qs), _dot_tn(_mx(wgt), dos)
            if parts is None:
                dk_acc[rows, :] += dk_part
                dv_acc[rows, :] += dv_part
            else:
                parts.append((dk_part, dv_part))
            return asc + tot, pre + jnp.sum(g, axis=1, keepdims=True), dq + _dot(dzb, kb)

        tiles = []
        for t in range(per):
            i = step * per + t
            sl = slice(t * tq, (t + 1) * tq)
            qv, dov, cv = q_ref[sl, :], do_ref[sl, :], c_ref[sl, :]
            tile = (jnp.concatenate([jnp.where(hmask, qv, jnp.zeros_like(qv)) for hmask in masks], axis=0),
                    jnp.concatenate([jnp.where(hmask, dov, jnp.zeros_like(dov)) for hmask in masks], axis=0),
                    jnp.concatenate([cv[:, 0:1], cv[:, HEAD_DIM:HEAD_DIM + 1]], axis=0))
            first = jnp.clip(stop_ref[hp, i], 0, jnp.maximum(i - 1, 0))
            st = (jnp.zeros((2 * tq, 1), _F32), jnp.zeros((2 * tq, 1), _F32), jnp.zeros((2 * tq, LANES), _F32))
            st = lax.fori_loop(first, i - 1, lambda j, st, tile=tile: block(tile, j, st), st)
            tiles.append((i, sl, tile, st))
        before, diagonal = [], []
        states = [block(tile, jnp.maximum(i - 1, 0), st, live=jnp.where(i >= 1, 1.0, 0.0), parts=before)
                  for i, sl, tile, st in tiles]
        for (i, sl, tile, _), st in zip(tiles, states):
            dq = block(tile, i, st, diag=True, parts=diagonal)[2]
            dq_ref[sl, :] = (jnp.where(masks[0], dq[:tq], dq[tq:]) * scale).astype(dq_ref.dtype)
        parts = [p for pair in zip(before, diagonal) for p in pair]
        for u in range(per + 1):
            terms = ([parts[2 * u - 1]] if u >= 1 else []) + ([parts[2 * u]] if u < per else [])
            dk_sum, dv_sum = terms[0] if len(terms) == 1 else (terms[0][0] + terms[1][0], terms[0][1] + terms[1][1])
            rows = pl.ds(pl.multiple_of(jnp.maximum(step * per + u - 1, 0) * tq, tq), tq)
            dk_acc[rows, :] += dk_sum
            dv_acc[rows, :] += dv_sum

        @pl.when(step == nq // per - 1)
        def _():
            cols = pl.ds(pl.multiple_of(hp * LANES, LANES), LANES)
            for acc, out in ((dk_acc, dk_hbm), (dv_acc, dv_hbm)):
                def cast(n, carry, acc=acc):
                    rows = pl.ds(pl.multiple_of(n * chunk, chunk), chunk)
                    stage[rows, :] = acc[rows, :].astype(stage.dtype)
                    return carry
                lax.fori_loop(0, s // chunk, cast, 0)
                cp = pltpu.make_async_copy(stage, out.at[:, cols], sem)
                cp.start()
                cp.wait()

    q_spec = pl.BlockSpec((per * tq, LANES), lambda h, i, *_: (i, h))
    grid_spec = pltpu.PrefetchScalarGridSpec(
        num_scalar_prefetch=1, grid=(n_hp, nq // per),
        in_specs=[q_spec, q_spec, q_spec, pl.BlockSpec((s, LANES), lambda h, i, *_: (0, h)),
                  pl.BlockSpec((s, LANES), lambda h, i, *_: (0, n_hp + h)),
                  pl.BlockSpec((tq, tq), lambda h, i, *_: (0, 0)), pl.BlockSpec((tq, tq), lambda h, i, *_: (0, 0))],
        out_specs=[q_spec, pl.BlockSpec(memory_space=pl.ANY), pl.BlockSpec(memory_space=pl.ANY)],
        scratch_shapes=[pltpu.VMEM((s, LANES), _F32), pltpu.VMEM((s, LANES), _F32), pltpu.VMEM((s, LANES), _MXU_DTYPE),
                        pltpu.SemaphoreType.DMA],
    )
    return pl.pallas_call(
        body, name="attn_bwd", grid_spec=grid_spec,
        out_shape=[jax.ShapeDtypeStruct((s, a), _MXU_DTYPE)] * 3,
        compiler_params=_params(2),
    )(stop, q, do, c_tot, kv, kv, tri_suffix, tri_prefix)


def _proj_bwd(x1, dx2, dq, dgate, dk, dv, w_in_b, w_kv, gain_b, gain_kv):
    s, d = x1.shape
    a = dq.shape[1]
    tm = min(BIG_ROW_TILE, s)

    def body(x1_ref, dx2_ref, dq_ref, dgate_ref, dk_ref, dv_ref, wib_ref, wkv_ref, gb_ref, gkv_ref,
             dx1_ref, gwib_out, gwkv_out, gnb_ref, gnkv_ref, gwib_ref, gwkv_ref):
        step = pl.program_id(0)
        _zero_at_first(step, gwib_ref, gwkv_ref, gnb_ref, gnkv_ref)
        r, xh = _rms_stats(x1_ref[...])
        gb, gkv = gb_ref[...], gkv_ref[...]
        hb, hk = _mx(xh * gb), _mx(xh * gkv)
        d_qg = jnp.concatenate([dq_ref[...], dgate_ref[...]], axis=1)
        d_kv = jnp.concatenate([dk_ref[...], dv_ref[...]], axis=1)
        gwib_ref[...] += _dot_tn(hb, d_qg)
        gwkv_ref[...] += _dot_tn(hk, d_kv)
        d_hb = _dot_nt(d_qg, wib_ref[...])
        d_hk = _dot_nt(d_kv, wkv_ref[...])
        gnb_ref[...] += jnp.sum(d_hb * xh, axis=0, keepdims=True)
        gnkv_ref[...] += jnp.sum(d_hk * xh, axis=0, keepdims=True)
        dx1_ref[...] = dx2_ref[...] + _rms_bwd(r, xh, d_hb * gb + d_hk * gkv)
        _emit_at_last(step, s // tm - 1, [(gwib_ref, gwib_out), (gwkv_ref, gwkv_out)])

    return pl.pallas_call(
        body, name="proj_bwd", grid=(s // tm,),
        in_specs=[_row_spec(tm, d), _row_spec(tm, d)] + [_row_spec(tm, a)] * 4
        + [_full_spec((d, 2 * a)), _full_spec((d, 2 * a)), _full_spec((1, d)), _full_spec((1, d))],
        out_specs=[_row_spec(tm, d), _full_spec((d, 2 * a)), _full_spec((d, 2 * a)), _full_spec((1, d)),
                   _full_spec((1, d))],
        out_shape=[jax.ShapeDtypeStruct((s, d), _F32), jax.ShapeDtypeStruct((d, 2 * a), _MXU_DTYPE),
                   jax.ShapeDtypeStruct((d, 2 * a), _MXU_DTYPE), jax.ShapeDtypeStruct((1, d), _F32),
                   jax.ShapeDtypeStruct((1, d), _F32)],
        scratch_shapes=[pltpu.VMEM((d, 2 * a), _F32), pltpu.VMEM((d, 2 * a), _F32)],
        compiler_params=_params(1),
    )(x1, dx2, dq, dgate, dk, dv, w_in_b, w_kv, gain_b, gain_kv)


def _layer_a_bwd(u, dx1, conv, w_out_a):
    s, d = dx1.shape
    tm = min(BIG_ROW_TILE, s)
    n = s // tm
    per8 = tm // 8

    def body(u_ref, uprev_ref, dx1_ref, conv_ref, woa_ref, du_ref, gwoa_out, gconv_ref, halo_ref, gwoa_ref):
        step = pl.program_id(0)
        _zero_at_first(step, gwoa_ref, gconv_ref, halo_ref)
        b, c, xin, g = (u_ref[:, k * d:(k + 1) * d] for k in range(4))
        p = c * xin
        before = uprev_ref[:, d:2 * d] * uprev_ref[:, 2 * d:3 * d]
        before = jnp.where(step == n - 1, jnp.zeros_like(before), before)
        p1, p2 = _shift_rows_down(p, before, tm)
        w = conv_ref[...]
        cv = w[0:1, :] * p2 + w[1:2, :] * p1 + w[2:3, :] * p
        sl, dsl = _silu_and_grad(g)
        y = b * cv
        dxb = _mx(dx1_ref[...])
        gwoa_ref[...] += _dot_tn(_mx(y * sl), dxb)
        d_ya = _dot_nt(dxb, woa_ref[...])
        d_y = d_ya * sl
        d_cv = d_y * b
        n1, n2 = _shift_rows_up(d_cv, halo_ref[...], tm)
        halo_ref[...] = d_cv[0:8, :]
        d_p = w[2:3, :] * d_cv + w[1:2, :] * n1 + w[0:1, :] * n2
        gconv_ref[0:1, :] += jnp.sum(d_cv * p2, axis=0, keepdims=True)
        gconv_ref[1:2, :] += jnp.sum(d_cv * p1, axis=0, keepdims=True)
        gconv_ref[2:3, :] += jnp.sum(d_cv * p, axis=0, keepdims=True)
        du_ref[:, 0:d] = (d_y * cv).astype(du_ref.dtype)
        du_ref[:, d:2 * d] = (d_p * xin).astype(du_ref.dtype)
        du_ref[:, 2 * d:3 * d] = (d_p * c).astype(du_ref.dtype)
        du_ref[:, 3 * d:4 * d] = (d_ya * y * dsl).astype(du_ref.dtype)
        _emit_at_last(step, n - 1, [(gwoa_ref, gwoa_out)])

    def rev(i):
        return (n - 1 - i, 0)

    return pl.pallas_call(
        body, name="layer_a_bwd", grid=(n,),
        in_specs=[pl.BlockSpec((tm, 4 * d), rev),
                  pl.BlockSpec((8, 4 * d), lambda i: (jnp.maximum((n - 1 - i) * per8 - 1, 0), 0)),
                  pl.BlockSpec((tm, d), rev), _full_spec((8, d)), _full_spec((d, d))],
        out_specs=[pl.BlockSpec((tm, 4 * d), rev), _full_spec((d, d)), _full_spec((8, d))],
        out_shape=[jax.ShapeDtypeStruct((s, 4 * d), _MXU_DTYPE), jax.ShapeDtypeStruct((d, d), _MXU_DTYPE),
                   jax.ShapeDtypeStruct((8, d), _F32)],
        scratch_shapes=[pltpu.VMEM((8, d), _F32), pltpu.VMEM((d, d), _F32)],
        compiler_params=_params(1),
    )(u, u, dx1, conv, w_out_a)


def _grad_w_in_a(h, du, nb, grads, axes):
    s, d = h.shape
    bn = du.shape[1] // nb
    tm = min(4 * BIG_ROW_TILE, s)
    half = nb // 2
    n = len(grads)
    steps = s // tm

    def body(h_ref, du_ref, *refs):
        gw_out, gw_ref = refs[n], refs[-1]
        start, wait = _exchange_ops(refs[:n], refs[n + 1:2 * n + 1], *refs[2 * n + 1:-1], True, axes)
        jh, i = pl.program_id(0), pl.program_id(1)
        pl.when(jnp.logical_and(jh == 0, i == 0))(start)
        _zero_at_first(i, gw_ref)
        hv = h_ref[...]
        for j in range(half):
            gw_ref[j] += _dot_tn(hv, du_ref[:, j * bn:(j + 1) * bn])
        _emit_at_last(i, steps - 1, [(gw_ref, gw_out)])
        pl.when(jnp.logical_and(jh == 1, i == steps - 1))(wait)

    outs = pl.pallas_call(
        body, name="grad_w_in_a", grid=(2, steps),
        in_specs=[pl.BlockSpec((tm, d), lambda jh, i: (i, 0)), pl.BlockSpec((tm, half * bn), lambda jh, i: (i, jh))]
        + [_ANY] * n,
        out_specs=[pl.BlockSpec((half, d, bn), lambda jh, i: (jh, 0, 0))] + [_ANY] * n,
        out_shape=[jax.ShapeDtypeStruct((nb, d, bn), _MXU_DTYPE)] + _exchange_shapes(grads, True, axes),
        scratch_shapes=_exchange_sems(n) + [pltpu.VMEM((half, d, bn), _F32)],
        compiler_params=_params(2),
    )(h, du, *grads)
    return outs[0], outs[1:]


def _input_grad(x, dx1, du, w_in, gain):
    s, d = x.shape
    f = w_in.shape[1]
    tm = min(BIG_ROW_TILE, s)

    def body(x_ref, dx1_ref, du_ref, w_ref, g_ref, dx_ref, gn_ref):
        _zero_at_first(pl.program_id(0), gn_ref)
        r, xh = _rms_stats(x_ref[...])
        d_h = _dot_nt(du_ref[...], w_ref[...])
        gn_ref[...] += jnp.sum(d_h * xh, axis=0, keepdims=True)
        dx_ref[...] = dx1_ref[...] + _rms_bwd(r, xh, d_h * g_ref[...])

    return pl.pallas_call(
        body, name="input_grad", grid=(s // tm,),
        in_specs=[_row_spec(tm, d), _row_spec(tm, d), _row_spec(tm, f), _full_spec((d, f)), _full_spec((1, d))],
        out_specs=[_row_spec(tm, d), _full_spec((1, d))],
        out_shape=[jax.ShapeDtypeStruct((s, d), _F32), jax.ShapeDtypeStruct((1, d), _F32)],
        compiler_params=_params(1),
    )(x, dx1, du, w_in, gain)


def _reduce_adamw(name, parts, w, m, v):
    rows, cols = w.shape
    n_parts = parts.shape[0]
    tr = min(256, rows)
    c1 = 1.0 - ADAM_B1 ** ADAM_STEP
    c2 = 1.0 - ADAM_B2 ** ADAM_STEP

    def body(p_ref, w_ref, m_ref, v_ref, g_ref, d_ref, nm_ref, nv_ref):
        g = p_ref[0].astype(_F32)
        for k in range(1, n_parts):
            g = g + p_ref[k].astype(_F32)
        nm = ADAM_B1 * m_ref[...] + (1.0 - ADAM_B1) * g
        nv = ADAM_B2 * v_ref[...] + (1.0 - ADAM_B2) * (g * g)
        g_ref[...] = g
        nm_ref[...] = nm
        nv_ref[...] = nv
        d_ref[...] = -ADAM_LR * ((nm / c1) / (jnp.sqrt(nv / c2) + ADAM_EPS) + ADAM_WD * w_ref[...])

    tile = _row_spec(tr, cols)
    return pl.pallas_call(
        body, name=name, grid=(rows // tr,),
        in_specs=[pl.BlockSpec((n_parts, tr, cols), lambda i: (0, i, 0)), tile, tile, tile],
        out_specs=[tile] * 4,
        out_shape=[jax.ShapeDtypeStruct((rows, cols), _F32)] * 4,
        compiler_params=_params(1),
    )(parts, w, m, v)


def _pad_rows(a, rows=8):
    return jnp.pad(a, ((0, rows - a.shape[0]), (0, 0)))


def kernel(x, norm_a, w_in_a, conv_a, w_out_a, norm_kv, w_kv, norm_b, w_in_b, w_out_b, norm_f, loss_target, m_norm_a, m_w_in_a, m_conv_a, m_w_out_a, m_norm_kv, m_w_kv, m_norm_b, m_w_in_b, m_w_out_b, m_norm_f, v_norm_a, v_w_in_a, v_conv_a, v_w_out_a, v_norm_kv, v_w_kv, v_norm_b, v_w_in_b, v_w_out_b, v_norm_f):
    x0 = x[0]
    s, d = x0.shape
    a = d // 2
    sh = d // N_DEV
    me = 4 * lax.axis_index("x") + 2 * lax.axis_index("y") + lax.axis_index("c")

    small_a = _pad_rows(jnp.concatenate([norm_a, conv_a[0]], axis=0))
    wia_g, small_g = _gather_two_level("exchange_gather", [_mx(w_in_a[0]), small_a], axes=[1, 1])
    small_f = small_g.reshape(8, d)
    gain_a = small_f[0:1]
    conv_f = _pad_rows(small_f[1:4])
    gain_kv, gain_b, gain_f = norm_kv.reshape(1, d), norm_b.reshape(1, d), norm_f.reshape(1, d)

    wia_f = wia_g.reshape(d, 4 * d)
    u, h, (woa_g, wkv_g, wib_g, wob_g) = _ln_matmul_in_a(
        x0, gain_a, wia_f, [_mx(w_out_a[0]), _mx(w_kv), _mx(w_in_b[0]), _mx(w_out_b[0])], axes=[0, 0, 0, 1])
    woa_f = woa_g.reshape(d, d)
    wkv_f = wkv_g.reshape(d, 2 * a)
    wib_f = wib_g.reshape(d, 2 * a)
    wob_f = wob_g.reshape(a, d)
    x1, kv, q, gate = _layer_a_out(u, x0, conv_f, woa_f, gain_kv, gain_b, wkv_f, wib_f)
    tq = min(ATT_TILE, s)
    idx = jnp.arange(tq)
    tri_suffix = _mx(idx[:, None] >= idx[None, :])
    tri_prefix = _mx(idx[:, None] <= idx[None, :])
    o, c_tot, stop = _attn_fwd(q, kv, tri_suffix)
    dx2, do, dgate, g_wob, g_norm_f, loss_part = _layer_b_out_loss_bwd(o, gate, x1, wob_f, gain_f, loss_target[0])

    dq, dk, dv = _attn_bwd(q, kv, do, c_tot, stop, tri_suffix, tri_prefix)
    dx1, g_wib, g_wkv, g_norm_b, g_norm_kv = _proj_bwd(x1, dx2, dq, dgate, dk, dv, wib_f, wkv_f, gain_b, gain_kv)
    du, g_woa, g_conv = _layer_a_bwd(u, dx1, conv_f, woa_f)
    dx0, g_norm_a = _input_grad(x0, dx1, du, wia_f, gain_a)

    g_wia, (p_woa, p_wkv, p_wib, p_wob) = _grad_w_in_a(
        h, du, N_DEV,
        [g_woa.reshape(N_DEV, sh, d), g_wkv.reshape(N_DEV, sh, 2 * a), g_wib.reshape(N_DEV, sh, 2 * a),
         g_wob.reshape(a, N_DEV, sh).transpose(1, 0, 2)], axes=[0, 0, 0, 0])
    p_wia = _reduce_scatter_two_level("exchange_scatter", g_wia)
    small_grads = jnp.concatenate(
        [g_norm_a, g_conv[0:3], g_norm_kv, g_norm_b, g_norm_f, jnp.pad(loss_part, ((0, 0), (0, d - LANES)))], axis=0)
    (p_small,) = _exchange("exchange_small", [small_grads], scatter=False, axes=[0])

    upd_wia = _reduce_adamw("adamw_w_in_a", p_wia, w_in_a[0], m_w_in_a[0], v_w_in_a[0])
    upd_woa = _reduce_adamw("adamw_w_out_a", p_woa, w_out_a[0], m_w_out_a[0], v_w_out_a[0])
    upd_wkv = _reduce_adamw("adamw_w_kv", p_wkv, w_kv, m_w_kv, v_w_kv)
    upd_wib = _reduce_adamw("adamw_w_in_b", p_wib, w_in_b[0], m_w_in_b[0], v_w_in_b[0])
    upd_wob = _reduce_adamw("adamw_w_out_b", p_wob, w_out_b[0], m_w_out_b[0], v_w_out_b[0])

    def rep(a1, a2, a3):
        return jnp.concatenate([jnp.zeros((4, d), _F32), a1.reshape(1, d), a2.reshape(1, d), a3.reshape(1, d),
                                jnp.zeros((1, d), _F32)], axis=0)

    upd_rep = _reduce_adamw("adamw_replicated", p_small, rep(norm_kv, norm_b, norm_f),
                            rep(m_norm_kv, m_norm_b, m_norm_f), rep(v_norm_kv, v_norm_b, v_norm_f))

    def mine(n1, cv1):
        return _pad_rows(jnp.concatenate([n1, cv1[0]], axis=0))

    p_mine = lax.dynamic_slice(p_small, (0, 0, me * sh), (N_DEV, 8, sh))
    upd_mine = _reduce_adamw("adamw_sharded_small", p_mine, mine(norm_a, conv_a), mine(m_norm_a, m_conv_a),
                             mine(v_norm_a, v_conv_a))

    loss = upd_rep[0][7, 0]
    groups = []
    for k in range(4):
        groups.append([
            upd_mine[k][0:1], upd_wia[k][None], upd_mine[k][1:4][None], upd_woa[k][None], upd_rep[k][4],
            upd_wkv[k], upd_rep[k][5:6], upd_wib[k][None], upd_wob[k][None], upd_rep[k][6]])
    return (loss, dx0[None], *groups[0], *groups[1], *groups[2], *groups[3])
```
